```python
import math
import jax, jax.numpy as jnp
from jax import lax
import numpy as np

D_MODEL = 4096
BATCH = 8
SEQ = 2048
DEPTH = 2

BLOCK = 128
EPS = 1e-6
NEG = -1e30
A_HEADS = 16
A_Q_LORA = 1536
A_KV_LORA = 512
A_NOPE = 128
A_ROPE = 64
A_V = 128
A_WIDTH = A_HEADS * A_V
ROPE_THETA = 10000.0
B_HEADS = 32
B_KV_HEADS = 4
B_HD = 64
B_WIDTH = B_HEADS * B_HD
WINDOW = 128
C_HEADS = 16
C_HD = 128
C_WIDTH = C_HEADS * C_HD
REL_BUCKETS = 32
REL_MAX_DIST = 128

IN_SIZES = (A_Q_LORA, A_KV_LORA, A_ROPE, A_WIDTH,
            B_WIDTH, B_KV_HEADS * B_HD, B_KV_HEADS * B_HD, B_WIDTH,
            C_WIDTH, C_WIDTH, C_WIDTH, C_HEADS, C_WIDTH)
IN_DIM = sum(IN_SIZES)
N_BRANCH = 3

kernel_name = "hybrid_mla_swa_fox_gated_block"


def rms_norm(x, g):
    xf = x.astype(jnp.float32)
    y = xf * lax.rsqrt(jnp.mean(xf * xf, axis=-1, keepdims=True) + EPS)
    return (y * g.astype(jnp.float32)).astype(x.dtype)


def split_columns(h):
    parts, start = [], 0
    for n in IN_SIZES:
        parts.append(h[..., start:start + n])
        start += n
    return parts


def apply_rope(x, positions):
    half = x.shape[-1] // 2
    inv_freq = ROPE_THETA ** (-jnp.arange(half, dtype=jnp.float32) / half)
    ang = positions.astype(jnp.float32)[..., None] * inv_freq
    cos = jnp.cos(ang)[:, :, None, :]
    sin = jnp.sin(ang)[:, :, None, :]
    xf = x.astype(jnp.float32)
    x1, x2 = xf[..., :half], xf[..., half:]
    return jnp.concatenate([x1 * cos - x2 * sin, x2 * cos + x1 * sin], axis=-1).astype(x.dtype)


def t5_causal_bucket(dist):
    max_exact = REL_BUCKETS // 2
    d = jnp.maximum(dist, 0)
    large = max_exact + (jnp.log(jnp.maximum(d, 1).astype(jnp.float32) / max_exact)
                         / math.log(REL_MAX_DIST / max_exact)
                         * (REL_BUCKETS - max_exact)).astype(jnp.int32)
    large = jnp.minimum(large, REL_BUCKETS - 1)
    return jnp.where(d < max_exact, d, large)


def causal_block_attention(q, k, v, scale, cum_log_f=None):
    b, s, h, dk = q.shape
    dv = v.shape[-1]
    nblk = s // BLOCK
    q_blocks = q.reshape(b, nblk, BLOCK, h, dk).transpose(1, 0, 2, 3, 4)
    key_pos = jnp.arange(s)

    def one_block(i, q_i, cq_i):
        logits = jnp.einsum('bqhd,bshd->bhqs', q_i, k).astype(jnp.float32) * scale
        if cq_i is not None:
            logits = logits + cq_i[..., :, None] - cum_log_f[:, :, None, :]
        q_pos = i * BLOCK + jnp.arange(BLOCK)
        mask = key_pos[None, :] <= q_pos[:, None]
        logits = jnp.where(mask[None, None], logits, NEG)
        p = jax.nn.softmax(logits, axis=-1)
        return jnp.einsum('bhqs,bshd->bqhd', p.astype(v.dtype), v)

    idx = jnp.arange(nblk)
    if cum_log_f is None:
        out = lax.map(lambda a: one_block(a[0], a[1], None), (idx, q_blocks))
    else:
        cq_blocks = cum_log_f.reshape(b, h, nblk, BLOCK).transpose(2, 0, 1, 3)
        out = lax.map(lambda a: one_block(a[0], a[1], a[2]), (idx, q_blocks, cq_blocks))
    return out.transpose(1, 0, 2, 3, 4).reshape(b, s, h, dv)


def sliding_window_attention(q, k, v, sinks, rel_bias):
    b, s, h, d = q.shape
    kvh = k.shape[2]
    g = h // kvh
    nblk = s // BLOCK
    qb = q.reshape(b, nblk, BLOCK, kvh, g, d)
    kb = k.reshape(b, nblk, BLOCK, kvh, d)
    vb = v.reshape(b, nblk, BLOCK, kvh, d)
    pad = ((0, 0), (1, 0), (0, 0), (0, 0), (0, 0))
    k_band = jnp.concatenate([jnp.pad(kb, pad)[:, :-1], kb], axis=2)
    v_band = jnp.concatenate([jnp.pad(vb, pad)[:, :-1], vb], axis=2)
    logits = jnp.einsum('bnqkgd,bnskd->bnkgqs', qb, k_band).astype(jnp.float32) * (d ** -0.5)
    logits = logits + rel_bias.astype(jnp.float32).reshape(kvh, g, BLOCK, 2 * BLOCK)
    dist = jnp.arange(BLOCK)[:, None] + BLOCK - jnp.arange(2 * BLOCK)[None, :]
    key_abs = jnp.arange(nblk)[:, None] * BLOCK - BLOCK + jnp.arange(2 * BLOCK)[None, :]
    valid = ((dist >= 0) & (dist < WINDOW))[None] & (key_abs >= 0)[:, None, :]
    logits = jnp.where(valid[None, :, None, None], logits, NEG)
    sink = jnp.broadcast_to(sinks.astype(jnp.float32).reshape(kvh, g)[None, None, :, :, None, None],
                            logits.shape[:-1] + (1,))
    p = jax.nn.softmax(jnp.concatenate([logits, sink], axis=-1), axis=-1)[..., :-1]
    out = jnp.einsum('bnkgqs,bnskd->bnqkgd', p.astype(v.dtype), v_band)
    return out.reshape(b, s, h * d)


def hybrid_layer(x, positions, rel_bias, pre_g, w_in, q_a_g, kv_a_g, w_uq, w_uk, w_uv,
                 sinks, b_f, w_proj_a, w_proj_b, w_proj_c, w_merge, w_o, post_g):
    b, s, _ = x.shape
    h = rms_norm(x, pre_g)
    proj = h @ w_in
    (a_cq, a_ckv, a_kr, a_z, b_q, b_k, b_v, b_z,
     c_q, c_k, c_v, c_f, c_z) = split_columns(proj)

    cq = rms_norm(a_cq, q_a_g)
    qa = jnp.einsum('bsr,rhd->bshd', cq, w_uq)
    q_a = jnp.concatenate([qa[..., :A_NOPE], apply_rope(qa[..., A_NOPE:], positions)], axis=-1)
    ckv = rms_norm(a_ckv, kv_a_g)
    k_nope = jnp.einsum('bsr,rhd->bshd', ckv, w_uk)
    v_a = jnp.einsum('bsr,rhd->bshd', ckv, w_uv)
    k_rope = apply_rope(a_kr[:, :, None, :], positions)
    k_a = jnp.concatenate([k_nope, jnp.broadcast_to(k_rope, (b, s, A_HEADS, A_ROPE))], axis=-1)
    o_a = causal_block_attention(q_a, k_a, v_a, (A_NOPE + A_ROPE) ** -0.5).reshape(b, s, A_WIDTH)
    o_a = o_a * jax.nn.silu(a_z)

    o_b = sliding_window_attention(b_q.reshape(b, s, B_HEADS, B_HD),
                                   b_k.reshape(b, s, B_KV_HEADS, B_HD),
                                   b_v.reshape(b, s, B_KV_HEADS, B_HD), sinks, rel_bias)
    o_b = o_b * jax.nn.silu(b_z)

    log_f = jax.nn.log_sigmoid((c_f + b_f).astype(jnp.float32))
    cum = jnp.cumsum(log_f, axis=1).transpose(0, 2, 1)
    o_c = causal_block_attention(c_q.reshape(b, s, C_HEADS, C_HD),
                                 c_k.reshape(b, s, C_HEADS, C_HD),
                                 c_v.reshape(b, s, C_HEADS, C_HD), C_HD ** -0.5, cum)
    o_c = o_c.reshape(b, s, C_WIDTH) * jax.nn.silu(c_z)

    gates = jax.nn.sigmoid(h @ w_merge)
    g_a, g_b, g_c = jnp.split(gates, N_BRANCH, axis=-1)
    y = g_a * (o_a @ w_proj_a) + g_b * (o_b @ w_proj_b) + g_c * (o_c @ w_proj_c)
    y = y @ w_o
    return x + rms_norm(y, post_g)


def setup_inputs(seed: int = 0) -> dict:
    key = jax.random.key(seed)
    ks = jax.random.split(key, 20)
    nrm = jax.random.normal
    f32 = jnp.float32
    x = nrm(ks[0], (BATCH, SEQ, D_MODEL), f32)
    positions = jnp.broadcast_to(jnp.arange(SEQ, dtype=jnp.int32)[None, :], (BATCH, SEQ))
    rel_table = 0.5 * nrm(ks[1], (REL_BUCKETS, B_HEADS), f32)
    pre_norm = 1.0 + 0.05 * nrm(ks[2], (DEPTH, D_MODEL), f32)
    w_in = nrm(ks[3], (DEPTH, D_MODEL, IN_DIM), f32) * D_MODEL ** -0.5
    q_a_norm = 1.0 + 0.05 * nrm(ks[4], (DEPTH, A_Q_LORA), f32)
    kv_a_norm = 1.0 + 0.05 * nrm(ks[5], (DEPTH, A_KV_LORA), f32)
    w_uq = nrm(ks[6], (DEPTH, A_Q_LORA, A_HEADS, A_NOPE + A_ROPE), f32) * A_Q_LORA ** -0.5
    w_uk = nrm(ks[7], (DEPTH, A_KV_LORA, A_HEADS, A_NOPE), f32) * A_KV_LORA ** -0.5
    w_uv = nrm(ks[8], (DEPTH, A_KV_LORA, A_HEADS, A_V), f32) * A_KV_LORA ** -0.5
    sinks = 0.5 * nrm(ks[9], (DEPTH, B_HEADS), f32)
    b_f = 1.0 + 0.1 * nrm(ks[10], (DEPTH, C_HEADS), f32)
    w_proj_a = nrm(ks[11], (DEPTH, A_WIDTH, D_MODEL), f32) * A_WIDTH ** -0.5
    w_proj_b = nrm(ks[12], (DEPTH, B_WIDTH, D_MODEL), f32) * B_WIDTH ** -0.5
    w_proj_c = nrm(ks[13], (DEPTH, C_WIDTH, D_MODEL), f32) * C_WIDTH ** -0.5
    w_merge = nrm(ks[14], (DEPTH, D_MODEL, N_BRANCH * D_MODEL), f32) * D_MODEL ** -0.5
    w_o = nrm(ks[15], (DEPTH, D_MODEL, D_MODEL), f32) * D_MODEL ** -0.5
    post_norm = 1.0 + 0.05 * nrm(ks[16], (DEPTH, D_MODEL), f32)
    return {"x": x, "positions": positions, "rel_table": rel_table, "pre_norm": pre_norm,
            "w_in": w_in, "q_a_norm": q_a_norm, "kv_a_norm": kv_a_norm, "w_uq": w_uq,
            "w_uk": w_uk, "w_uv": w_uv, "sinks": sinks, "b_f": b_f, "w_proj_a": w_proj_a,
            "w_proj_b": w_proj_b, "w_proj_c": w_proj_c, "w_merge": w_merge, "w_o": w_o,
            "post_norm": post_norm}


def reference(x, positions, rel_table, pre_norm, w_in, q_a_norm, kv_a_norm, w_uq, w_uk, w_uv,
              sinks, b_f, w_proj_a, w_proj_b, w_proj_c, w_merge, w_o, post_norm):
    dist = jnp.arange(BLOCK)[:, None] + BLOCK - jnp.arange(2 * BLOCK)[None, :]
    rel_bias = rel_table[t5_causal_bucket(dist)].transpose(2, 0, 1)
    for l in range(DEPTH):
        x = hybrid_layer(x, positions, rel_bias, pre_norm[l], w_in[l], q_a_norm[l], kv_a_norm[l],
                         w_uq[l], w_uk[l], w_uv[l], sinks[l], b_f[l], w_proj_a[l], w_proj_b[l],
                         w_proj_c[l], w_merge[l], w_o[l], post_norm[l])
    return x
```

```python
import functools
import math

import jax
import jax.numpy as jnp
from jax import lax
from jax.experimental import pallas as pl
from jax.experimental.pallas import tpu as pltpu

F32 = jnp.float32
BF16 = jnp.bfloat16

D = 4096
SEQ_BLOCK = 128
NORM_EPS = 1e-6
MASKED = -1e30
HA = 16
A_QR = 1536
A_KVR = 512
A_NOPE = 128
A_ROPE = 64
A_HALF = A_ROPE // 2
A_QPAD = 256
THETA = 10000.0
HB = 32
KVB = 4
DB = 64
GB = HB // KVB
HC = 16
DC = 128
N_BUCKETS = 32
MAX_DIST = 128

LANES = 128
CHUNK = 512

CH_ACQ, CH_ACKV, CH_BQ, CH_CQ, CH_CK, CH_CV = 0, 3, 4, 8, 12, 16
CH_BZ, CH_AZ, CH_CZ = 20, 24, 28
CH_GATE = 32
CH_BKV = 56
N_ID_CHUNKS = 20
N_SILU_CHUNKS = 12
N_CHUNKS = 57
NP = N_CHUNKS * CHUNK
F_LANE = 64

VMEM_CAP = 60 * 1024 * 1024


def _vmem(nbytes):
    return int(min(VMEM_CAP, nbytes))


def _params(sem, nbytes):
    return pltpu.CompilerParams(dimension_semantics=sem, vmem_limit_bytes=_vmem(nbytes))


def _rope_table_kernel(pos_ref, freq_ref, cos_ref, sin_ref):
    ang = pos_ref[...].astype(F32) * freq_ref[...]
    lane = lax.broadcasted_iota(jnp.int32, ang.shape, 1)
    live = lane < A_ROPE
    s = jnp.sin(ang)
    cos_ref[...] = jnp.where(live, jnp.cos(ang), 0.0)
    sin_ref[...] = jnp.where(live, jnp.where(lane < A_HALF, -s, s), 0.0)


def _rope_tables(positions):
    t = positions.size
    tm = 2048
    half = jnp.arange(A_HALF, dtype=F32)
    inv_freq = THETA ** (-half / A_HALF)
    freq = jnp.concatenate([inv_freq, inv_freq, jnp.zeros((LANES - A_ROPE,), F32)]).reshape(1, LANES)
    pos = positions.reshape(t, 1)
    return pl.pallas_call(
        _rope_table_kernel,
        grid=(t // tm,),
        in_specs=[pl.BlockSpec((tm, 1), lambda i: (i, 0)),
                  pl.BlockSpec((1, LANES), lambda i: (0, 0))],
        out_specs=[pl.BlockSpec((tm, LANES), lambda i: (i, 0)),
                   pl.BlockSpec((tm, LANES), lambda i: (i, 0))],
        out_shape=[jax.ShapeDtypeStruct((t, LANES), F32)] * 2,
        compiler_params=_params(("arbitrary",), 32 << 20),
        name="rope_tables",
    )(pos, freq)


def _rope(t, cos, sin):
    lane = lax.broadcasted_iota(jnp.int32, t.shape, 1)
    partner = jnp.where(lane < A_HALF,
                        pltpu.roll(t, LANES - A_HALF, 1),
                        pltpu.roll(t, A_HALF, 1))
    return t * cos + partner * sin


def _relbias_kernel(table_ref, o_ref):
    h = pl.program_id(0)
    qi = lax.broadcasted_iota(jnp.int32, (SEQ_BLOCK, 2 * SEQ_BLOCK), 0)
    si = lax.broadcasted_iota(jnp.int32, (SEQ_BLOCK, 2 * SEQ_BLOCK), 1)
    dist = qi + SEQ_BLOCK - si
    max_exact = N_BUCKETS // 2
    d = jnp.maximum(dist, 0)
    large = max_exact + (jnp.log(jnp.maximum(d, 1).astype(F32) / max_exact)
                         / math.log(MAX_DIST / max_exact)
                         * (N_BUCKETS - max_exact)).astype(jnp.int32)
    large = jnp.minimum(large, N_BUCKETS - 1)
    bucket = jnp.where(d < max_exact, d, large)
    acc = jnp.zeros(bucket.shape, F32)
    for b in range(N_BUCKETS):
        acc = jnp.where(bucket == b, table_ref[b, h], acc)
    o_ref[0] = acc


def _rel_bias(rel_table):
    def out_map(h):
        return (h // GB, (h % GB) // 2, h % 2)
    return pl.pallas_call(
        _relbias_kernel,
        grid=(HB,),
        in_specs=[pl.BlockSpec(memory_space=pltpu.SMEM)],
        out_specs=pl.BlockSpec((1, SEQ_BLOCK, 2 * SEQ_BLOCK), out_map),
        out_shape=jax.ShapeDtypeStruct((KVB, (GB // 2) * SEQ_BLOCK, 4 * SEQ_BLOCK), F32),
        compiler_params=_params(("arbitrary",), 16 << 20),
        name="rel_bias",
    )(rel_table)


def _prenorm_kernel(x_ref, g_ref, o_ref):
    x = x_ref[...]
    r = lax.rsqrt(jnp.mean(x * x, axis=-1, keepdims=True) + NORM_EPS)
    o_ref[...] = (x * r * g_ref[...]).astype(o_ref.dtype)


def _prenorm(x2, g):
    t = x2.shape[0]
    tm = 256
    return pl.pallas_call(
        _prenorm_kernel,
        grid=(t // tm,),
        in_specs=[pl.BlockSpec((tm, D), lambda i: (i, 0)),
                  pl.BlockSpec((1, D), lambda i: (0, 0))],
        out_specs=pl.BlockSpec((tm, D), lambda i: (i, 0)),
        out_shape=jax.ShapeDtypeStruct((t, D), BF16),
        compiler_params=_params(("arbitrary",), 32 << 20),
        name="prenorm",
    )(x2, g.reshape(1, D))


def _proj_kernel(h_ref, w_ref, cs_ref, o_ref, *, silu_start, gate_start, gate_end):
    j = pl.program_id(1)
    acc = jnp.dot(h_ref[...], w_ref[...], preferred_element_type=F32)

    @pl.when((j < silu_start) | (j >= gate_end))
    def _():
        o_ref[...] = (acc * cs_ref[...]).astype(o_ref.dtype)

    @pl.when((j >= silu_start) & (j < gate_start))
    def _():
        o_ref[...] = (acc * jax.nn.sigmoid(acc)).astype(o_ref.dtype)

    @pl.when((j >= gate_start) & (j < gate_end))
    def _():
        o_ref[...] = jax.nn.sigmoid(acc).astype(o_ref.dtype)


def _fused_proj(h, w_all, col_scale):
    t = h.shape[0]
    tm, tn = 1024, CHUNK
    kern = functools.partial(_proj_kernel, silu_start=CH_BZ, gate_start=CH_GATE, gate_end=CH_BKV)
    need = 2 * (tm * D * 2 + D * tn * 2 + tm * tn * 2) + 4 * tm * tn * 4
    return pl.pallas_call(
        kern,
        grid=(t // tm, NP // tn),
        in_specs=[pl.BlockSpec((tm, D), lambda i, j: (i, 0)),
                  pl.BlockSpec((D, tn), lambda i, j: (0, j)),
                  pl.BlockSpec((1, tn), lambda i, j: (0, j))],
        out_specs=pl.BlockSpec((tm, tn), lambda i, j: (i, j)),
        out_shape=jax.ShapeDtypeStruct((t, NP), BF16),
        compiler_params=_params(("parallel", "arbitrary"), need + (8 << 20)),
        name="fused_proj",
    )(h, w_all, col_scale)


def _misc_kernel(h_ref, w_ref, bf_ref, cos_ref, sin_ref, kr_ref, ccol_ref, crow_ref, carry_ref):
    t = pl.program_id(1)
    acc = jnp.dot(h_ref[...], w_ref[...], preferred_element_type=F32)
    kr_ref[...] = _rope(acc, cos_ref[...], sin_ref[...]).astype(kr_ref.dtype)

    xx = acc + bf_ref[...]
    log_f = jnp.minimum(xx, 0.0) - jnp.log1p(jnp.exp(-jnp.abs(xx)))
    hi = log_f.astype(BF16)
    r1 = log_f - hi.astype(F32)
    mid = r1.astype(BF16)
    lo = (r1 - mid.astype(F32)).astype(BF16)
    tm = acc.shape[0]
    ri = lax.broadcasted_iota(jnp.int32, (tm, tm), 0)
    ci = lax.broadcasted_iota(jnp.int32, (tm, tm), 1)
    tri = jnp.where(ri >= ci, 1.0, 0.0).astype(BF16)
    cs = (jnp.dot(tri, hi, preferred_element_type=F32)
          + jnp.dot(tri, mid, preferred_element_type=F32)
          + jnp.dot(tri, lo, preferred_element_type=F32))

    @pl.when(t == 0)
    def _():
        carry_ref[...] = jnp.zeros_like(carry_ref)

    cs = cs + carry_ref[...]
    carry_ref[...] = cs[tm - 1:tm, :]
    ccol_ref[...] = cs
    crow_ref[0] = cs.T


def _misc_proj(h, w_misc, bf_row, cos_t, sin_t, batch, seq):
    t = h.shape[0]
    tm = 512
    nt = seq // tm
    need = 2 * (tm * D * 2 + D * LANES * 2 + 5 * tm * LANES * 4) + 4 * tm * tm * 4
    return pl.pallas_call(
        _misc_kernel,
        grid=(batch, nt),
        in_specs=[pl.BlockSpec((tm, D), lambda b, i: (b * nt + i, 0)),
                  pl.BlockSpec((D, LANES), lambda b, i: (0, 0)),
                  pl.BlockSpec((1, LANES), lambda b, i: (0, 0)),
                  pl.BlockSpec((tm, LANES), lambda b, i: (b * nt + i, 0)),
                  pl.BlockSpec((tm, LANES), lambda b, i: (b * nt + i, 0))],
        out_specs=[pl.BlockSpec((tm, LANES), lambda b, i: (b * nt + i, 0)),
                   pl.BlockSpec((tm, LANES), lambda b, i: (b * nt + i, 0)),
                   pl.BlockSpec((1, LANES, tm), lambda b, i: (b, 0, i))],
        out_shape=[jax.ShapeDtypeStruct((t, LANES), BF16),
                   jax.ShapeDtypeStruct((t, LANES), F32),
                   jax.ShapeDtypeStruct((batch, LANES, seq), F32)],
        scratch_shapes=[pltpu.VMEM((1, LANES), F32)],
        compiler_params=_params(("arbitrary", "arbitrary"), need + (8 << 20)),
        name="misc_proj",
    )(h, w_misc, bf_row, cos_t, sin_t)


def _latent_kernel(*refs, rope, out_scale):
    if rope:
        x_ref, g_ref, w_ref, cos_ref, sin_ref, o_ref, xn_ref = refs
    else:
        x_ref, g_ref, w_ref, o_ref, xn_ref = refs

    @pl.when(pl.program_id(1) == 0)
    def _():
        x = x_ref[...].astype(F32)
        r = lax.rsqrt(jnp.mean(x * x, axis=-1, keepdims=True) + NORM_EPS)
        xn_ref[...] = (x * r * g_ref[...]).astype(xn_ref.dtype)

    acc = jnp.dot(xn_ref[...], w_ref[...], preferred_element_type=F32)
    if rope:
        cos = cos_ref[...]
        sin = sin_ref[...]
        for hh in range(acc.shape[1] // A_QPAD):
            base = hh * A_QPAD
            o_ref[:, base:base + A_NOPE] = (acc[:, base:base + A_NOPE] * out_scale).astype(o_ref.dtype)
            rot = _rope(acc[:, base + A_NOPE:base + A_QPAD], cos, sin)
            o_ref[:, base + A_NOPE:base + A_QPAD] = (rot * out_scale).astype(o_ref.dtype)
    else:
        o_ref[...] = acc.astype(o_ref.dtype)


def _latent_proj(proj, col_block, k, g, w, tables, out_scale):
    t = proj.shape[0]
    n = w.shape[1]
    tm, tn = 512, 512
    rope = tables is not None
    kern = functools.partial(_latent_kernel, rope=rope, out_scale=out_scale)
    in_specs = [pl.BlockSpec((tm, k), lambda i, j: (i, col_block)),
                pl.BlockSpec((1, k), lambda i, j: (0, 0)),
                pl.BlockSpec((k, tn), lambda i, j: (0, j))]
    args = [proj, g.reshape(1, k), w]
    if rope:
        in_specs += [pl.BlockSpec((tm, LANES), lambda i, j: (i, 0))] * 2
        args += list(tables)
    need = 2 * (tm * k * 2 + k * tn * 2 + tm * tn * 2 + 2 * tm * LANES * 4) + tm * k * 2 + 4 * tm * tn * 4
    return pl.pallas_call(
        kern,
        grid=(t // tm, n // tn),
        in_specs=in_specs,
        out_specs=pl.BlockSpec((tm, tn), lambda i, j: (i, j)),
        out_shape=jax.ShapeDtypeStruct((t, n), BF16),
        scratch_shapes=[pltpu.VMEM((tm, k), BF16)],
        compiler_params=_params(("parallel", "arbitrary"), need + (8 << 20)),
        name="latent_q" if rope else "latent_kv",
    )(*args)


def _flash_kernel(*refs, tq, fox):
    if fox:
        q_ref, k_ref, v_ref, z_ref, ccol_ref, crow_ref, o_ref = refs
        keys = k_ref
    else:
        q_ref, kn_ref, kr_ref, v_ref, z_ref, o_ref, kcat_ref = refs
        keys = kcat_ref
    h = pl.program_id(1)
    qi = pl.program_id(2)
    tk = tq

    if not fox:
        @pl.when(qi == 0)
        def _():
            kcat_ref[:, 0:LANES] = kn_ref[...]
            kcat_ref[:, LANES:2 * LANES] = kr_ref[...]

    q = q_ref[...]
    if fox:
        lane = lax.broadcasted_iota(jnp.int32, (tq, LANES), 1)
        cq = jnp.sum(jnp.where(lane == F_LANE + h, ccol_ref[...], 0.0), axis=1, keepdims=True)

    def block(j, carry, diagonal):
        m, l, acc = carry
        start = pl.multiple_of(j * tk, tk)
        k = keys[pl.ds(start, tk), :]
        s = lax.dot_general(q, k, (((1,), (1,)), ((), ())), preferred_element_type=F32)
        if fox:
            ck = crow_ref[0, pl.ds(F_LANE + h, 1), pl.ds(start, tk)]
            s = s + cq - ck
        if diagonal:
            row = lax.broadcasted_iota(jnp.int32, (tq, tk), 0)
            col = lax.broadcasted_iota(jnp.int32, (tq, tk), 1)
            s = jnp.where(row >= col, s, MASKED)
        m_new = jnp.maximum(m, jnp.max(s, axis=1, keepdims=True))
        alpha = jnp.exp(m - m_new)
        p = jnp.exp(s - m_new)
        l_new = alpha * l + jnp.sum(p, axis=1, keepdims=True)
        v = v_ref[pl.ds(start, tk), :]
        acc_new = alpha * acc + jnp.dot(p.astype(v.dtype), v, preferred_element_type=F32)
        return m_new, l_new, acc_new

    init = (jnp.full((tq, 1), MASKED, F32), jnp.zeros((tq, 1), F32),
            jnp.zeros((tq, v_ref.shape[1]), F32))
    carry = lax.fori_loop(0, qi, lambda j, c: block(j, c, False), init)
    _, l, acc = block(qi, carry, True)
    o_ref[...] = (acc / l * z_ref[...].astype(F32)).astype(o_ref.dtype)


def _attention_a(q_a, kv_a, k_rope, proj, batch, seq):
    t = q_a.shape[0]
    tq = 256
    nq = seq // tq
    kern = functools.partial(_flash_kernel, tq=tq, fox=False)
    z_col = CH_AZ * CHUNK // LANES
    need = 2 * (tq * A_QPAD * 2 + 3 * seq * LANES * 2 + 2 * tq * LANES * 2) + seq * A_QPAD * 2
    return pl.pallas_call(
        kern,
        grid=(batch, HA, nq),
        in_specs=[pl.BlockSpec((tq, A_QPAD), lambda b, h, i: (b * nq + i, h)),
                  pl.BlockSpec((seq, LANES), lambda b, h, i: (b, h)),
                  pl.BlockSpec((seq, LANES), lambda b, h, i: (b, 0)),
                  pl.BlockSpec((seq, LANES), lambda b, h, i: (b, HA + h)),
                  pl.BlockSpec((tq, LANES), lambda b, h, i: (b * nq + i, z_col + h))],
        out_specs=pl.BlockSpec((tq, LANES), lambda b, h, i: (b * nq + i, h)),
        out_shape=jax.ShapeDtypeStruct((t, HA * LANES), BF16),
        scratch_shapes=[pltpu.VMEM((seq, A_QPAD), BF16)],
        compiler_params=_params(("parallel", "arbitrary", "arbitrary"), need + (16 << 20)),
        name="attn_mla",
    )(q_a, kv_a, k_rope, kv_a, proj)


def _attention_c(proj, ccol, crow, batch, seq):
    t = proj.shape[0]
    tq = 256
    nq = seq // tq
    kern = functools.partial(_flash_kernel, tq=tq, fox=True)
    q_col, k_col, v_col, z_col = (c * CHUNK // LANES for c in (CH_CQ, CH_CK, CH_CV, CH_CZ))
    need = 2 * (3 * tq * LANES * 2 + 2 * seq * LANES * 2 + tq * LANES * 4 + LANES * seq * 4)
    return pl.pallas_call(
        kern,
        grid=(batch, HC, nq),
        in_specs=[pl.BlockSpec((tq, LANES), lambda b, h, i: (b * nq + i, q_col + h)),
                  pl.BlockSpec((seq, LANES), lambda b, h, i: (b, k_col + h)),
                  pl.BlockSpec((seq, LANES), lambda b, h, i: (b, v_col + h)),
                  pl.BlockSpec((tq, LANES), lambda b, h, i: (b * nq + i, z_col + h)),
                  pl.BlockSpec((tq, LANES), lambda b, h, i: (b * nq + i, 0)),
                  pl.BlockSpec((1, LANES, seq), lambda b, h, i: (b, 0, 0))],
        out_specs=pl.BlockSpec((tq, LANES), lambda b, h, i: (b * nq + i, h)),
        out_shape=jax.ShapeDtypeStruct((t, HC * DC), BF16),
        compiler_params=_params(("parallel", "arbitrary", "arbitrary"), need + (16 << 20)),
        name="attn_fox",
    )(proj, proj, proj, proj, ccol, crow)


def _swa_kernel(q_ref, kvp_ref, kvc_ref, z_ref, bias_ref, sink_ref, o_ref):
    n = pl.program_id(1)
    blk = SEQ_BLOCK
    band = 2 * blk
    rows = (GB // 2) * blk
    kv = jnp.concatenate([kvp_ref[...], kvc_ref[...]], axis=0).astype(F32)
    lane = lax.broadcasted_iota(jnp.int32, (band, LANES), 1)
    low = lane < DB

    def split(two_heads):
        lo = jnp.where(low, two_heads, 0.0)
        hi = jnp.where(low, 0.0, two_heads)
        return (lo, pltpu.roll(lo, DB, 1)), (pltpu.roll(hi, DB, 1), hi)

    k_heads = split(kv[:, 0:LANES]) + split(kv[:, LANES:2 * LANES])
    v_heads = split(kv[:, 2 * LANES:3 * LANES]) + split(kv[:, 3 * LANES:4 * LANES])

    r = lax.broadcasted_iota(jnp.int32, (rows, band), 0) % blk
    c = lax.broadcasted_iota(jnp.int32, (rows, band), 1)
    dist = r + blk - c
    first_key = jnp.where(n > 0, 0, blk)
    valid = (dist >= 0) & (dist < blk) & (c >= first_key)

    for kh in range(KVB):
        k_cat = jnp.concatenate(k_heads[kh], axis=0).astype(BF16)
        v_cat = jnp.concatenate(v_heads[kh], axis=0).astype(BF16)
        q4 = jnp.concatenate(
            [q_ref[:, (kh * (GB // 2) + jj) * LANES:(kh * (GB // 2) + jj + 1) * LANES]
             for jj in range(GB // 2)], axis=0)
        s = lax.dot_general(q4, k_cat, (((1,), (1,)), ((), ())), preferred_element_type=F32)
        s = s + bias_ref[kh]
        halves = []
        for e in range(2):
            se = jnp.where(valid, s[:, e * band:(e + 1) * band], MASKED)
            sink = sink_ref[kh, :, e:e + 1]
            m = jnp.maximum(jnp.max(se, axis=1, keepdims=True), sink)
            p = jnp.exp(se - m)
            denom = jnp.sum(p, axis=1, keepdims=True) + jnp.exp(sink - m)
            halves.append((p / denom).astype(BF16))
        p_cat = jnp.concatenate(halves, axis=1)
        o = jnp.dot(p_cat, v_cat, preferred_element_type=F32)
        for jj in range(GB // 2):
            c0 = (kh * (GB // 2) + jj) * LANES
            o_ref[:, c0:c0 + LANES] = (o[jj * blk:(jj + 1) * blk, :]
                                       * z_ref[:, c0:c0 + LANES].astype(F32)).astype(o_ref.dtype)


def _attention_b(proj, bias, sink_cols, batch, seq):
    t = proj.shape[0]
    blk = SEQ_BLOCK
    nb = seq // blk
    width = HB * DB
    q_col = CH_BQ * CHUNK // width
    kv_col = CH_BKV
    z_col = CH_BZ * CHUNK // width
    assert CH_BQ * CHUNK % width == 0 and CH_BZ * CHUNK % width == 0
    need = 2 * (3 * blk * width * 2 + 2 * blk * CHUNK * 2 + bias.size * 4 + sink_cols.size * 4)
    return pl.pallas_call(
        _swa_kernel,
        grid=(batch, nb),
        in_specs=[pl.BlockSpec((blk, width), lambda b, n: (b * nb + n, q_col)),
                  pl.BlockSpec((blk, CHUNK), lambda b, n: (b * nb + jnp.maximum(n - 1, 0), kv_col)),
                  pl.BlockSpec((blk, CHUNK), lambda b, n: (b * nb + n, kv_col)),
                  pl.BlockSpec((blk, width), lambda b, n: (b * nb + n, z_col)),
                  pl.BlockSpec(bias.shape, lambda b, n: (0, 0, 0)),
                  pl.BlockSpec(sink_cols.shape, lambda b, n: (0, 0, 0))],
        out_specs=pl.BlockSpec((blk, width), lambda b, n: (b * nb + n, 0)),
        out_shape=jax.ShapeDtypeStruct((t, width), BF16),
        compiler_params=_params(("parallel", "arbitrary"), need + (24 << 20)),
        name="attn_swa",
    )(proj, proj, proj, proj, bias, sink_cols)


def _merge_kernel(oa_ref, ob_ref, oc_ref, wa_ref, wb_ref, wc_ref, ga_ref, gb_ref, gc_ref, y_ref):
    y = ga_ref[...].astype(F32) * jnp.dot(oa_ref[...], wa_ref[...], preferred_element_type=F32)
    y = y + gb_ref[...].astype(F32) * jnp.dot(ob_ref[...], wb_ref[...], preferred_element_type=F32)
    y = y + gc_ref[...].astype(F32) * jnp.dot(oc_ref[...], wc_ref[...], preferred_element_type=F32)
    y_ref[...] = y.astype(y_ref.dtype)


def _merge_proj(oa, ob, oc, wa, wb, wc, proj):
    t, k = oa.shape
    tm, tn = 512, CHUNK
    gcols = D // tn
    o_spec = pl.BlockSpec((tm, k), lambda i, j: (i, 0))
    w_spec = pl.BlockSpec((k, tn), lambda i, j: (0, j))

    def gate_spec(branch):
        return pl.BlockSpec((tm, tn), lambda i, j: (i, CH_GATE + branch * gcols + j))

    need = 2 * (3 * tm * k * 2 + 3 * k * tn * 2 + 4 * tm * tn * 2) + 4 * tm * tn * 4
    return pl.pallas_call(
        _merge_kernel,
        grid=(t // tm, D // tn),
        in_specs=[o_spec, o_spec, o_spec, w_spec, w_spec, w_spec,
                  gate_spec(0), gate_spec(1), gate_spec(2)],
        out_specs=pl.BlockSpec((tm, tn), lambda i, j: (i, j)),
        out_shape=jax.ShapeDtypeStruct((t, D), BF16),
        compiler_params=_params(("parallel", "arbitrary"), need + (8 << 20)),
        name="merge_proj",
    )(oa, ob, oc, wa, wb, wc, proj, proj, proj)


def _out_kernel(y_ref, w_ref, x_ref, g_ref, o_ref, acc_ref, ssq_ref, *, nj):
    j = pl.program_id(1)

    @pl.when(j < nj)
    def _():
        r = jnp.dot(y_ref[...], w_ref[...], preferred_element_type=F32)
        acc_ref[j] = r
        ss = jnp.sum(r * r, axis=1, keepdims=True)

        @pl.when(j == 0)
        def _():
            ssq_ref[...] = ss

        @pl.when(j > 0)
        def _():
            ssq_ref[...] += ss

    @pl.when(j >= nj)
    def _():
        rs = lax.rsqrt(ssq_ref[...] * (1.0 / D) + NORM_EPS)
        o_ref[...] = x_ref[...] + acc_ref[j - nj] * rs * g_ref[...]


def _out_proj(y, w_o, x2, g):
    t = y.shape[0]
    tm, tn = 512, CHUNK
    nj = D // tn
    kern = functools.partial(_out_kernel, nj=nj)
    need = 2 * (tm * D * 2 + D * tn * 2 + 2 * tm * tn * 4) + tm * D * 4 + 2 * tm * tn * 4
    return pl.pallas_call(
        kern,
        grid=(t // tm, 2 * nj),
        in_specs=[pl.BlockSpec((tm, D), lambda i, j: (i, 0)),
                  pl.BlockSpec((D, tn), lambda i, j: (0, jnp.minimum(j, nj - 1))),
                  pl.BlockSpec((tm, tn), lambda i, j: (i, jnp.maximum(j - nj, 0))),
                  pl.BlockSpec((1, tn), lambda i, j: (0, jnp.maximum(j - nj, 0)))],
        out_specs=pl.BlockSpec((tm, tn), lambda i, j: (i, jnp.maximum(j - nj, 0))),
        out_shape=jax.ShapeDtypeStruct((t, D), F32),
        scratch_shapes=[pltpu.VMEM((nj, tm, tn), F32), pltpu.VMEM((tm, 1), F32)],
        compiler_params=_params(("parallel", "arbitrary"), need + (8 << 20)),
        name="out_proj",
    )(y, w_o, x2, g.reshape(1, D))


def _layer_weights(w_in, w_merge, w_uq, w_uk, w_uv, b_f):
    sizes = (A_QR, A_KVR, A_ROPE, HA * A_NOPE, HB * DB, KVB * DB, KVB * DB, HB * DB,
             HC * DC, HC * DC, HC * DC, HC, HC * DC)
    parts, start = [], 0
    for n in sizes:
        parts.append(w_in[:, start:start + n])
        start += n
    (a_cq, a_ckv, a_kr, a_z, b_q, b_k, b_v, b_z, c_q, c_k, c_v, c_f, c_z) = parts
    w_all = jnp.concatenate([a_cq, a_ckv, b_q, c_q, c_k, c_v, b_z, a_z, c_z, w_merge, b_k, b_v],
                            axis=1).astype(BF16)
    col_scale = jnp.ones((NP,), F32)
    col_scale = col_scale.at[CH_BQ * CHUNK:CH_CQ * CHUNK].set(DB ** -0.5)
    col_scale = col_scale.at[CH_CQ * CHUNK:CH_CK * CHUNK].set(DC ** -0.5)
    col_scale = col_scale.reshape(1, NP)

    pad = jnp.zeros((D, LANES - F_LANE - HC), w_in.dtype)
    w_misc = jnp.concatenate([a_kr, c_f, pad], axis=1).astype(BF16)
    bf_row = jnp.concatenate([jnp.zeros((F_LANE,), F32), b_f.astype(F32),
                              jnp.zeros((LANES - F_LANE - HC,), F32)]).reshape(1, LANES)

    wq = jnp.pad(w_uq, ((0, 0), (0, 0), (0, A_QPAD - A_NOPE - A_ROPE)))
    wq = wq.reshape(A_QR, HA * A_QPAD).astype(BF16)
    wkv = jnp.concatenate([w_uk.reshape(A_KVR, HA * A_NOPE), w_uv.reshape(A_KVR, HA * A_NOPE)],
                          axis=1).astype(BF16)
    return w_all, col_scale, w_misc, bf_row, wq, wkv


def _sink_columns(sinks):
    s = sinks.astype(F32).reshape(KVB, GB // 2, 1, 2)
    s = jnp.broadcast_to(s, (KVB, GB // 2, SEQ_BLOCK, 2)).reshape(KVB, (GB // 2) * SEQ_BLOCK, 2)
    return jnp.pad(s, ((0, 0), (0, 0), (0, LANES - 2)))


def kernel(x, positions, rel_table, pre_norm, w_in, q_a_norm, kv_a_norm, w_uq, w_uk, w_uv,
           sinks, b_f, w_proj_a, w_proj_b, w_proj_c, w_merge, w_o, post_norm):
    batch, seq, _ = x.shape
    depth = w_in.shape[0]
    x2 = x.reshape(batch * seq, D)
    tables = _rope_tables(positions)
    bias = _rel_bias(rel_table)
    for l in range(depth):
        w_all, col_scale, w_misc, bf_row, wq, wkv = _layer_weights(
            w_in[l], w_merge[l], w_uq[l], w_uk[l], w_uv[l], b_f[l])
        h = _prenorm(x2, pre_norm[l])
        proj = _fused_proj(h, w_all, col_scale)
        k_rope, ccol, crow = _misc_proj(h, w_misc, bf_row, tables[0], tables[1], batch, seq)
        q_a = _latent_proj(proj, 0, A_QR, q_a_norm[l], wq, tables, (A_NOPE + A_ROPE) ** -0.5)
        kv_a = _latent_proj(proj, CH_ACKV, A_KVR, kv_a_norm[l], wkv, None, 1.0)
        o_a = _attention_a(q_a, kv_a, k_rope, proj, batch, seq)
        o_b = _attention_b(proj, bias, _sink_columns(sinks[l]), batch, seq)
        o_c = _attention_c(proj, ccol, crow, batch, seq)
        y = _merge_proj(o_a, o_b, o_c, w_proj_a[l].astype(BF16), w_proj_b[l].astype(BF16),
                        w_proj_c[l].astype(BF16), proj)
        x2 = _out_proj(y, w_o[l].astype(BF16), x2, post_norm[l])
    return x2.reshape(batch, seq, D)
```

```python
import functools
import math

import jax
import jax.numpy as jnp
from jax import lax
from jax.experimental import pallas as pl
from jax.experimental.pallas import tpu as pltpu

F32 = jnp.float32
BF16 = jnp.bfloat16

D = 4096
SEQ_BLOCK = 128
NORM_EPS = 1e-6
MASKED = -1e30
LOG2E = math.log2(math.e)
HA = 16
A_QR = 1536
A_KVR = 512
A_NOPE = 128
A_ROPE = 64
A_HALF = A_ROPE // 2
A_QPAD = 256
THETA = 10000.0
HB = 32
KVB = 4
DB = 64
GB = HB // KVB
HC = 16
DC = 128
N_BUCKETS = 32
MAX_DIST = 128

LANES = 128
CHUNK = 512

CH_ACQ, CH_ACKV, CH_BQ, CH_CQ, CH_CK, CH_CV = 0, 3, 4, 8, 12, 16
CH_BZ, CH_AZ, CH_CZ = 20, 24, 28
CH_GATE = 32
CH_BKV = 56
N_ID_CHUNKS = 20
N_SILU_CHUNKS = 12
N_CHUNKS = 57
NP = N_CHUNKS * CHUNK
F_LANE = 64

VMEM_CAP = 60 * 1024 * 1024


def _vmem(nbytes):
    return int(min(VMEM_CAP, nbytes))


def _params(sem, nbytes):
    return pltpu.CompilerParams(dimension_semantics=sem, vmem_limit_bytes=_vmem(nbytes))


def _rope_table_kernel(pos_ref, freq_ref, cos_ref, sin_ref):
    ang = pos_ref[...].astype(F32) * freq_ref[...]
    lane = lax.broadcasted_iota(jnp.int32, ang.shape, 1)
    live = lane < A_ROPE
    s = jnp.sin(ang)
    cos_ref[...] = jnp.where(live, jnp.cos(ang), 0.0)
    sin_ref[...] = jnp.where(live, jnp.where(lane < A_HALF, -s, s), 0.0)


def _rope_tables(positions):
    t = positions.size
    tm = 2048
    half = jnp.arange(A_HALF, dtype=F32)
    inv_freq = THETA ** (-half / A_HALF)
    freq = jnp.concatenate([inv_freq, inv_freq, jnp.zeros((LANES - A_ROPE,), F32)]).reshape(1, LANES)
    pos = positions.reshape(t, 1)
    return pl.pallas_call(
        _rope_table_kernel,
        grid=(t // tm,),
        in_specs=[pl.BlockSpec((tm, 1), lambda i: (i, 0)),
                  pl.BlockSpec((1, LANES), lambda i: (0, 0))],
        out_specs=[pl.BlockSpec((tm, LANES), lambda i: (i, 0)),
                   pl.BlockSpec((tm, LANES), lambda i: (i, 0))],
        out_shape=[jax.ShapeDtypeStruct((t, LANES), F32)] * 2,
        compiler_params=_params(("arbitrary",), 32 << 20),
        name="rope_tables",
    )(pos, freq)


def _rope(t, cos, sin):
    lane = lax.broadcasted_iota(jnp.int32, t.shape, 1)
    partner = jnp.where(lane < A_HALF,
                        pltpu.roll(t, LANES - A_HALF, 1),
                        pltpu.roll(t, A_HALF, 1))
    return t * cos + partner * sin


def _relbias_kernel(table_ref, o_ref):
    h = pl.program_id(0)
    qi = lax.broadcasted_iota(jnp.int32, (SEQ_BLOCK, 2 * SEQ_BLOCK), 0)
    si = lax.broadcasted_iota(jnp.int32, (SEQ_BLOCK, 2 * SEQ_BLOCK), 1)
    dist = qi + SEQ_BLOCK - si
    max_exact = N_BUCKETS // 2
    d = jnp.maximum(dist, 0)
    large = max_exact + (jnp.log(jnp.maximum(d, 1).astype(F32) / max_exact)
                         / math.log(MAX_DIST / max_exact)
                         * (N_BUCKETS - max_exact)).astype(jnp.int32)
    large = jnp.minimum(large, N_BUCKETS - 1)
    bucket = jnp.where(d < max_exact, d, large)
    acc = jnp.zeros(bucket.shape, F32)
    for b in range(N_BUCKETS):
        acc = jnp.where(bucket == b, table_ref[b, h], acc)
    o_ref[0] = acc


def _rel_bias(rel_table):
    def out_map(h):
        return (h // GB, (h % GB) // 2, h % 2)
    return pl.pallas_call(
        _relbias_kernel,
        grid=(HB,),
        in_specs=[pl.BlockSpec(memory_space=pltpu.SMEM)],
        out_specs=pl.BlockSpec((1, SEQ_BLOCK, 2 * SEQ_BLOCK), out_map),
        out_shape=jax.ShapeDtypeStruct((KVB, (GB // 2) * SEQ_BLOCK, 4 * SEQ_BLOCK), F32),
        compiler_params=_params(("arbitrary",), 16 << 20),
        name="rel_bias",
    )(rel_table)


def _prenorm_kernel(x_ref, g_ref, o_ref):
    x = x_ref[...]
    r = lax.rsqrt(jnp.mean(x * x, axis=-1, keepdims=True) + NORM_EPS)
    o_ref[...] = (x * r * g_ref[...]).astype(o_ref.dtype)


def _prenorm(x2, g):
    t = x2.shape[0]
    tm = 256
    return pl.pallas_call(
        _prenorm_kernel,
        grid=(t // tm,),
        in_specs=[pl.BlockSpec((tm, D), lambda i: (i, 0)),
                  pl.BlockSpec((1, D), lambda i: (0, 0))],
        out_specs=pl.BlockSpec((tm, D), lambda i: (i, 0)),
        out_shape=jax.ShapeDtypeStruct((t, D), BF16),
        compiler_params=_params(("arbitrary",), 32 << 20),
        name="prenorm",
    )(x2, g.reshape(1, D))


def _proj_kernel(h_ref, w_ref, cs_ref, o_ref, *, silu_start, gate_start, gate_end):
    j = pl.program_id(1)
    acc = jnp.dot(h_ref[...], w_ref[...], preferred_element_type=F32)

    @pl.when((j < silu_start) | (j >= gate_end))
    def _():
        o_ref[...] = (acc * cs_ref[...]).astype(o_ref.dtype)

    @pl.when((j >= silu_start) & (j < gate_start))
    def _():
        o_ref[...] = (acc * jax.nn.sigmoid(acc)).astype(o_ref.dtype)

    @pl.when((j >= gate_start) & (j < gate_end))
    def _():
        o_ref[...] = jax.nn.sigmoid(acc).astype(o_ref.dtype)


def _fused_proj(h, w_all, col_scale):
    t = h.shape[0]
    tm, tn = 1024, CHUNK
    kern = functools.partial(_proj_kernel, silu_start=CH_BZ, gate_start=CH_GATE, gate_end=CH_BKV)
    need = 2 * (tm * D * 2 + D * tn * 2 + tm * tn * 2) + 4 * tm * tn * 4
    return pl.pallas_call(
        kern,
        grid=(t // tm, NP // tn),
        in_specs=[pl.BlockSpec((tm, D), lambda i, j: (i, 0)),
                  pl.BlockSpec((D, tn), lambda i, j: (0, j)),
                  pl.BlockSpec((1, tn), lambda i, j: (0, j))],
        out_specs=pl.BlockSpec((tm, tn), lambda i, j: (i, j)),
        out_shape=jax.ShapeDtypeStruct((t, NP), BF16),
        compiler_params=_params(("parallel", "arbitrary"), need + (8 << 20)),
        name="fused_proj",
    )(h, w_all, col_scale)


def _misc_kernel(h_ref, w_ref, bf_ref, cos_ref, sin_ref, kr_ref, ccol_ref, crow_ref, carry_ref):
    t = pl.program_id(1)
    acc = jnp.dot(h_ref[...], w_ref[...], preferred_element_type=F32)
    kr_ref[...] = _rope(acc, cos_ref[...], sin_ref[...]).astype(kr_ref.dtype)

    xx = acc + bf_ref[...]
    log_f = jnp.minimum(xx, 0.0) - jnp.log1p(jnp.exp(-jnp.abs(xx)))
    hi = log_f.astype(BF16)
    r1 = log_f - hi.astype(F32)
    mid = r1.astype(BF16)
    lo = (r1 - mid.astype(F32)).astype(BF16)
    tm = acc.shape[0]
    ri = lax.broadcasted_iota(jnp.int32, (tm, tm), 0)
    ci = lax.broadcasted_iota(jnp.int32, (tm, tm), 1)
    tri = jnp.where(ri >= ci, 1.0, 0.0).astype(BF16)
    cs = (jnp.dot(tri, hi, preferred_element_type=F32)
          + jnp.dot(tri, mid, preferred_element_type=F32)
          + jnp.dot(tri, lo, preferred_element_type=F32))

    @pl.when(t == 0)
    def _():
        carry_ref[...] = jnp.zeros_like(carry_ref)

    cs = cs + carry_ref[...]
    carry_ref[...] = cs[tm - 1:tm, :]
    cs2 = cs * LOG2E
    ccol_ref[...] = cs2
    crow_ref[0] = cs2.T


def _misc_proj(h, w_misc, bf_row, cos_t, sin_t, batch, seq):
    t = h.shape[0]
    tm = 512
    nt = seq // tm
    need = 2 * (tm * D * 2 + D * LANES * 2 + 5 * tm * LANES * 4) + 4 * tm * tm * 4
    return pl.pallas_call(
        _misc_kernel,
        grid=(batch, nt),
        in_specs=[pl.BlockSpec((tm, D), lambda b, i: (b * nt + i, 0)),
                  pl.BlockSpec((D, LANES), lambda b, i: (0, 0)),
                  pl.BlockSpec((1, LANES), lambda b, i: (0, 0)),
                  pl.BlockSpec((tm, LANES), lambda b, i: (b * nt + i, 0)),
                  pl.BlockSpec((tm, LANES), lambda b, i: (b * nt + i, 0))],
        out_specs=[pl.BlockSpec((tm, LANES), lambda b, i: (b * nt + i, 0)),
                   pl.BlockSpec((tm, LANES), lambda b, i: (b * nt + i, 0)),
                   pl.BlockSpec((1, LANES, tm), lambda b, i: (b, 0, i))],
        out_shape=[jax.ShapeDtypeStruct((t, LANES), BF16),
                   jax.ShapeDtypeStruct((t, LANES), F32),
                   jax.ShapeDtypeStruct((batch, LANES, seq), F32)],
        scratch_shapes=[pltpu.VMEM((1, LANES), F32)],
        compiler_params=_params(("arbitrary", "arbitrary"), need + (8 << 20)),
        name="misc_proj",
    )(h, w_misc, bf_row, cos_t, sin_t)


def _latent_kernel(*refs, rope, out_scale):
    if rope:
        x_ref, g_ref, w_ref, cos_ref, sin_ref, o_ref, xn_ref = refs
    else:
        x_ref, g_ref, w_ref, o_ref, xn_ref = refs

    @pl.when(pl.program_id(1) == 0)
    def _():
        x = x_ref[...].astype(F32)
        r = lax.rsqrt(jnp.mean(x * x, axis=-1, keepdims=True) + NORM_EPS)
        xn_ref[...] = (x * r * g_ref[...]).astype(xn_ref.dtype)

    acc = jnp.dot(xn_ref[...], w_ref[...], preferred_element_type=F32)
    if rope:
        cos = cos_ref[...]
        sin = sin_ref[...]
        for hh in range(acc.shape[1] // A_QPAD):
            base = hh * A_QPAD
            o_ref[:, base:base + A_NOPE] = (acc[:, base:base + A_NOPE] * out_scale).astype(o_ref.dtype)
            rot = _rope(acc[:, base + A_NOPE:base + A_QPAD], cos, sin)
            o_ref[:, base + A_NOPE:base + A_QPAD] = (rot * out_scale).astype(o_ref.dtype)
    else:
        o_ref[...] = acc.astype(o_ref.dtype)


def _latent_proj(proj, col_block, k, g, w, tables, out_scale):
    t = proj.shape[0]
    n = w.shape[1]
    tm, tn = 512, 512
    rope = tables is not None
    kern = functools.partial(_latent_kernel, rope=rope, out_scale=out_scale)
    in_specs = [pl.BlockSpec((tm, k), lambda i, j: (i, col_block)),
                pl.BlockSpec((1, k), lambda i, j: (0, 0)),
                pl.BlockSpec((k, tn), lambda i, j: (0, j))]
    args = [proj, g.reshape(1, k), w]
    if rope:
        in_specs += [pl.BlockSpec((tm, LANES), lambda i, j: (i, 0))] * 2
        args += list(tables)
    need = 2 * (tm * k * 2 + k * tn * 2 + tm * tn * 2 + 2 * tm * LANES * 4) + tm * k * 2 + 4 * tm * tn * 4
    return pl.pallas_call(
        kern,
        grid=(t // tm, n // tn),
        in_specs=in_specs,
        out_specs=pl.BlockSpec((tm, tn), lambda i, j: (i, j)),
        out_shape=jax.ShapeDtypeStruct((t, n), BF16),
        scratch_shapes=[pltpu.VMEM((tm, k), BF16)],
        compiler_params=_params(("parallel", "arbitrary"), need + (8 << 20)),
        name="latent_q" if rope else "latent_kv",
    )(*args)


def _causal_attn_kernel(*refs, tq, fox):
    if fox:
        q_ref, k_ref, v_ref, z_ref, ccol_ref, crow_ref, o_ref = refs
        keys = k_ref
    else:
        q_ref, kn_ref, kr_ref, v_ref, z_ref, o_ref, kcat_ref = refs
        kcat_ref[:, 0:LANES] = kn_ref[...]
        kcat_ref[:, LANES:2 * LANES] = kr_ref[...]
        keys = kcat_ref
    seq = q_ref.shape[0]
    if fox:
        h = pl.program_id(1)
        lane = lax.broadcasted_iota(jnp.int32, (seq, LANES), 1)
        cq_all = jnp.sum(jnp.where(lane == F_LANE + h, ccol_ref[...], 0.0), axis=1, keepdims=True)
        ck_all = crow_ref[0, pl.ds(F_LANE + h, 1), :]
    row = lax.broadcasted_iota(jnp.int32, (tq, tq), 0)
    col = lax.broadcasted_iota(jnp.int32, (tq, tq), 1)
    causal = row >= col
    dims = (((1,), (1,)), ((), ()))

    for c in range(seq // tq):
        lo, hi = c * tq, (c + 1) * tq
        q = q_ref[lo:hi, :]
        s_d = lax.dot_general(q, keys[lo:hi, :], dims, preferred_element_type=F32)
        if fox:
            s_d = s_d + cq_all[lo:hi, :] - ck_all[:, lo:hi]
        s_d = jnp.where(causal, s_d, MASKED)
        m = jnp.max(s_d, axis=1, keepdims=True)
        if c:
            s_o = lax.dot_general(q, keys[0:lo, :], dims, preferred_element_type=F32)
            if fox:
                s_o = s_o + cq_all[lo:hi, :] - ck_all[:, 0:lo]
            m = jnp.maximum(m, jnp.max(s_o, axis=1, keepdims=True))
            p_o = jnp.exp2(s_o - m)
            l = jnp.sum(p_o, axis=1, keepdims=True)
            acc = jnp.dot(p_o.astype(BF16), v_ref[0:lo, :], preferred_element_type=F32)
        p_d = jnp.exp2(s_d - m)
        l_d = jnp.sum(p_d, axis=1, keepdims=True)
        acc_d = jnp.dot(p_d.astype(BF16), v_ref[lo:hi, :], preferred_element_type=F32)
        if c:
            l = l + l_d
            acc = acc + acc_d
        else:
            l, acc = l_d, acc_d
        o_ref[lo:hi, :] = (acc / l * z_ref[lo:hi, :].astype(F32)).astype(o_ref.dtype)


ATTN_TQ = 256


def _attention_a(q_a, kv_a, k_rope, proj, batch, seq):
    t = q_a.shape[0]
    kern = functools.partial(_causal_attn_kernel, tq=ATTN_TQ, fox=False)
    z_col = CH_AZ * CHUNK // LANES
    need = 2 * (seq * A_QPAD * 2 + 5 * seq * LANES * 2) + seq * A_QPAD * 2
    return pl.pallas_call(
        kern,
        grid=(batch, HA),
        in_specs=[pl.BlockSpec((seq, A_QPAD), lambda b, h: (b, h)),
                  pl.BlockSpec((seq, LANES), lambda b, h: (b, h)),
                  pl.BlockSpec((seq, LANES), lambda b, h: (b, 0)),
                  pl.BlockSpec((seq, LANES), lambda b, h: (b, HA + h)),
                  pl.BlockSpec((seq, LANES), lambda b, h: (b, z_col + h))],
        out_specs=pl.BlockSpec((seq, LANES), lambda b, h: (b, h)),
        out_shape=jax.ShapeDtypeStruct((t, HA * LANES), BF16),
        scratch_shapes=[pltpu.VMEM((seq, A_QPAD), BF16)],
        compiler_params=_params(("parallel", "arbitrary"), need + (24 << 20)),
        name="attn_mla",
    )(q_a, kv_a, k_rope, kv_a, proj)


def _attention_c(proj, ccol, crow, batch, seq):
    t = proj.shape[0]
    kern = functools.partial(_causal_attn_kernel, tq=ATTN_TQ, fox=True)
    q_col, k_col, v_col, z_col = (c * CHUNK // LANES for c in (CH_CQ, CH_CK, CH_CV, CH_CZ))
    need = 2 * (5 * seq * LANES * 2 + 2 * seq * LANES * 4)
    return pl.pallas_call(
        kern,
        grid=(batch, HC),
        in_specs=[pl.BlockSpec((seq, LANES), lambda b, h: (b, q_col + h)),
                  pl.BlockSpec((seq, LANES), lambda b, h: (b, k_col + h)),
                  pl.BlockSpec((seq, LANES), lambda b, h: (b, v_col + h)),
                  pl.BlockSpec((seq, LANES), lambda b, h: (b, z_col + h)),
                  pl.BlockSpec((seq, LANES), lambda b, h: (b, 0)),
                  pl.BlockSpec((1, LANES, seq), lambda b, h: (b, 0, 0))],
        out_specs=pl.BlockSpec((seq, LANES), lambda b, h: (b, h)),
        out_shape=jax.ShapeDtypeStruct((t, HC * DC), BF16),
        compiler_params=_params(("parallel", "arbitrary"), need + (24 << 20)),
        name="attn_fox",
    )(proj, proj, proj, proj, ccol, crow)


def _swa_kernel(q_ref, kvp_ref, kvc_ref, z_ref, bias_ref, sink_ref, o_ref):
    n = pl.program_id(1)
    blk = SEQ_BLOCK
    band = 2 * blk
    rows = (GB // 2) * blk
    kv = jnp.concatenate([kvp_ref[...], kvc_ref[...]], axis=0).astype(F32)
    lane = lax.broadcasted_iota(jnp.int32, (band, LANES), 1)
    low = lane < DB

    def split(two_heads):
        lo = jnp.where(low, two_heads, 0.0)
        hi = jnp.where(low, 0.0, two_heads)
        return (lo, pltpu.roll(lo, DB, 1)), (pltpu.roll(hi, DB, 1), hi)

    k_heads = split(kv[:, 0:LANES]) + split(kv[:, LANES:2 * LANES])
    v_heads = split(kv[:, 2 * LANES:3 * LANES]) + split(kv[:, 3 * LANES:4 * LANES])

    r = lax.broadcasted_iota(jnp.int32, (rows, band), 0) % blk
    c = lax.broadcasted_iota(jnp.int32, (rows, band), 1)
    dist = r + blk - c
    first_key = jnp.where(n > 0, 0, blk)
    valid = (dist >= 0) & (dist < blk) & (c >= first_key)

    for kh in range(KVB):
        k_cat = jnp.concatenate(k_heads[kh], axis=0).astype(BF16)
        v_cat = jnp.concatenate(v_heads[kh], axis=0).astype(BF16)
        q4 = jnp.concatenate(
            [q_ref[:, (kh * (GB // 2) + jj) * LANES:(kh * (GB // 2) + jj + 1) * LANES]
             for jj in range(GB // 2)], axis=0)
        s = lax.dot_general(q4, k_cat, (((1,), (1,)), ((), ())), preferred_element_type=F32)
        s = s + bias_ref[kh]
        halves = []
        for e in range(2):
            se = jnp.where(valid, s[:, e * band:(e + 1) * band], MASKED)
            sink = sink_ref[kh, :, e:e + 1]
            m = jnp.maximum(jnp.max(se, axis=1, keepdims=True), sink)
            p = jnp.exp(se - m)
            denom = jnp.sum(p, axis=1, keepdims=True) + jnp.exp(sink - m)
            halves.append((p / denom).astype(BF16))
        p_cat = jnp.concatenate(halves, axis=1)
        o = jnp.dot(p_cat, v_cat, preferred_element_type=F32)
        for jj in range(GB // 2):
            c0 = (kh * (GB // 2) + jj) * LANES
            o_ref[:, c0:c0 + LANES] = (o[jj * blk:(jj + 1) * blk, :]
                                       * z_ref[:, c0:c0 + LANES].astype(F32)).astype(o_ref.dtype)


def _attention_b(proj, bias, sink_cols, batch, seq):
    t = proj.shape[0]
    blk = SEQ_BLOCK
    nb = seq // blk
    width = HB * DB
    q_col = CH_BQ * CHUNK // width
    kv_col = CH_BKV
    z_col = CH_BZ * CHUNK // width
    assert CH_BQ * CHUNK % width == 0 and CH_BZ * CHUNK % width == 0
    need = 2 * (3 * blk * width * 2 + 2 * blk * CHUNK * 2 + bias.size * 4 + sink_cols.size * 4)
    return pl.pallas_call(
        _swa_kernel,
        grid=(batch, nb),
        in_specs=[pl.BlockSpec((blk, width), lambda b, n: (b * nb + n, q_col)),
                  pl.BlockSpec((blk, CHUNK), lambda b, n: (b * nb + jnp.maximum(n - 1, 0), kv_col)),
                  pl.BlockSpec((blk, CHUNK), lambda b, n: (b * nb + n, kv_col)),
                  pl.BlockSpec((blk, width), lambda b, n: (b * nb + n, z_col)),
                  pl.BlockSpec(bias.shape, lambda b, n: (0, 0, 0)),
                  pl.BlockSpec(sink_cols.shape, lambda b, n: (0, 0, 0))],
        out_specs=pl.BlockSpec((blk, width), lambda b, n: (b * nb + n, 0)),
        out_shape=jax.ShapeDtypeStruct((t, width), BF16),
        compiler_params=_params(("parallel", "arbitrary"), need + (24 << 20)),
        name="attn_swa",
    )(proj, proj, proj, proj, bias, sink_cols)


def _merge_kernel(oa_ref, ob_ref, oc_ref, wa_ref, wb_ref, wc_ref, ga_ref, gb_ref, gc_ref, y_ref):
    y = ga_ref[...].astype(F32) * jnp.dot(oa_ref[...], wa_ref[...], preferred_element_type=F32)
    y = y + gb_ref[...].astype(F32) * jnp.dot(ob_ref[...], wb_ref[...], preferred_element_type=F32)
    y = y + gc_ref[...].astype(F32) * jnp.dot(oc_ref[...], wc_ref[...], preferred_element_type=F32)
    y_ref[...] = y.astype(y_ref.dtype)


def _merge_proj(oa, ob, oc, wa, wb, wc, proj):
    t, k = oa.shape
    tm, tn = 512, CHUNK
    gcols = D // tn
    o_spec = pl.BlockSpec((tm, k), lambda i, j: (i, 0))
    w_spec = pl.BlockSpec((k, tn), lambda i, j: (0, j))

    def gate_spec(branch):
        return pl.BlockSpec((tm, tn), lambda i, j: (i, CH_GATE + branch * gcols + j))

    need = 2 * (3 * tm * k * 2 + 3 * k * tn * 2 + 4 * tm * tn * 2) + 4 * tm * tn * 4
    return pl.pallas_call(
        _merge_kernel,
        grid=(t // tm, D // tn),
        in_specs=[o_spec, o_spec, o_spec, w_spec, w_spec, w_spec,
                  gate_spec(0), gate_spec(1), gate_spec(2)],
        out_specs=pl.BlockSpec((tm, tn), lambda i, j: (i, j)),
        out_shape=jax.ShapeDtypeStruct((t, D), BF16),
        compiler_params=_params(("parallel", "arbitrary"), need + (8 << 20)),
        name="merge_proj",
    )(oa, ob, oc, wa, wb, wc, proj, proj, proj)


def _out_kernel(y_ref, w_ref, x_ref, g_ref, o_ref, acc_ref, ssq_ref, *, nj):
    j = pl.program_id(1)

    @pl.when(j < nj)
    def _():
        r = jnp.dot(y_ref[...], w_ref[...], preferred_element_type=F32)
        acc_ref[j] = r
        ss = jnp.sum(r * r, axis=1, keepdims=True)

        @pl.when(j == 0)
        def _():
            ssq_ref[...] = ss

        @pl.when(j > 0)
        def _():
            ssq_ref[...] += ss

    @pl.when(j >= nj)
    def _():
        rs = lax.rsqrt(ssq_ref[...] * (1.0 / D) + NORM_EPS)
        o_ref[...] = x_ref[...] + acc_ref[j - nj] * rs * g_ref[...]


def _out_proj(y, w_o, x2, g):
    t = y.shape[0]
    tm, tn = 512, CHUNK
    nj = D // tn
    kern = functools.partial(_out_kernel, nj=nj)
    need = 2 * (tm * D * 2 + D * tn * 2 + 2 * tm * tn * 4) + tm * D * 4 + 2 * tm * tn * 4
    return pl.pallas_call(
        kern,
        grid=(t // tm, 2 * nj),
        in_specs=[pl.BlockSpec((tm, D), lambda i, j: (i, 0)),
                  pl.BlockSpec((D, tn), lambda i, j: (0, jnp.minimum(j, nj - 1))),
                  pl.BlockSpec((tm, tn), lambda i, j: (i, jnp.maximum(j - nj, 0))),
                  pl.BlockSpec((1, tn), lambda i, j: (0, jnp.maximum(j - nj, 0)))],
        out_specs=pl.BlockSpec((tm, tn), lambda i, j: (i, jnp.maximum(j - nj, 0))),
        out_shape=jax.ShapeDtypeStruct((t, D), F32),
        scratch_shapes=[pltpu.VMEM((nj, tm, tn), F32), pltpu.VMEM((tm, 1), F32)],
        compiler_params=_params(("parallel", "arbitrary"), need + (8 << 20)),
        name="out_proj",
    )(y, w_o, x2, g.reshape(1, D))


def _layer_weights(w_in, w_merge, w_uq, w_uk, w_uv, b_f):
    sizes = (A_QR, A_KVR, A_ROPE, HA * A_NOPE, HB * DB, KVB * DB, KVB * DB, HB * DB,
             HC * DC, HC * DC, HC * DC, HC, HC * DC)
    parts, start = [], 0
    for n in sizes:
        parts.append(w_in[:, start:start + n])
        start += n
    (a_cq, a_ckv, a_kr, a_z, b_q, b_k, b_v, b_z, c_q, c_k, c_v, c_f, c_z) = parts
    w_all = jnp.concatenate([a_cq, a_ckv, b_q, c_q, c_k, c_v, b_z, a_z, c_z, w_merge, b_k, b_v],
                            axis=1).astype(BF16)
    col_scale = jnp.ones((NP,), F32)
    col_scale = col_scale.at[CH_BQ * CHUNK:CH_CQ * CHUNK].set(DB ** -0.5)
    col_scale = col_scale.at[CH_CQ * CHUNK:CH_CK * CHUNK].set(DC ** -0.5 * LOG2E)
    col_scale = col_scale.reshape(1, NP)

    pad = jnp.zeros((D, LANES - F_LANE - HC), w_in.dtype)
    w_misc = jnp.concatenate([a_kr, c_f, pad], axis=1).astype(BF16)
    bf_row = jnp.concatenate([jnp.zeros((F_LANE,), F32), b_f.astype(F32),
                              jnp.zeros((LANES - F_LANE - HC,), F32)]).reshape(1, LANES)

    wq = jnp.pad(w_uq, ((0, 0), (0, 0), (0, A_QPAD - A_NOPE - A_ROPE)))
    wq = wq.reshape(A_QR, HA * A_QPAD).astype(BF16)
    wkv = jnp.concatenate([w_uk.reshape(A_KVR, HA * A_NOPE), w_uv.reshape(A_KVR, HA * A_NOPE)],
                          axis=1).astype(BF16)
    return w_all, col_scale, w_misc, bf_row, wq, wkv


def _sink_columns(sinks):
    s = sinks.astype(F32).reshape(KVB, GB // 2, 1, 2)
    s = jnp.broadcast_to(s, (KVB, GB // 2, SEQ_BLOCK, 2)).reshape(KVB, (GB // 2) * SEQ_BLOCK, 2)
    return jnp.pad(s, ((0, 0), (0, 0), (0, LANES - 2)))


def kernel(x, positions, rel_table, pre_norm, w_in, q_a_norm, kv_a_norm, w_uq, w_uk, w_uv,
           sinks, b_f, w_proj_a, w_proj_b, w_proj_c, w_merge, w_o, post_norm):
    batch, seq, _ = x.shape
    depth = w_in.shape[0]
    x2 = x.reshape(batch * seq, D)
    tables = _rope_tables(positions)
    bias = _rel_bias(rel_table)
    for l in range(depth):
        w_all, col_scale, w_misc, bf_row, wq, wkv = _layer_weights(
            w_in[l], w_merge[l], w_uq[l], w_uk[l], w_uv[l], b_f[l])
        h = _prenorm(x2, pre_norm[l])
        proj = _fused_proj(h, w_all, col_scale)
        k_rope, ccol, crow = _misc_proj(h, w_misc, bf_row, tables[0], tables[1], batch, seq)
        q_a = _latent_proj(proj, 0, A_QR, q_a_norm[l], wq, tables,
                           (A_NOPE + A_ROPE) ** -0.5 * LOG2E)
        kv_a = _latent_proj(proj, CH_ACKV, A_KVR, kv_a_norm[l], wkv, None, 1.0)
        o_a = _attention_a(q_a, kv_a, k_rope, proj, batch, seq)
        o_b = _attention_b(proj, bias, _sink_columns(sinks[l]), batch, seq)
        o_c = _attention_c(proj, ccol, crow, batch, seq)
        y = _merge_proj(o_a, o_b, o_c, w_proj_a[l].astype(BF16), w_proj_b[l].astype(BF16),
                        w_proj_c[l].astype(BF16), proj)
        x2 = _out_proj(y, w_o[l].astype(BF16), x2, post_norm[l])
    return x2.reshape(batch, seq, D)
```

```python
import functools
import math

import jax
import jax.numpy as jnp
from jax import lax
from jax.experimental import pallas as pl
from jax.experimental.pallas import tpu as pltpu

F32 = jnp.float32
BF16 = jnp.bfloat16

D = 4096
SEQ_BLOCK = 128
NORM_EPS = 1e-6
MASKED = -1e30
LOG2E = math.log2(math.e)
HA = 16
A_QR = 1536
A_KVR = 512
A_NOPE = 128
A_ROPE = 64
A_HALF = A_ROPE // 2
A_QPAD = 256
THETA = 10000.0
HB = 32
KVB = 4
DB = 64
GB = HB // KVB
HC = 16
DC = 128
N_BUCKETS = 32
MAX_DIST = 128

LANES = 128
CHUNK = 512

CH_ACQ, CH_ACKV, CH_BQ, CH_CQ, CH_CK, CH_CV = 0, 3, 4, 8, 12, 16
CH_BZ, CH_AZ, CH_CZ = 20, 24, 28
CH_GATE = 32
CH_BKV = 56
N_ID_CHUNKS = 20
N_SILU_CHUNKS = 12
N_CHUNKS = 57
NP = N_CHUNKS * CHUNK
F_LANE = 64

VMEM_CAP = 60 * 1024 * 1024


def _vmem(nbytes):
    return int(min(VMEM_CAP, nbytes))


def _params(sem, nbytes):
    return pltpu.CompilerParams(dimension_semantics=sem, vmem_limit_bytes=_vmem(nbytes))


def _rope_table_kernel(pos_ref, freq_ref, cos_ref, sin_ref):
    ang = pos_ref[...].astype(F32) * freq_ref[...]
    lane = lax.broadcasted_iota(jnp.int32, ang.shape, 1)
    live = lane < A_ROPE
    s = jnp.sin(ang)
    cos_ref[...] = jnp.where(live, jnp.cos(ang), 0.0)
    sin_ref[...] = jnp.where(live, jnp.where(lane < A_HALF, -s, s), 0.0)


def _rope_tables(positions):
    t = positions.size
    tm = 2048
    half = jnp.arange(A_HALF, dtype=F32)
    inv_freq = THETA ** (-half / A_HALF)
    freq = jnp.concatenate([inv_freq, inv_freq, jnp.zeros((LANES - A_ROPE,), F32)]).reshape(1, LANES)
    pos = positions.reshape(t, 1)
    return pl.pallas_call(
        _rope_table_kernel,
        grid=(t // tm,),
        in_specs=[pl.BlockSpec((tm, 1), lambda i: (i, 0)),
                  pl.BlockSpec((1, LANES), lambda i: (0, 0))],
        out_specs=[pl.BlockSpec((tm, LANES), lambda i: (i, 0)),
                   pl.BlockSpec((tm, LANES), lambda i: (i, 0))],
        out_shape=[jax.ShapeDtypeStruct((t, LANES), F32)] * 2,
        compiler_params=_params(("arbitrary",), 32 << 20),
        name="rope_tables",
    )(pos, freq)


def _rope(t, cos, sin):
    lane = lax.broadcasted_iota(jnp.int32, t.shape, 1)
    partner = jnp.where(lane < A_HALF,
                        pltpu.roll(t, LANES - A_HALF, 1),
                        pltpu.roll(t, A_HALF, 1))
    return t * cos + partner * sin


def _relbias_kernel(table_ref, o_ref):
    h = pl.program_id(0)
    qi = lax.broadcasted_iota(jnp.int32, (SEQ_BLOCK, 2 * SEQ_BLOCK), 0)
    si = lax.broadcasted_iota(jnp.int32, (SEQ_BLOCK, 2 * SEQ_BLOCK), 1)
    dist = qi + SEQ_BLOCK - si
    max_exact = N_BUCKETS // 2
    d = jnp.maximum(dist, 0)
    large = max_exact + (jnp.log(jnp.maximum(d, 1).astype(F32) / max_exact)
                         / math.log(MAX_DIST / max_exact)
                         * (N_BUCKETS - max_exact)).astype(jnp.int32)
    large = jnp.minimum(large, N_BUCKETS - 1)
    bucket = jnp.where(d < max_exact, d, large)
    acc = jnp.zeros(bucket.shape, F32)
    for b in range(N_BUCKETS):
        acc = jnp.where(bucket == b, table_ref[b, h], acc)
    o_ref[0] = acc * LOG2E


def _rel_bias(rel_table):
    def out_map(h):
        return (h // GB, (h % GB) // 2, h % 2)
    return pl.pallas_call(
        _relbias_kernel,
        grid=(HB,),
        in_specs=[pl.BlockSpec(memory_space=pltpu.SMEM)],
        out_specs=pl.BlockSpec((1, SEQ_BLOCK, 2 * SEQ_BLOCK), out_map),
        out_shape=jax.ShapeDtypeStruct((KVB, (GB // 2) * SEQ_BLOCK, 4 * SEQ_BLOCK), F32),
        compiler_params=_params(("arbitrary",), 16 << 20),
        name="rel_bias",
    )(rel_table)


def _prenorm_kernel(x_ref, g_ref, o_ref):
    x = x_ref[...]
    r = lax.rsqrt(jnp.mean(x * x, axis=-1, keepdims=True) + NORM_EPS)
    o_ref[...] = (x * r * g_ref[...]).astype(o_ref.dtype)


def _prenorm(x2, g):
    t = x2.shape[0]
    tm = 256
    return pl.pallas_call(
        _prenorm_kernel,
        grid=(t // tm,),
        in_specs=[pl.BlockSpec((tm, D), lambda i: (i, 0)),
                  pl.BlockSpec((1, D), lambda i: (0, 0))],
        out_specs=pl.BlockSpec((tm, D), lambda i: (i, 0)),
        out_shape=jax.ShapeDtypeStruct((t, D), BF16),
        compiler_params=_params(("arbitrary",), 32 << 20),
        name="prenorm",
    )(x2, g.reshape(1, D))


ROW_CHUNK = 256


def _chunked_dot(x_ref, w_ref, o_ref, epilogue):
    for r in range(x_ref.shape[0] // ROW_CHUNK):
        rows = slice(r * ROW_CHUNK, (r + 1) * ROW_CHUNK)
        acc = jnp.dot(x_ref[rows, :], w_ref[...], preferred_element_type=F32)
        o_ref[rows, :] = epilogue(acc, rows).astype(o_ref.dtype)


def _proj_kernel(h_ref, wi_ref, wm_ref, cs_ref, o_ref, *, silu_start, gate_start, gate_end):
    j = pl.program_id(1)

    @pl.when((j < silu_start) | (j >= gate_end))
    def _():
        _chunked_dot(h_ref, wi_ref, o_ref, lambda a, rows: a * cs_ref[...])

    @pl.when((j >= silu_start) & (j < gate_start))
    def _():
        _chunked_dot(h_ref, wi_ref, o_ref, lambda a, rows: a * jax.nn.sigmoid(a))

    @pl.when((j >= gate_start) & (j < gate_end))
    def _():
        _chunked_dot(h_ref, wm_ref, o_ref, lambda a, rows: jax.nn.sigmoid(a))


def _fused_proj(h, w_inr, w_merge, layer, col_scale):
    t = h.shape[0]
    tm, tn = 1024, CHUNK
    n_gate = CH_BKV - CH_GATE
    kern = functools.partial(_proj_kernel, silu_start=CH_BZ, gate_start=CH_GATE, gate_end=CH_BKV)

    def wi_map(i, j):
        return (0, jnp.where(j < CH_GATE, j, jnp.where(j >= CH_BKV, j - n_gate, CH_GATE - 1)))

    def wm_map(i, j):
        return (layer, 0, jnp.clip(j - CH_GATE, 0, n_gate - 1))

    need = 2 * (tm * D * 2 + 2 * D * tn * 2 + tm * tn * 2) + 4 * tm * tn * 4
    return pl.pallas_call(
        kern,
        grid=(t // tm, NP // tn),
        in_specs=[pl.BlockSpec((tm, D), lambda i, j: (i, 0)),
                  pl.BlockSpec((D, tn), wi_map),
                  pl.BlockSpec((None, D, tn), wm_map),
                  pl.BlockSpec((1, tn), lambda i, j: (0, j))],
        out_specs=pl.BlockSpec((tm, tn), lambda i, j: (i, j)),
        out_shape=jax.ShapeDtypeStruct((t, NP), BF16),
        compiler_params=_params(("parallel", "arbitrary"), need + (8 << 20)),
        name="fused_proj",
    )(h, w_inr, w_merge, col_scale)


def _misc_kernel(h_ref, w_ref, bf_ref, cos_ref, sin_ref, kr_ref, ccol_ref, crow_ref, carry_ref):
    t = pl.program_id(1)
    acc = jnp.dot(h_ref[...], w_ref[...], preferred_element_type=F32)
    kr_ref[...] = _rope(acc, cos_ref[...], sin_ref[...]).astype(kr_ref.dtype)

    xx = acc + bf_ref[...]
    log_f = jnp.minimum(xx, 0.0) - jnp.log1p(jnp.exp(-jnp.abs(xx)))
    hi = log_f.astype(BF16)
    r1 = log_f - hi.astype(F32)
    mid = r1.astype(BF16)
    lo = (r1 - mid.astype(F32)).astype(BF16)
    tm = acc.shape[0]
    ri = lax.broadcasted_iota(jnp.int32, (tm, tm), 0)
    ci = lax.broadcasted_iota(jnp.int32, (tm, tm), 1)
    tri = jnp.where(ri >= ci, 1.0, 0.0).astype(BF16)
    cs = (jnp.dot(tri, hi, preferred_element_type=F32)
          + jnp.dot(tri, mid, preferred_element_type=F32)
          + jnp.dot(tri, lo, preferred_element_type=F32))

    @pl.when(t == 0)
    def _():
        carry_ref[...] = jnp.zeros_like(carry_ref)

    cs = cs + carry_ref[...]
    carry_ref[...] = cs[tm - 1:tm, :]
    cs2 = cs * LOG2E
    ccol_ref[...] = cs2
    crow_ref[0] = cs2.T


def _misc_proj(h, w_misc, bf_row, cos_t, sin_t, batch, seq):
    t = h.shape[0]
    tm = 512
    nt = seq // tm
    need = 2 * (tm * D * 2 + D * LANES * 2 + 5 * tm * LANES * 4) + 4 * tm * tm * 4
    return pl.pallas_call(
        _misc_kernel,
        grid=(batch, nt),
        in_specs=[pl.BlockSpec((tm, D), lambda b, i: (b * nt + i, 0)),
                  pl.BlockSpec((D, LANES), lambda b, i: (0, 0)),
                  pl.BlockSpec((1, LANES), lambda b, i: (0, 0)),
                  pl.BlockSpec((tm, LANES), lambda b, i: (b * nt + i, 0)),
                  pl.BlockSpec((tm, LANES), lambda b, i: (b * nt + i, 0))],
        out_specs=[pl.BlockSpec((tm, LANES), lambda b, i: (b * nt + i, 0)),
                   pl.BlockSpec((tm, LANES), lambda b, i: (b * nt + i, 0)),
                   pl.BlockSpec((1, LANES, tm), lambda b, i: (b, 0, i))],
        out_shape=[jax.ShapeDtypeStruct((t, LANES), BF16),
                   jax.ShapeDtypeStruct((t, LANES), F32),
                   jax.ShapeDtypeStruct((batch, LANES, seq), F32)],
        scratch_shapes=[pltpu.VMEM((1, LANES), F32)],
        compiler_params=_params(("arbitrary", "arbitrary"), need + (8 << 20)),
        name="misc_proj",
    )(h, w_misc, bf_row, cos_t, sin_t)


def _latent_kernel(*refs, rope, out_scale):
    if rope:
        x_ref, g_ref, w_ref, cos_ref, sin_ref, o_ref, xn_ref = refs
    else:
        x_ref, g_ref, w_ref, o_ref, xn_ref = refs

    @pl.when(pl.program_id(1) == 0)
    def _():
        x = x_ref[...].astype(F32)
        r = lax.rsqrt(jnp.mean(x * x, axis=-1, keepdims=True) + NORM_EPS)
        xn_ref[...] = (x * r * g_ref[...]).astype(xn_ref.dtype)

    def rotary(acc, rows):
        cos = cos_ref[rows, :]
        sin = sin_ref[rows, :]
        pieces = []
        for hh in range(acc.shape[1] // A_QPAD):
            base = hh * A_QPAD
            pieces.append(acc[:, base:base + A_NOPE])
            pieces.append(_rope(acc[:, base + A_NOPE:base + A_QPAD], cos, sin))
        return jnp.concatenate(pieces, axis=1) * out_scale

    _chunked_dot(xn_ref, w_ref, o_ref, rotary if rope else (lambda acc, rows: acc))


def _latent_proj(proj, col_block, k, g, w, tables, out_scale):
    t = proj.shape[0]
    n = w.shape[1]
    tm, tn = 1024, 1024
    rope = tables is not None
    kern = functools.partial(_latent_kernel, rope=rope, out_scale=out_scale)
    in_specs = [pl.BlockSpec((tm, k), lambda i, j: (i, col_block)),
                pl.BlockSpec((1, k), lambda i, j: (0, 0)),
                pl.BlockSpec((k, tn), lambda i, j: (0, j))]
    args = [proj, g.reshape(1, k), w]
    if rope:
        in_specs += [pl.BlockSpec((tm, LANES), lambda i, j: (i, 0))] * 2
        args += list(tables)
    need = 2 * (tm * k * 2 + k * tn * 2 + tm * tn * 2 + 2 * tm * LANES * 4) + tm * k * 2 + 4 * tm * tn * 4
    return pl.pallas_call(
        kern,
        grid=(t // tm, n // tn),
        in_specs=in_specs,
        out_specs=pl.BlockSpec((tm, tn), lambda i, j: (i, j)),
        out_shape=jax.ShapeDtypeStruct((t, n), BF16),
        scratch_shapes=[pltpu.VMEM((tm, k), BF16)],
        compiler_params=_params(("parallel", "arbitrary"), need + (8 << 20)),
        name="latent_q" if rope else "latent_kv",
    )(*args)


def _causal_attn_kernel(*refs, tq, fox):
    if fox:
        q_ref, k_ref, v_ref, z_ref, ccol_ref, crow_ref, o_ref = refs
        keys = k_ref
    else:
        q_ref, kn_ref, kr_ref, v_ref, z_ref, o_ref, kcat_ref = refs
        kcat_ref[:, 0:LANES] = kn_ref[...]
        kcat_ref[:, LANES:2 * LANES] = kr_ref[...]
        keys = kcat_ref
    seq = q_ref.shape[0]
    if fox:
        h = pl.program_id(1)
        lane = lax.broadcasted_iota(jnp.int32, (seq, LANES), 1)
        cq_all = jnp.sum(jnp.where(lane == F_LANE + h, ccol_ref[...], 0.0), axis=1, keepdims=True)
        ck_all = crow_ref[0, pl.ds(F_LANE + h, 1), :]
    row = lax.broadcasted_iota(jnp.int32, (tq, tq), 0)
    col = lax.broadcasted_iota(jnp.int32, (tq, tq), 1)
    causal = row >= col
    dims = (((1,), (1,)), ((), ()))

    for c in range(seq // tq):
        lo, hi = c * tq, (c + 1) * tq
        q = q_ref[lo:hi, :]
        s_d = lax.dot_general(q, keys[lo:hi, :], dims, preferred_element_type=F32)
        if fox:
            s_d = s_d + cq_all[lo:hi, :] - ck_all[:, lo:hi]
        s_d = jnp.where(causal, s_d, MASKED)
        m = jnp.max(s_d, axis=1, keepdims=True)
        if c:
            s_o = lax.dot_general(q, keys[0:lo, :], dims, preferred_element_type=F32)
            if fox:
                s_o = s_o + cq_all[lo:hi, :] - ck_all[:, 0:lo]
            m = jnp.maximum(m, jnp.max(s_o, axis=1, keepdims=True))
            p_o = jnp.exp2(s_o - m)
            l = jnp.sum(p_o, axis=1, keepdims=True)
            acc = jnp.dot(p_o.astype(BF16), v_ref[0:lo, :], preferred_element_type=F32)
        p_d = jnp.exp2(s_d - m)
        l_d = jnp.sum(p_d, axis=1, keepdims=True)
        acc_d = jnp.dot(p_d.astype(BF16), v_ref[lo:hi, :], preferred_element_type=F32)
        if c:
            l = l + l_d
            acc = acc + acc_d
        else:
            l, acc = l_d, acc_d
        o_ref[lo:hi, :] = (acc / l * z_ref[lo:hi, :].astype(F32)).astype(o_ref.dtype)


ATTN_TQ = 256


def _attention_a(q_a, kv_a, k_rope, proj, batch, seq):
    t = q_a.shape[0]
    kern = functools.partial(_causal_attn_kernel, tq=ATTN_TQ, fox=False)
    z_col = CH_AZ * CHUNK // LANES
    need = 2 * (seq * A_QPAD * 2 + 5 * seq * LANES * 2) + seq * A_QPAD * 2
    return pl.pallas_call(
        kern,
        grid=(batch, HA),
        in_specs=[pl.BlockSpec((seq, A_QPAD), lambda b, h: (b, h)),
                  pl.BlockSpec((seq, LANES), lambda b, h: (b, h)),
                  pl.BlockSpec((seq, LANES), lambda b, h: (b, 0)),
                  pl.BlockSpec((seq, LANES), lambda b, h: (b, HA + h)),
                  pl.BlockSpec((seq, LANES), lambda b, h: (b, z_col + h))],
        out_specs=pl.BlockSpec((seq, LANES), lambda b, h: (b, h)),
        out_shape=jax.ShapeDtypeStruct((t, HA * LANES), BF16),
        scratch_shapes=[pltpu.VMEM((seq, A_QPAD), BF16)],
        compiler_params=_params(("parallel", "arbitrary"), need + (24 << 20)),
        name="attn_mla",
    )(q_a, kv_a, k_rope, kv_a, proj)


def _attention_c(proj, ccol, crow, batch, seq):
    t = proj.shape[0]
    kern = functools.partial(_causal_attn_kernel, tq=ATTN_TQ, fox=True)
    q_col, k_col, v_col, z_col = (c * CHUNK // LANES for c in (CH_CQ, CH_CK, CH_CV, CH_CZ))
    need = 2 * (5 * seq * LANES * 2 + 2 * seq * LANES * 4)
    return pl.pallas_call(
        kern,
        grid=(batch, HC),
        in_specs=[pl.BlockSpec((seq, LANES), lambda b, h: (b, q_col + h)),
                  pl.BlockSpec((seq, LANES), lambda b, h: (b, k_col + h)),
                  pl.BlockSpec((seq, LANES), lambda b, h: (b, v_col + h)),
                  pl.BlockSpec((seq, LANES), lambda b, h: (b, z_col + h)),
                  pl.BlockSpec((seq, LANES), lambda b, h: (b, 0)),
                  pl.BlockSpec((1, LANES, seq), lambda b, h: (b, 0, 0))],
        out_specs=pl.BlockSpec((seq, LANES), lambda b, h: (b, h)),
        out_shape=jax.ShapeDtypeStruct((t, HC * DC), BF16),
        compiler_params=_params(("parallel", "arbitrary"), need + (24 << 20)),
        name="attn_fox",
    )(proj, proj, proj, proj, ccol, crow)


def _swa_kernel(q_ref, kvp_ref, kvc_ref, z_ref, bias_ref, sink_ref, o_ref):
    n = pl.program_id(1)
    blk = SEQ_BLOCK
    band = 2 * blk
    rows = (GB // 2) * blk
    kv = jnp.concatenate([kvp_ref[...], kvc_ref[...]], axis=0).astype(F32)
    lane = lax.broadcasted_iota(jnp.int32, (band, LANES), 1)
    low = lane < DB

    def split(two_heads):
        lo = jnp.where(low, two_heads, 0.0)
        hi = jnp.where(low, 0.0, two_heads)
        return (lo, pltpu.roll(lo, DB, 1)), (pltpu.roll(hi, DB, 1), hi)

    k_heads = split(kv[:, 0:LANES]) + split(kv[:, LANES:2 * LANES])
    v_heads = split(kv[:, 2 * LANES:3 * LANES]) + split(kv[:, 3 * LANES:4 * LANES])

    r = lax.broadcasted_iota(jnp.int32, (rows, band), 0) % blk
    c = lax.broadcasted_iota(jnp.int32, (rows, band), 1)
    dist = r + blk - c
    first_key = jnp.where(n > 0, 0, blk)
    valid = (dist >= 0) & (dist < blk) & (c >= first_key)

    for kh in range(KVB):
        k_cat = jnp.concatenate(k_heads[kh], axis=0).astype(BF16)
        v_cat = jnp.concatenate(v_heads[kh], axis=0).astype(BF16)
        q4 = jnp.concatenate(
            [q_ref[:, (kh * (GB // 2) + jj) * LANES:(kh * (GB // 2) + jj + 1) * LANES]
             for jj in range(GB // 2)], axis=0)
        s = lax.dot_general(q4, k_cat, (((1,), (1,)), ((), ())), preferred_element_type=F32)
        s = s + bias_ref[kh]
        halves, inv = [], []
        for e in range(2):
            se = jnp.where(valid, s[:, e * band:(e + 1) * band], MASKED)
            sink = sink_ref[kh, :, e:e + 1]
            m = jnp.maximum(jnp.max(se, axis=1, keepdims=True), sink)
            p = jnp.exp2(se - m)
            denom = jnp.sum(p, axis=1, keepdims=True) + jnp.exp2(sink - m)
            halves.append(p.astype(BF16))
            inv.append(1.0 / denom)
        p_cat = jnp.concatenate(halves, axis=1)
        o = jnp.dot(p_cat, v_cat, preferred_element_type=F32)
        o = o * jnp.where(lax.broadcasted_iota(jnp.int32, o.shape, 1) < DB, inv[0], inv[1])
        for jj in range(GB // 2):
            c0 = (kh * (GB // 2) + jj) * LANES
            o_ref[:, c0:c0 + LANES] = (o[jj * blk:(jj + 1) * blk, :]
                                       * z_ref[:, c0:c0 + LANES].astype(F32)).astype(o_ref.dtype)


def _attention_b(proj, bias, sink_cols, batch, seq):
    t = proj.shape[0]
    blk = SEQ_BLOCK
    nb = seq // blk
    width = HB * DB
    q_col = CH_BQ * CHUNK // width
    kv_col = CH_BKV
    z_col = CH_BZ * CHUNK // width
    assert CH_BQ * CHUNK % width == 0 and CH_BZ * CHUNK % width == 0
    need = 2 * (3 * blk * width * 2 + 2 * blk * CHUNK * 2 + bias.size * 4 + sink_cols.size * 4)
    return pl.pallas_call(
        _swa_kernel,
        grid=(batch, nb),
        in_specs=[pl.BlockSpec((blk, width), lambda b, n: (b * nb + n, q_col)),
                  pl.BlockSpec((blk, CHUNK), lambda b, n: (b * nb + jnp.maximum(n - 1, 0), kv_col)),
                  pl.BlockSpec((blk, CHUNK), lambda b, n: (b * nb + n, kv_col)),
                  pl.BlockSpec((blk, width), lambda b, n: (b * nb + n, z_col)),
                  pl.BlockSpec(bias.shape, lambda b, n: (0, 0, 0)),
                  pl.BlockSpec(sink_cols.shape, lambda b, n: (0, 0, 0))],
        out_specs=pl.BlockSpec((blk, width), lambda b, n: (b * nb + n, 0)),
        out_shape=jax.ShapeDtypeStruct((t, width), BF16),
        compiler_params=_params(("parallel", "arbitrary"), need + (24 << 20)),
        name="attn_swa",
    )(proj, proj, proj, proj, bias, sink_cols)


def _merge_kernel(oa_ref, ob_ref, oc_ref, wa_ref, wb_ref, wc_ref, ga_ref, gb_ref, gc_ref, y_ref):
    y = ga_ref[...].astype(F32) * jnp.dot(oa_ref[...], wa_ref[...], preferred_element_type=F32)
    y = y + gb_ref[...].astype(F32) * jnp.dot(ob_ref[...], wb_ref[...], preferred_element_type=F32)
    y = y + gc_ref[...].astype(F32) * jnp.dot(oc_ref[...], wc_ref[...], preferred_element_type=F32)
    y_ref[...] = y.astype(y_ref.dtype)


def _merge_proj(oa, ob, oc, wa, wb, wc, layer, proj):
    t, k = oa.shape
    tm, tn = 512, CHUNK
    gcols = D // tn
    o_spec = pl.BlockSpec((tm, k), lambda i, j: (i, 0))
    w_spec = pl.BlockSpec((None, k, tn), lambda i, j: (layer, 0, j))

    def gate_spec(branch):
        return pl.BlockSpec((tm, tn), lambda i, j: (i, CH_GATE + branch * gcols + j))

    need = 2 * (3 * tm * k * 2 + 3 * k * tn * 2 + 4 * tm * tn * 2) + 4 * tm * tn * 4
    return pl.pallas_call(
        _merge_kernel,
        grid=(t // tm, D // tn),
        in_specs=[o_spec, o_spec, o_spec, w_spec, w_spec, w_spec,
                  gate_spec(0), gate_spec(1), gate_spec(2)],
        out_specs=pl.BlockSpec((tm, tn), lambda i, j: (i, j)),
        out_shape=jax.ShapeDtypeStruct((t, D), BF16),
        compiler_params=_params(("parallel", "arbitrary"), need + (8 << 20)),
        name="merge_proj",
    )(oa, ob, oc, wa, wb, wc, proj, proj, proj)


def _out_kernel(y_ref, w_ref, x_ref, g_ref, o_ref, acc_ref, ssq_ref):
    i = pl.program_id(0)
    j = pl.program_id(1)
    slot = i % 2
    prev = 1 - slot

    @pl.when((i == 0) & (j == 0))
    def _():
        acc_ref[...] = jnp.zeros_like(acc_ref)
        ssq_ref[...] = jnp.zeros_like(ssq_ref)

    rs = lax.rsqrt(ssq_ref[prev] * (1.0 / D) + NORM_EPS)
    o_ref[...] = x_ref[...] + acc_ref[prev, j] * rs * g_ref[...]

    ss = jnp.where(j == 0, 0.0, ssq_ref[slot])
    for r in range(y_ref.shape[0] // ROW_CHUNK):
        rows = slice(r * ROW_CHUNK, (r + 1) * ROW_CHUNK)
        acc = jnp.dot(y_ref[rows, :], w_ref[...], preferred_element_type=F32)
        acc_ref[slot, j, rows, :] = acc
        ssq_ref[slot, rows, :] = ss[rows, :] + jnp.sum(acc * acc, axis=1, keepdims=True)


def _out_proj(y, w_o, layer, x2, g):
    t = y.shape[0]
    tm, tn = 512, CHUNK
    ni, nj = t // tm, D // tn
    need = 2 * (tm * D * 2 + D * tn * 2 + 2 * tm * tn * 4) + 2 * tm * D * 4 + 4 * tm * tn * 4
    return pl.pallas_call(
        _out_kernel,
        grid=(ni + 1, nj),
        in_specs=[pl.BlockSpec((tm, D), lambda i, j: (jnp.minimum(i, ni - 1), 0)),
                  pl.BlockSpec((None, D, tn), lambda i, j: (layer, 0, j)),
                  pl.BlockSpec((tm, tn), lambda i, j: (jnp.maximum(i - 1, 0), j)),
                  pl.BlockSpec((1, tn), lambda i, j: (0, j))],
        out_specs=pl.BlockSpec((tm, tn), lambda i, j: (jnp.maximum(i - 1, 0), j)),
        out_shape=jax.ShapeDtypeStruct((t, D), F32),
        scratch_shapes=[pltpu.VMEM((2, nj, tm, tn), F32), pltpu.VMEM((2, tm, 1), F32)],
        compiler_params=_params(("arbitrary", "arbitrary"), need + (8 << 20)),
        name="out_proj",
    )(y, w_o, x2, g.reshape(1, D))


def _layer_weights(w_in, w_uq, w_uk, w_uv, b_f):
    sizes = (A_QR, A_KVR, A_ROPE, HA * A_NOPE, HB * DB, KVB * DB, KVB * DB, HB * DB,
             HC * DC, HC * DC, HC * DC, HC, HC * DC)
    parts, start = [], 0
    for n in sizes:
        parts.append(w_in[:, start:start + n])
        start += n
    (a_cq, a_ckv, a_kr, a_z, b_q, b_k, b_v, b_z, c_q, c_k, c_v, c_f, c_z) = parts
    w_inr = jnp.concatenate([a_cq, a_ckv, b_q, c_q, c_k, c_v, b_z, a_z, c_z, b_k, b_v],
                            axis=1).astype(BF16)
    col_scale = jnp.ones((NP,), F32)
    col_scale = col_scale.at[CH_BQ * CHUNK:CH_CQ * CHUNK].set(DB ** -0.5 * LOG2E)
    col_scale = col_scale.at[CH_CQ * CHUNK:CH_CK * CHUNK].set(DC ** -0.5 * LOG2E)
    col_scale = col_scale.reshape(1, NP)

    pad = jnp.zeros((D, LANES - F_LANE - HC), w_in.dtype)
    w_misc = jnp.concatenate([a_kr, c_f, pad], axis=1).astype(BF16)
    bf_row = jnp.concatenate([jnp.zeros((F_LANE,), F32), b_f.astype(F32),
                              jnp.zeros((LANES - F_LANE - HC,), F32)]).reshape(1, LANES)

    wq = jnp.pad(w_uq, ((0, 0), (0, 0), (0, A_QPAD - A_NOPE - A_ROPE)))
    wq = wq.reshape(A_QR, HA * A_QPAD).astype(BF16)
    wkv = jnp.concatenate([w_uk.reshape(A_KVR, HA * A_NOPE), w_uv.reshape(A_KVR, HA * A_NOPE)],
                          axis=1).astype(BF16)
    return w_inr, col_scale, w_misc, bf_row, wq, wkv


def _sink_columns(sinks):
    s = (sinks.astype(F32) * LOG2E).reshape(KVB, GB // 2, 1, 2)
    s = jnp.broadcast_to(s, (KVB, GB // 2, SEQ_BLOCK, 2)).reshape(KVB, (GB // 2) * SEQ_BLOCK, 2)
    return jnp.pad(s, ((0, 0), (0, 0), (0, LANES - 2)))


def kernel(x, positions, rel_table, pre_norm, w_in, q_a_norm, kv_a_norm, w_uq, w_uk, w_uv,
           sinks, b_f, w_proj_a, w_proj_b, w_proj_c, w_merge, w_o, post_norm):
    batch, seq, _ = x.shape
    depth = w_in.shape[0]
    x2 = x.reshape(batch * seq, D)
    tables = _rope_tables(positions)
    bias = _rel_bias(rel_table)
    wm16, wo16 = w_merge.astype(BF16), w_o.astype(BF16)
    wa16, wb16, wc16 = (w.astype(BF16) for w in (w_proj_a, w_proj_b, w_proj_c))
    for l in range(depth):
        w_inr, col_scale, w_misc, bf_row, wq, wkv = _layer_weights(
            w_in[l], w_uq[l], w_uk[l], w_uv[l], b_f[l])
        h = _prenorm(x2, pre_norm[l])
        proj = _fused_proj(h, w_inr, wm16, l, col_scale)
        k_rope, ccol, crow = _misc_proj(h, w_misc, bf_row, tables[0], tables[1], batch, seq)
        q_a = _latent_proj(proj, 0, A_QR, q_a_norm[l], wq, tables,
                           (A_NOPE + A_ROPE) ** -0.5 * LOG2E)
        kv_a = _latent_proj(proj, CH_ACKV, A_KVR, kv_a_norm[l], wkv, None, 1.0)
        o_a = _attention_a(q_a, kv_a, k_rope, proj, batch, seq)
        o_b = _attention_b(proj, bias, _sink_columns(sinks[l]), batch, seq)
        o_c = _attention_c(proj, ccol, crow, batch, seq)
        y = _merge_proj(o_a, o_b, o_c, wa16, wb16, wc16, l, proj)
        x2 = _out_proj(y, wo16, l, x2, post_norm[l])
    return x2.reshape(batch, seq, D)
```

```python
import functools
import math

import jax
import jax.numpy as jnp
from jax import lax
from jax.experimental import pallas as pl
from jax.experimental.pallas import tpu as pltpu

F32 = jnp.float32
BF16 = jnp.bfloat16

D = 4096
SEQ_BLOCK = 128
NORM_EPS = 1e-6
MASKED = -1e30
LOG2E = math.log2(math.e)
HA = 16
A_QR = 1536
A_KVR = 512
A_NOPE = 128
A_ROPE = 64
A_HALF = A_ROPE // 2
A_QPAD = 256
THETA = 10000.0
HB = 32
KVB = 4
DB = 64
GB = HB // KVB
HC = 16
DC = 128
N_BUCKETS = 32
MAX_DIST = 128

LANES = 128
CHUNK = 512

CH_ACQ, CH_ACKV, CH_BQ, CH_CQ, CH_CK, CH_CV = 0, 3, 4, 8, 12, 16
CH_BZ, CH_AZ, CH_CZ = 20, 24, 28
CH_GATE = 32
CH_BKV = 56
N_ID_CHUNKS = 20
N_SILU_CHUNKS = 12
N_CHUNKS = 57
NP = N_CHUNKS * CHUNK
F_LANE = 64

VMEM_CAP = 60 * 1024 * 1024


def _vmem(nbytes):
    return int(min(VMEM_CAP, nbytes))


def _params(sem, nbytes):
    return pltpu.CompilerParams(dimension_semantics=sem, vmem_limit_bytes=_vmem(nbytes))


def _rope_table_kernel(pos_ref, freq_ref, cos_ref, sin_ref):
    ang = pos_ref[...].astype(F32) * freq_ref[...]
    lane = lax.broadcasted_iota(jnp.int32, ang.shape, 1)
    live = lane < A_ROPE
    s = jnp.sin(ang)
    cos_ref[...] = jnp.where(live, jnp.cos(ang), 0.0)
    sin_ref[...] = jnp.where(live, jnp.where(lane < A_HALF, -s, s), 0.0)


def _rope_tables(positions):
    t = positions.size
    tm = 2048
    half = jnp.arange(A_HALF, dtype=F32)
    inv_freq = THETA ** (-half / A_HALF)
    freq = jnp.concatenate([inv_freq, inv_freq, jnp.zeros((LANES - A_ROPE,), F32)]).reshape(1, LANES)
    pos = positions.reshape(t, 1)
    return pl.pallas_call(
        _rope_table_kernel,
        grid=(t // tm,),
        in_specs=[pl.BlockSpec((tm, 1), lambda i: (i, 0)),
                  pl.BlockSpec((1, LANES), lambda i: (0, 0))],
        out_specs=[pl.BlockSpec((tm, LANES), lambda i: (i, 0)),
                   pl.BlockSpec((tm, LANES), lambda i: (i, 0))],
        out_shape=[jax.ShapeDtypeStruct((t, LANES), F32)] * 2,
        compiler_params=_params(("arbitrary",), 32 << 20),
        name="rope_tables",
    )(pos, freq)


def _rope(t, cos, sin):
    lane = lax.broadcasted_iota(jnp.int32, t.shape, 1)
    partner = jnp.where(lane < A_HALF,
                        pltpu.roll(t, LANES - A_HALF, 1),
                        pltpu.roll(t, A_HALF, 1))
    return t * cos + partner * sin


def _relbias_kernel(table_ref, o_ref):
    h = pl.program_id(0)
    qi = lax.broadcasted_iota(jnp.int32, (SEQ_BLOCK, 2 * SEQ_BLOCK), 0)
    si = lax.broadcasted_iota(jnp.int32, (SEQ_BLOCK, 2 * SEQ_BLOCK), 1)
    dist = qi + SEQ_BLOCK - si
    max_exact = N_BUCKETS // 2
    d = jnp.maximum(dist, 0)
    large = max_exact + (jnp.log(jnp.maximum(d, 1).astype(F32) / max_exact)
                         / math.log(MAX_DIST / max_exact)
                         * (N_BUCKETS - max_exact)).astype(jnp.int32)
    large = jnp.minimum(large, N_BUCKETS - 1)
    bucket = jnp.where(d < max_exact, d, large)
    acc = jnp.zeros(bucket.shape, F32)
    for b in range(N_BUCKETS):
        acc = jnp.where(bucket == b, table_ref[b, h], acc)
    o_ref[0] = acc * LOG2E


def _rel_bias(rel_table):
    def out_map(h):
        return (h // GB, (h % GB) // 2, h % 2)
    return pl.pallas_call(
        _relbias_kernel,
        grid=(HB,),
        in_specs=[pl.BlockSpec(memory_space=pltpu.SMEM)],
        out_specs=pl.BlockSpec((1, SEQ_BLOCK, 2 * SEQ_BLOCK), out_map),
        out_shape=jax.ShapeDtypeStruct((KVB, (GB // 2) * SEQ_BLOCK, 4 * SEQ_BLOCK), F32),
        compiler_params=_params(("arbitrary",), 16 << 20),
        name="rel_bias",
    )(rel_table)


def _prenorm_kernel(x_ref, g_ref, o_ref):
    x = x_ref[...]
    r = lax.rsqrt(jnp.mean(x * x, axis=-1, keepdims=True) + NORM_EPS)
    o_ref[...] = (x * r * g_ref[...]).astype(o_ref.dtype)


def _prenorm(x2, g):
    t = x2.shape[0]
    tm = 256
    return pl.pallas_call(
        _prenorm_kernel,
        grid=(t // tm,),
        in_specs=[pl.BlockSpec((tm, D), lambda i: (i, 0)),
                  pl.BlockSpec((1, D), lambda i: (0, 0))],
        out_specs=pl.BlockSpec((tm, D), lambda i: (i, 0)),
        out_shape=jax.ShapeDtypeStruct((t, D), BF16),
        compiler_params=_params(("arbitrary",), 32 << 20),
        name="prenorm",
    )(x2, g.reshape(1, D))


ROW_CHUNK = 256


def _chunked_dot(x_ref, w_ref, o_ref, epilogue):
    for r in range(x_ref.shape[0] // ROW_CHUNK):
        rows = slice(r * ROW_CHUNK, (r + 1) * ROW_CHUNK)
        acc = jnp.dot(x_ref[rows, :], w_ref[...], preferred_element_type=F32)
        o_ref[rows, :] = epilogue(acc, rows).astype(o_ref.dtype)


def _proj_kernel(h_ref, wi_ref, wm_ref, cs_ref, o_ref, *, silu_start, gate_start, gate_end):
    j = pl.program_id(1)

    @pl.when((j < silu_start) | (j >= gate_end))
    def _():
        _chunked_dot(h_ref, wi_ref, o_ref, lambda a, rows: a * cs_ref[...])

    @pl.when((j >= silu_start) & (j < gate_start))
    def _():
        _chunked_dot(h_ref, wi_ref, o_ref, lambda a, rows: a * jax.nn.sigmoid(a))

    @pl.when((j >= gate_start) & (j < gate_end))
    def _():
        _chunked_dot(h_ref, wm_ref, o_ref, lambda a, rows: jax.nn.sigmoid(a))


def _fused_proj(h, w_inr, w_merge, layer, col_scale):
    t = h.shape[0]
    tm, tn = 2048, CHUNK
    n_gate = CH_BKV - CH_GATE
    kern = functools.partial(_proj_kernel, silu_start=CH_BZ, gate_start=CH_GATE, gate_end=CH_BKV)

    def wi_map(i, j):
        return (0, jnp.where(j < CH_GATE, j, jnp.where(j >= CH_BKV, j - n_gate, CH_GATE - 1)))

    def wm_map(i, j):
        return (layer, 0, jnp.clip(j - CH_GATE, 0, n_gate - 1))

    need = 2 * (tm * D * 2 + 2 * D * tn * 2 + tm * tn * 2) + 4 * tm * tn * 4
    return pl.pallas_call(
        kern,
        grid=(t // tm, NP // tn),
        in_specs=[pl.BlockSpec((tm, D), lambda i, j: (i, 0)),
                  pl.BlockSpec((D, tn), wi_map),
                  pl.BlockSpec((None, D, tn), wm_map),
                  pl.BlockSpec((1, tn), lambda i, j: (0, j))],
        out_specs=pl.BlockSpec((tm, tn), lambda i, j: (i, j)),
        out_shape=jax.ShapeDtypeStruct((t, NP), BF16),
        compiler_params=_params(("parallel", "arbitrary"), need + (8 << 20)),
        name="fused_proj",
    )(h, w_inr, w_merge, col_scale)


def _misc_kernel(h_ref, w_ref, bf_ref, cos_ref, sin_ref, kr_ref, ccol_ref, crow_ref, carry_ref):
    t = pl.program_id(1)
    acc = jnp.dot(h_ref[...], w_ref[...], preferred_element_type=F32)
    kr_ref[...] = _rope(acc, cos_ref[...], sin_ref[...]).astype(kr_ref.dtype)

    xx = acc + bf_ref[...]
    log_f = jnp.minimum(xx, 0.0) - jnp.log1p(jnp.exp(-jnp.abs(xx)))
    hi = log_f.astype(BF16)
    r1 = log_f - hi.astype(F32)
    mid = r1.astype(BF16)
    lo = (r1 - mid.astype(F32)).astype(BF16)
    tm = acc.shape[0]
    ri = lax.broadcasted_iota(jnp.int32, (tm, tm), 0)
    ci = lax.broadcasted_iota(jnp.int32, (tm, tm), 1)
    tri = jnp.where(ri >= ci, 1.0, 0.0).astype(BF16)
    cs = (jnp.dot(tri, hi, preferred_element_type=F32)
          + jnp.dot(tri, mid, preferred_element_type=F32)
          + jnp.dot(tri, lo, preferred_element_type=F32))

    @pl.when(t == 0)
    def _():
        carry_ref[...] = jnp.zeros_like(carry_ref)

    cs = cs + carry_ref[...]
    carry_ref[...] = cs[tm - 1:tm, :]
    cs2 = cs * LOG2E
    ccol_ref[...] = cs2
    crow_ref[0] = cs2.T


def _misc_proj(h, w_misc, bf_row, cos_t, sin_t, batch, seq):
    t = h.shape[0]
    tm = 512
    nt = seq // tm
    need = 2 * (tm * D * 2 + D * LANES * 2 + 5 * tm * LANES * 4) + 4 * tm * tm * 4
    return pl.pallas_call(
        _misc_kernel,
        grid=(batch, nt),
        in_specs=[pl.BlockSpec((tm, D), lambda b, i: (b * nt + i, 0)),
                  pl.BlockSpec((D, LANES), lambda b, i: (0, 0)),
                  pl.BlockSpec((1, LANES), lambda b, i: (0, 0)),
                  pl.BlockSpec((tm, LANES), lambda b, i: (b * nt + i, 0)),
                  pl.BlockSpec((tm, LANES), lambda b, i: (b * nt + i, 0))],
        out_specs=[pl.BlockSpec((tm, LANES), lambda b, i: (b * nt + i, 0)),
                   pl.BlockSpec((tm, LANES), lambda b, i: (b * nt + i, 0)),
                   pl.BlockSpec((1, LANES, tm), lambda b, i: (b, 0, i))],
        out_shape=[jax.ShapeDtypeStruct((t, LANES), BF16),
                   jax.ShapeDtypeStruct((t, LANES), F32),
                   jax.ShapeDtypeStruct((batch, LANES, seq), F32)],
        scratch_shapes=[pltpu.VMEM((1, LANES), F32)],
        compiler_params=_params(("arbitrary", "arbitrary"), need + (8 << 20)),
        name="misc_proj",
    )(h, w_misc, bf_row, cos_t, sin_t)


def _latent_kernel(*refs, rope, out_scale):
    if rope:
        x_ref, g_ref, w_ref, cos_ref, sin_ref, o_ref, xn_ref = refs
    else:
        x_ref, g_ref, w_ref, o_ref, xn_ref = refs

    @pl.when(pl.program_id(1) == 0)
    def _():
        x = x_ref[...].astype(F32)
        r = lax.rsqrt(jnp.mean(x * x, axis=-1, keepdims=True) + NORM_EPS)
        xn_ref[...] = (x * r * g_ref[...]).astype(xn_ref.dtype)

    def rotary(acc, rows):
        cos = cos_ref[rows, :]
        sin = sin_ref[rows, :]
        pieces = []
        for hh in range(acc.shape[1] // A_QPAD):
            base = hh * A_QPAD
            pieces.append(acc[:, base:base + A_NOPE])
            pieces.append(_rope(acc[:, base + A_NOPE:base + A_QPAD], cos, sin))
        return jnp.concatenate(pieces, axis=1) * out_scale

    _chunked_dot(xn_ref, w_ref, o_ref, rotary if rope else (lambda acc, rows: acc))


def _latent_proj(proj, col_block, k, g, w, tables, out_scale):
    t = proj.shape[0]
    n = w.shape[1]
    tm, tn = 1024, 1024
    rope = tables is not None
    kern = functools.partial(_latent_kernel, rope=rope, out_scale=out_scale)
    in_specs = [pl.BlockSpec((tm, k), lambda i, j: (i, col_block)),
                pl.BlockSpec((1, k), lambda i, j: (0, 0)),
                pl.BlockSpec((k, tn), lambda i, j: (0, j))]
    args = [proj, g.reshape(1, k), w]
    if rope:
        in_specs += [pl.BlockSpec((tm, LANES), lambda i, j: (i, 0))] * 2
        args += list(tables)
    need = 2 * (tm * k * 2 + k * tn * 2 + tm * tn * 2 + 2 * tm * LANES * 4) + tm * k * 2 + 4 * tm * tn * 4
    return pl.pallas_call(
        kern,
        grid=(t // tm, n // tn),
        in_specs=in_specs,
        out_specs=pl.BlockSpec((tm, tn), lambda i, j: (i, j)),
        out_shape=jax.ShapeDtypeStruct((t, n), BF16),
        scratch_shapes=[pltpu.VMEM((tm, k), BF16)],
        compiler_params=_params(("parallel", "arbitrary"), need + (8 << 20)),
        name="latent_q" if rope else "latent_kv",
    )(*args)


def _causal_attn_kernel(*refs, tq, fox):
    if fox:
        q_ref, k_ref, v_ref, z_ref, ccol_ref, crow_ref, o_ref = refs
        keys = k_ref
    else:
        q_ref, kn_ref, kr_ref, v_ref, z_ref, o_ref, kcat_ref = refs
        kcat_ref[:, 0:LANES] = kn_ref[...]
        kcat_ref[:, LANES:2 * LANES] = kr_ref[...]
        keys = kcat_ref
    seq = q_ref.shape[0]
    if fox:
        h = pl.program_id(1)
        lane = lax.broadcasted_iota(jnp.int32, (seq, LANES), 1)
        cq_all = jnp.sum(jnp.where(lane == F_LANE + h, ccol_ref[...], 0.0), axis=1, keepdims=True)
        ck_all = crow_ref[0, pl.ds(F_LANE + h, 1), :]
    row = lax.broadcasted_iota(jnp.int32, (tq, tq), 0)
    col = lax.broadcasted_iota(jnp.int32, (tq, tq), 1)
    causal = row >= col
    dims = (((1,), (1,)), ((), ()))

    for c in range(seq // tq):
        lo, hi = c * tq, (c + 1) * tq
        q = q_ref[lo:hi, :]
        s_d = lax.dot_general(q, keys[lo:hi, :], dims, preferred_element_type=F32)
        if fox:
            s_d = s_d + cq_all[lo:hi, :] - ck_all[:, lo:hi]
        s_d = jnp.where(causal, s_d, MASKED)
        m = jnp.max(s_d, axis=1, keepdims=True)
        if c:
            s_o = lax.dot_general(q, keys[0:lo, :], dims, preferred_element_type=F32)
            if fox:
                s_o = s_o + cq_all[lo:hi, :] - ck_all[:, 0:lo]
            m = jnp.maximum(m, jnp.max(s_o, axis=1, keepdims=True))
            p_o = jnp.exp2(s_o - m)
            l = jnp.sum(p_o, axis=1, keepdims=True)
            acc = jnp.dot(p_o.astype(BF16), v_ref[0:lo, :], preferred_element_type=F32)
        p_d = jnp.exp2(s_d - m)
        l_d = jnp.sum(p_d, axis=1, keepdims=True)
        acc_d = jnp.dot(p_d.astype(BF16), v_ref[lo:hi, :], preferred_element_type=F32)
        if c:
            l = l + l_d
            acc = acc + acc_d
        else:
            l, acc = l_d, acc_d
        o_ref[lo:hi, :] = (acc / l * z_ref[lo:hi, :].astype(F32)).astype(o_ref.dtype)


ATTN_TQ = 256


def _attention_a(q_a, kv_a, k_rope, proj, batch, seq):
    t = q_a.shape[0]
    kern = functools.partial(_causal_attn_kernel, tq=ATTN_TQ, fox=False)
    z_col = CH_AZ * CHUNK // LANES
    need = 2 * (seq * A_QPAD * 2 + 5 * seq * LANES * 2) + seq * A_QPAD * 2
    return pl.pallas_call(
        kern,
        grid=(batch, HA),
        in_specs=[pl.BlockSpec((seq, A_QPAD), lambda b, h: (b, h)),
                  pl.BlockSpec((seq, LANES), lambda b, h: (b, h)),
                  pl.BlockSpec((seq, LANES), lambda b, h: (b, 0)),
                  pl.BlockSpec((seq, LANES), lambda b, h: (b, HA + h)),
                  pl.BlockSpec((seq, LANES), lambda b, h: (b, z_col + h))],
        out_specs=pl.BlockSpec((seq, LANES), lambda b, h: (b, h)),
        out_shape=jax.ShapeDtypeStruct((t, HA * LANES), BF16),
        scratch_shapes=[pltpu.VMEM((seq, A_QPAD), BF16)],
        compiler_params=_params(("parallel", "arbitrary"), need + (24 << 20)),
        name="attn_mla",
    )(q_a, kv_a, k_rope, kv_a, proj)


def _attention_c(proj, ccol, crow, batch, seq):
    t = proj.shape[0]
    kern = functools.partial(_causal_attn_kernel, tq=ATTN_TQ, fox=True)
    q_col, k_col, v_col, z_col = (c * CHUNK // LANES for c in (CH_CQ, CH_CK, CH_CV, CH_CZ))
    need = 2 * (5 * seq * LANES * 2 + 2 * seq * LANES * 4)
    return pl.pallas_call(
        kern,
        grid=(batch, HC),
        in_specs=[pl.BlockSpec((seq, LANES), lambda b, h: (b, q_col + h)),
                  pl.BlockSpec((seq, LANES), lambda b, h: (b, k_col + h)),
                  pl.BlockSpec((seq, LANES), lambda b, h: (b, v_col + h)),
                  pl.BlockSpec((seq, LANES), lambda b, h: (b, z_col + h)),
                  pl.BlockSpec((seq, LANES), lambda b, h: (b, 0)),
                  pl.BlockSpec((1, LANES, seq), lambda b, h: (b, 0, 0))],
        out_specs=pl.BlockSpec((seq, LANES), lambda b, h: (b, h)),
        out_shape=jax.ShapeDtypeStruct((t, HC * DC), BF16),
        compiler_params=_params(("parallel", "arbitrary"), need + (24 << 20)),
        name="attn_fox",
    )(proj, proj, proj, proj, ccol, crow)


def _swa_kernel(q_ref, kvp_ref, kvc_ref, z_ref, bias_ref, sink_ref, o_ref):
    n = pl.program_id(1)
    blk = SEQ_BLOCK
    band = 2 * blk
    rows = (GB // 2) * blk
    kv = jnp.concatenate([kvp_ref[...], kvc_ref[...]], axis=0).astype(F32)
    lane = lax.broadcasted_iota(jnp.int32, (band, LANES), 1)
    low = lane < DB

    def split(two_heads):
        lo = jnp.where(low, two_heads, 0.0)
        hi = jnp.where(low, 0.0, two_heads)
        return (lo, pltpu.roll(lo, DB, 1)), (pltpu.roll(hi, DB, 1), hi)

    k_heads = split(kv[:, 0:LANES]) + split(kv[:, LANES:2 * LANES])
    v_heads = split(kv[:, 2 * LANES:3 * LANES]) + split(kv[:, 3 * LANES:4 * LANES])

    r = lax.broadcasted_iota(jnp.int32, (rows, band), 0) % blk
    c = lax.broadcasted_iota(jnp.int32, (rows, band), 1)
    dist = r + blk - c
    first_key = jnp.where(n > 0, 0, blk)
    valid = (dist >= 0) & (dist < blk) & (c >= first_key)

    for kh in range(KVB):
        k_cat = jnp.concatenate(k_heads[kh], axis=0).astype(BF16)
        v_cat = jnp.concatenate(v_heads[kh], axis=0).astype(BF16)
        q4 = jnp.concatenate(
            [q_ref[:, (kh * (GB // 2) + jj) * LANES:(kh * (GB // 2) + jj + 1) * LANES]
             for jj in range(GB // 2)], axis=0)
        s = lax.dot_general(q4, k_cat, (((1,), (1,)), ((), ())), preferred_element_type=F32)
        s = s + bias_ref[kh]
        halves, inv = [], []
        for e in range(2):
            se = jnp.where(valid, s[:, e * band:(e + 1) * band], MASKED)
            sink = sink_ref[kh, :, e:e + 1]
            m = jnp.maximum(jnp.max(se, axis=1, keepdims=True), sink)
            p = jnp.exp2(se - m)
            denom = jnp.sum(p, axis=1, keepdims=True) + jnp.exp2(sink - m)
            halves.append(p.astype(BF16))
            inv.append(1.0 / denom)
        p_cat = jnp.concatenate(halves, axis=1)
        o = jnp.dot(p_cat, v_cat, preferred_element_type=F32)
        o = o * jnp.where(lax.broadcasted_iota(jnp.int32, o.shape, 1) < DB, inv[0], inv[1])
        for jj in range(GB // 2):
            c0 = (kh * (GB // 2) + jj) * LANES
            o_ref[:, c0:c0 + LANES] = (o[jj * blk:(jj + 1) * blk, :]
                                       * z_ref[:, c0:c0 + LANES].astype(F32)).astype(o_ref.dtype)


def _attention_b(proj, bias, sink_cols, batch, seq):
    t = proj.shape[0]
    blk = SEQ_BLOCK
    nb = seq // blk
    width = HB * DB
    q_col = CH_BQ * CHUNK // width
    kv_col = CH_BKV
    z_col = CH_BZ * CHUNK // width
    assert CH_BQ * CHUNK % width == 0 and CH_BZ * CHUNK % width == 0
    need = 2 * (3 * blk * width * 2 + 2 * blk * CHUNK * 2 + bias.size * 4 + sink_cols.size * 4)
    return pl.pallas_call(
        _swa_kernel,
        grid=(batch, nb),
        in_specs=[pl.BlockSpec((blk, width), lambda b, n: (b * nb + n, q_col)),
                  pl.BlockSpec((blk, CHUNK), lambda b, n: (b * nb + jnp.maximum(n - 1, 0), kv_col)),
                  pl.BlockSpec((blk, CHUNK), lambda b, n: (b * nb + n, kv_col)),
                  pl.BlockSpec((blk, width), lambda b, n: (b * nb + n, z_col)),
                  pl.BlockSpec(bias.shape, lambda b, n: (0, 0, 0)),
                  pl.BlockSpec(sink_cols.shape, lambda b, n: (0, 0, 0))],
        out_specs=pl.BlockSpec((blk, width), lambda b, n: (b * nb + n, 0)),
        out_shape=jax.ShapeDtypeStruct((t, width), BF16),
        compiler_params=_params(("parallel", "arbitrary"), need + (24 << 20)),
        name="attn_swa",
    )(proj, proj, proj, proj, bias, sink_cols)


def _merge_kernel(oa_ref, ob_ref, oc_ref, wa_ref, wb_ref, wc_ref, ga_ref, gb_ref, gc_ref, y_ref):
    branches = ((oa_ref, wa_ref, ga_ref), (ob_ref, wb_ref, gb_ref), (oc_ref, wc_ref, gc_ref))
    for r in range(y_ref.shape[0] // ROW_CHUNK):
        rows = slice(r * ROW_CHUNK, (r + 1) * ROW_CHUNK)
        y = None
        for o_ref, w_ref, g_ref in branches:
            term = g_ref[rows, :].astype(F32) * jnp.dot(o_ref[rows, :], w_ref[...],
                                                        preferred_element_type=F32)
            y = term if y is None else y + term
        y_ref[rows, :] = y.astype(y_ref.dtype)


def _merge_proj(oa, ob, oc, wa, wb, wc, layer, proj):
    t, k = oa.shape
    tm, tn = 1024, CHUNK
    gcols = D // tn
    o_spec = pl.BlockSpec((tm, k), lambda i, j: (i, 0))
    w_spec = pl.BlockSpec((None, k, tn), lambda i, j: (layer, 0, j))

    def gate_spec(branch):
        return pl.BlockSpec((tm, tn), lambda i, j: (i, CH_GATE + branch * gcols + j))

    need = 2 * (3 * tm * k * 2 + 3 * k * tn * 2 + 4 * tm * tn * 2) + 4 * tm * tn * 4
    return pl.pallas_call(
        _merge_kernel,
        grid=(t // tm, D // tn),
        in_specs=[o_spec, o_spec, o_spec, w_spec, w_spec, w_spec,
                  gate_spec(0), gate_spec(1), gate_spec(2)],
        out_specs=pl.BlockSpec((tm, tn), lambda i, j: (i, j)),
        out_shape=jax.ShapeDtypeStruct((t, D), BF16),
        compiler_params=_params(("parallel", "arbitrary"), need + (8 << 20)),
        name="merge_proj",
    )(oa, ob, oc, wa, wb, wc, proj, proj, proj)


def _out_kernel(y_ref, w_ref, x_ref, g_ref, o_ref, acc_ref, ssq_ref):
    i = pl.program_id(0)
    j = pl.program_id(1)
    slot = i % 2
    prev = 1 - slot

    @pl.when((i == 0) & (j == 0))
    def _():
        acc_ref[...] = jnp.zeros_like(acc_ref)
        ssq_ref[...] = jnp.zeros_like(ssq_ref)

    rs = lax.rsqrt(ssq_ref[prev] * (1.0 / D) + NORM_EPS)
    o_ref[...] = x_ref[...] + acc_ref[prev, j] * rs * g_ref[...]

    ss = jnp.where(j == 0, 0.0, ssq_ref[slot])
    for r in range(y_ref.shape[0] // ROW_CHUNK):
        rows = slice(r * ROW_CHUNK, (r + 1) * ROW_CHUNK)
        acc = jnp.dot(y_ref[rows, :], w_ref[...], preferred_element_type=F32)
        acc_ref[slot, j, rows, :] = acc
        ssq_ref[slot, rows, :] = ss[rows, :] + jnp.sum(acc * acc, axis=1, keepdims=True)


def _out_proj(y, w_o, layer, x2, g):
    t = y.shape[0]
    tm, tn = 512, 2 * CHUNK
    ni, nj = t // tm, D // tn
    need = 2 * (tm * D * 2 + D * tn * 2 + 2 * tm * tn * 4) + 2 * tm * D * 4 + 4 * tm * tn * 4
    return pl.pallas_call(
        _out_kernel,
        grid=(ni + 1, nj),
        in_specs=[pl.BlockSpec((tm, D), lambda i, j: (jnp.minimum(i, ni - 1), 0)),
                  pl.BlockSpec((None, D, tn), lambda i, j: (layer, 0, j)),
                  pl.BlockSpec((tm, tn), lambda i, j: (jnp.maximum(i - 1, 0), j)),
                  pl.BlockSpec((1, tn), lambda i, j: (0, j))],
        out_specs=pl.BlockSpec((tm, tn), lambda i, j: (jnp.maximum(i - 1, 0), j)),
        out_shape=jax.ShapeDtypeStruct((t, D), F32),
        scratch_shapes=[pltpu.VMEM((2, nj, tm, tn), F32), pltpu.VMEM((2, tm, 1), F32)],
        compiler_params=_params(("arbitrary", "arbitrary"), need + (8 << 20)),
        name="out_proj",
    )(y, w_o, x2, g.reshape(1, D))


def _layer_weights(w_in, w_uq, w_uk, w_uv, b_f):
    sizes = (A_QR, A_KVR, A_ROPE, HA * A_NOPE, HB * DB, KVB * DB, KVB * DB, HB * DB,
             HC * DC, HC * DC, HC * DC, HC, HC * DC)
    parts, start = [], 0
    for n in sizes:
        parts.append(w_in[:, start:start + n])
        start += n
    (a_cq, a_ckv, a_kr, a_z, b_q, b_k, b_v, b_z, c_q, c_k, c_v, c_f, c_z) = parts
    w_inr = jnp.concatenate([a_cq, a_ckv, b_q, c_q, c_k, c_v, b_z, a_z, c_z, b_k, b_v],
                            axis=1).astype(BF16)
    col_scale = jnp.ones((NP,), F32)
    col_scale = col_scale.at[CH_BQ * CHUNK:CH_CQ * CHUNK].set(DB ** -0.5 * LOG2E)
    col_scale = col_scale.at[CH_CQ * CHUNK:CH_CK * CHUNK].set(DC ** -0.5 * LOG2E)
    col_scale = col_scale.reshape(1, NP)

    pad = jnp.zeros((D, LANES - F_LANE - HC), w_in.dtype)
    w_misc = jnp.concatenate([a_kr, c_f, pad], axis=1).astype(BF16)
    bf_row = jnp.concatenate([jnp.zeros((F_LANE,), F32), b_f.astype(F32),
                              jnp.zeros((LANES - F_LANE - HC,), F32)]).reshape(1, LANES)

    wq = jnp.pad(w_uq, ((0, 0), (0, 0), (0, A_QPAD - A_NOPE - A_ROPE)))
    wq = wq.reshape(A_QR, HA * A_QPAD).astype(BF16)
    wkv = jnp.concatenate([w_uk.reshape(A_KVR, HA * A_NOPE), w_uv.reshape(A_KVR, HA * A_NOPE)],
                          axis=1).astype(BF16)
    return w_inr, col_scale, w_misc, bf_row, wq, wkv


def _sink_columns(sinks):
    s = (sinks.astype(F32) * LOG2E).reshape(KVB, GB // 2, 1, 2)
    s = jnp.broadcast_to(s, (KVB, GB // 2, SEQ_BLOCK, 2)).reshape(KVB, (GB // 2) * SEQ_BLOCK, 2)
    return jnp.pad(s, ((0, 0), (0, 0), (0, LANES - 2)))


def kernel(x, positions, rel_table, pre_norm, w_in, q_a_norm, kv_a_norm, w_uq, w_uk, w_uv,
           sinks, b_f, w_proj_a, w_proj_b, w_proj_c, w_merge, w_o, post_norm):
    batch, seq, _ = x.shape
    depth = w_in.shape[0]
    x2 = x.reshape(batch * seq, D)
    tables = _rope_tables(positions)
    bias = _rel_bias(rel_table)
    wm16, wo16 = w_merge.astype(BF16), w_o.astype(BF16)
    wa16, wb16, wc16 = (w.astype(BF16) for w in (w_proj_a, w_proj_b, w_proj_c))
    for l in range(depth):
        w_inr, col_scale, w_misc, bf_row, wq, wkv = _layer_weights(
            w_in[l], w_uq[l], w_uk[l], w_uv[l], b_f[l])
        h = _prenorm(x2, pre_norm[l])
        proj = _fused_proj(h, w_inr, wm16, l, col_scale)
        k_rope, ccol, crow = _misc_proj(h, w_misc, bf_row, tables[0], tables[1], batch, seq)
        q_a = _latent_proj(proj, 0, A_QR, q_a_norm[l], wq, tables,
                           (A_NOPE + A_ROPE) ** -0.5 * LOG2E)
        kv_a = _latent_proj(proj, CH_ACKV, A_KVR, kv_a_norm[l], wkv, None, 1.0)
        o_a = _attention_a(q_a, kv_a, k_rope, proj, batch, seq)
        o_b = _attention_b(proj, bias, _sink_columns(sinks[l]), batch, seq)
        o_c = _attention_c(proj, ccol, crow, batch, seq)
        y = _merge_proj(o_a, o_b, o_c, wa16, wb16, wc16, l, proj)
        x2 = _out_proj(y, wo16, l, x2, post_norm[l])
    return x2.reshape(batch, seq, D)
```

```python
import functools
import math

import jax
import jax.numpy as jnp
from jax import lax
from jax.experimental import pallas as pl
from jax.experimental.pallas import tpu as pltpu

F32 = jnp.float32
BF16 = jnp.bfloat16

D = 4096
SEQ_BLOCK = 128
NORM_EPS = 1e-6
MASKED = -1e30
LOG2E = math.log2(math.e)
HA = 16
A_QR = 1536
A_KVR = 512
A_NOPE = 128
A_ROPE = 64
A_HALF = A_ROPE // 2
A_QPAD = 256
THETA = 10000.0
HB = 32
KVB = 4
DB = 64
GB = HB // KVB
HC = 16
DC = 128
N_BUCKETS = 32
MAX_DIST = 128

LANES = 128
CHUNK = 512

CH_ACQ, CH_ACKV, CH_BQ, CH_CQ, CH_CK, CH_CV = 0, 3, 4, 8, 12, 16
CH_BZ, CH_AZ, CH_CZ = 20, 24, 28
CH_GATE = 32
CH_BKV = 56
N_ID_CHUNKS = 20
N_SILU_CHUNKS = 12
N_CHUNKS = 57
NP = N_CHUNKS * CHUNK
F_LANE = 64

VMEM_CAP = 60 * 1024 * 1024


def _vmem(nbytes):
    return int(min(VMEM_CAP, nbytes))


def _params(sem, nbytes):
    return pltpu.CompilerParams(dimension_semantics=sem, vmem_limit_bytes=_vmem(nbytes))


def _rope_table_kernel(pos_ref, freq_ref, cos_ref, sin_ref):
    ang = pos_ref[...].astype(F32) * freq_ref[...]
    lane = lax.broadcasted_iota(jnp.int32, ang.shape, 1)
    live = lane < A_ROPE
    s = jnp.sin(ang)
    cos_ref[...] = jnp.where(live, jnp.cos(ang), 0.0)
    sin_ref[...] = jnp.where(live, jnp.where(lane < A_HALF, -s, s), 0.0)


def _rope_tables(positions):
    t = positions.size
    tm = 2048
    half = jnp.arange(A_HALF, dtype=F32)
    inv_freq = THETA ** (-half / A_HALF)
    freq = jnp.concatenate([inv_freq, inv_freq, jnp.zeros((LANES - A_ROPE,), F32)]).reshape(1, LANES)
    pos = positions.reshape(t, 1)
    return pl.pallas_call(
        _rope_table_kernel,
        grid=(t // tm,),
        in_specs=[pl.BlockSpec((tm, 1), lambda i: (i, 0)),
                  pl.BlockSpec((1, LANES), lambda i: (0, 0))],
        out_specs=[pl.BlockSpec((tm, LANES), lambda i: (i, 0)),
                   pl.BlockSpec((tm, LANES), lambda i: (i, 0))],
        out_shape=[jax.ShapeDtypeStruct((t, LANES), F32)] * 2,
        compiler_params=_params(("arbitrary",), 32 << 20),
        name="rope_tables",
    )(pos, freq)


def _rope(t, cos, sin):
    lane = lax.broadcasted_iota(jnp.int32, t.shape, 1)
    partner = jnp.where(lane < A_HALF,
                        pltpu.roll(t, LANES - A_HALF, 1),
                        pltpu.roll(t, A_HALF, 1))
    return t * cos + partner * sin


def _relbias_kernel(table_ref, o_ref):
    h = pl.program_id(0)
    qi = lax.broadcasted_iota(jnp.int32, (SEQ_BLOCK, 2 * SEQ_BLOCK), 0)
    si = lax.broadcasted_iota(jnp.int32, (SEQ_BLOCK, 2 * SEQ_BLOCK), 1)
    dist = qi + SEQ_BLOCK - si
    max_exact = N_BUCKETS // 2
    d = jnp.maximum(dist, 0)
    large = max_exact + (jnp.log(jnp.maximum(d, 1).astype(F32) / max_exact)
                         / math.log(MAX_DIST / max_exact)
                         * (N_BUCKETS - max_exact)).astype(jnp.int32)
    large = jnp.minimum(large, N_BUCKETS - 1)
    bucket = jnp.where(d < max_exact, d, large)
    acc = jnp.zeros(bucket.shape, F32)
    for b in range(N_BUCKETS):
        acc = jnp.where(bucket == b, table_ref[b, h], acc)
    o_ref[0] = acc * LOG2E


def _rel_bias(rel_table):
    def out_map(h):
        return (h // GB, (h % GB) // 2, h % 2)
    return pl.pallas_call(
        _relbias_kernel,
        grid=(HB,),
        in_specs=[pl.BlockSpec(memory_space=pltpu.SMEM)],
        out_specs=pl.BlockSpec((1, SEQ_BLOCK, 2 * SEQ_BLOCK), out_map),
        out_shape=jax.ShapeDtypeStruct((KVB, (GB // 2) * SEQ_BLOCK, 4 * SEQ_BLOCK), F32),
        compiler_params=_params(("arbitrary",), 16 << 20),
        name="rel_bias",
    )(rel_table)


def _prenorm_kernel(x_ref, g_ref, o_ref):
    x = x_ref[...]
    r = lax.rsqrt(jnp.mean(x * x, axis=-1, keepdims=True) + NORM_EPS)
    o_ref[...] = (x * r * g_ref[...]).astype(o_ref.dtype)


def _prenorm(x2, g):
    t = x2.shape[0]
    tm = 256
    return pl.pallas_call(
        _prenorm_kernel,
        grid=(t // tm,),
        in_specs=[pl.BlockSpec((tm, D), lambda i: (i, 0)),
                  pl.BlockSpec((1, D), lambda i: (0, 0))],
        out_specs=pl.BlockSpec((tm, D), lambda i: (i, 0)),
        out_shape=jax.ShapeDtypeStruct((t, D), BF16),
        compiler_params=_params(("arbitrary",), 32 << 20),
        name="prenorm",
    )(x2, g.reshape(1, D))


ROW_CHUNK = 256


def _chunked_dot(x_ref, w_ref, o_ref, epilogue):
    for r in range(x_ref.shape[0] // ROW_CHUNK):
        rows = slice(r * ROW_CHUNK, (r + 1) * ROW_CHUNK)
        acc = jnp.dot(x_ref[rows, :], w_ref[...], preferred_element_type=F32)
        o_ref[rows, :] = epilogue(acc, rows).astype(o_ref.dtype)


def _proj_kernel(h_ref, wi_ref, wm_ref, cs_ref, o_ref, *, silu_start, gate_start, gate_end):
    j = pl.program_id(1)

    @pl.when((j < silu_start) | (j >= gate_end))
    def _():
        _chunked_dot(h_ref, wi_ref, o_ref, lambda a, rows: a * cs_ref[...])

    @pl.when((j >= silu_start) & (j < gate_start))
    def _():
        _chunked_dot(h_ref, wi_ref, o_ref, lambda a, rows: a * jax.nn.sigmoid(a))

    @pl.when((j >= gate_start) & (j < gate_end))
    def _():
        _chunked_dot(h_ref, wm_ref, o_ref, lambda a, rows: jax.nn.sigmoid(a))


def _fused_proj(h, w_inr, w_merge, layer, col_scale):
    t = h.shape[0]
    tm, tn = 2048, CHUNK
    n_gate = CH_BKV - CH_GATE
    kern = functools.partial(_proj_kernel, silu_start=CH_BZ, gate_start=CH_GATE, gate_end=CH_BKV)

    def wi_map(i, j):
        return (layer, 0,
                jnp.where(j < CH_GATE, j, jnp.where(j >= CH_BKV, j - n_gate, CH_GATE - 1)))

    def wm_map(i, j):
        return (layer, 0, jnp.clip(j - CH_GATE, 0, n_gate - 1))

    need = 2 * (tm * D * 2 + 2 * D * tn * 2 + tm * tn * 2) + 4 * tm * tn * 4
    return pl.pallas_call(
        kern,
        grid=(t // tm, NP // tn),
        in_specs=[pl.BlockSpec((tm, D), lambda i, j: (i, 0)),
                  pl.BlockSpec((None, D, tn), wi_map),
                  pl.BlockSpec((None, D, tn), wm_map),
                  pl.BlockSpec((1, tn), lambda i, j: (0, j))],
        out_specs=pl.BlockSpec((tm, tn), lambda i, j: (i, j)),
        out_shape=jax.ShapeDtypeStruct((t, NP), BF16),
        compiler_params=_params(("parallel", "arbitrary"), need + (8 << 20)),
        name="fused_proj",
    )(h, w_inr, w_merge, col_scale)


def _misc_kernel(h_ref, w_ref, bf_ref, cos_ref, sin_ref, kr_ref, ccol_ref, crow_ref, carry_ref):
    t = pl.program_id(1)
    acc = jnp.dot(h_ref[...], w_ref[...], preferred_element_type=F32)
    kr_ref[...] = _rope(acc, cos_ref[...], sin_ref[...]).astype(kr_ref.dtype)

    xx = acc + bf_ref[...]
    log_f = jnp.minimum(xx, 0.0) - jnp.log1p(jnp.exp(-jnp.abs(xx)))
    hi = log_f.astype(BF16)
    r1 = log_f - hi.astype(F32)
    mid = r1.astype(BF16)
    lo = (r1 - mid.astype(F32)).astype(BF16)
    tm = acc.shape[0]
    ri = lax.broadcasted_iota(jnp.int32, (tm, tm), 0)
    ci = lax.broadcasted_iota(jnp.int32, (tm, tm), 1)
    tri = jnp.where(ri >= ci, 1.0, 0.0).astype(BF16)
    cs = (jnp.dot(tri, hi, preferred_element_type=F32)
          + jnp.dot(tri, mid, preferred_element_type=F32)
          + jnp.dot(tri, lo, preferred_element_type=F32))

    @pl.when(t == 0)
    def _():
        carry_ref[...] = jnp.zeros_like(carry_ref)

    cs = cs + carry_ref[...]
    carry_ref[...] = cs[tm - 1:tm, :]
    cs2 = cs * LOG2E
    ccol_ref[...] = cs2
    crow_ref[0] = cs2.T


def _misc_proj(h, w_misc, layer, bf_row, cos_t, sin_t, batch, seq):
    t = h.shape[0]
    tm = 512
    nt = seq // tm
    need = 2 * (tm * D * 2 + D * LANES * 2 + 5 * tm * LANES * 4) + 4 * tm * tm * 4
    return pl.pallas_call(
        _misc_kernel,
        grid=(batch, nt),
        in_specs=[pl.BlockSpec((tm, D), lambda b, i: (b * nt + i, 0)),
                  pl.BlockSpec((None, D, LANES), lambda b, i: (layer, 0, 0)),
                  pl.BlockSpec((1, LANES), lambda b, i: (0, 0)),
                  pl.BlockSpec((tm, LANES), lambda b, i: (b * nt + i, 0)),
                  pl.BlockSpec((tm, LANES), lambda b, i: (b * nt + i, 0))],
        out_specs=[pl.BlockSpec((tm, LANES), lambda b, i: (b * nt + i, 0)),
                   pl.BlockSpec((tm, LANES), lambda b, i: (b * nt + i, 0)),
                   pl.BlockSpec((1, LANES, tm), lambda b, i: (b, 0, i))],
        out_shape=[jax.ShapeDtypeStruct((t, LANES), BF16),
                   jax.ShapeDtypeStruct((t, LANES), F32),
                   jax.ShapeDtypeStruct((batch, LANES, seq), F32)],
        scratch_shapes=[pltpu.VMEM((1, LANES), F32)],
        compiler_params=_params(("arbitrary", "arbitrary"), need + (8 << 20)),
        name="misc_proj",
    )(h, w_misc, bf_row, cos_t, sin_t)


def _latent_kernel(*refs, rope, out_scale):
    if rope:
        x_ref, g_ref, w_ref, cos_ref, sin_ref, o_ref, xn_ref = refs
    else:
        x_ref, g_ref, w_ref, o_ref, xn_ref = refs

    @pl.when(pl.program_id(1) == 0)
    def _():
        x = x_ref[...].astype(F32)
        r = lax.rsqrt(jnp.mean(x * x, axis=-1, keepdims=True) + NORM_EPS)
        xn_ref[...] = (x * r * g_ref[...]).astype(xn_ref.dtype)

    def rotary(acc, rows):
        cos = cos_ref[rows, :]
        sin = sin_ref[rows, :]
        pieces = []
        for hh in range(acc.shape[1] // A_QPAD):
            base = hh * A_QPAD
            pieces.append(acc[:, base:base + A_NOPE])
            pieces.append(_rope(acc[:, base + A_NOPE:base + A_QPAD], cos, sin))
        return jnp.concatenate(pieces, axis=1) * out_scale

    _chunked_dot(xn_ref, w_ref, o_ref, rotary if rope else (lambda acc, rows: acc))


def _latent_proj(proj, col_block, k, g, w, tables, out_scale):
    t = proj.shape[0]
    n = w.shape[1]
    tm, tn = 1024, 1024
    rope = tables is not None
    kern = functools.partial(_latent_kernel, rope=rope, out_scale=out_scale)
    in_specs = [pl.BlockSpec((tm, k), lambda i, j: (i, col_block)),
                pl.BlockSpec((1, k), lambda i, j: (0, 0)),
                pl.BlockSpec((k, tn), lambda i, j: (0, j))]
    args = [proj, g.reshape(1, k), w]
    if rope:
        in_specs += [pl.BlockSpec((tm, LANES), lambda i, j: (i, 0))] * 2
        args += list(tables)
    need = 2 * (tm * k * 2 + k * tn * 2 + tm * tn * 2 + 2 * tm * LANES * 4) + tm * k * 2 + 4 * tm * tn * 4
    return pl.pallas_call(
        kern,
        grid=(t // tm, n // tn),
        in_specs=in_specs,
        out_specs=pl.BlockSpec((tm, tn), lambda i, j: (i, j)),
        out_shape=jax.ShapeDtypeStruct((t, n), BF16),
        scratch_shapes=[pltpu.VMEM((tm, k), BF16)],
        compiler_params=_params(("parallel", "arbitrary"), need + (8 << 20)),
        name="latent_q" if rope else "latent_kv",
    )(*args)


def _causal_attn_kernel(*refs, tq, fox):
    if fox:
        q_ref, k_ref, v_ref, z_ref, ccol_ref, crow_ref, o_ref, vaug_ref = refs
        keys = k_ref
    else:
        q_ref, kn_ref, kr_ref, v_ref, z_ref, o_ref, kcat_ref, vaug_ref = refs
        kcat_ref[:, 0:LANES] = kn_ref[...]
        kcat_ref[:, LANES:2 * LANES] = kr_ref[...]
        keys = kcat_ref
    seq = q_ref.shape[0]
    lane = lax.broadcasted_iota(jnp.int32, (seq, LANES), 1)
    vaug_ref[:, 0:LANES] = v_ref[...]
    vaug_ref[:, LANES:2 * LANES] = jnp.where(lane == 0, 1.0, 0.0).astype(BF16)
    if fox:
        h = pl.program_id(1)
        cq_all = jnp.sum(jnp.where(lane == F_LANE + h, ccol_ref[...], 0.0), axis=1, keepdims=True)
        ck_all = crow_ref[0, pl.ds(F_LANE + h, 1), :]
    row = lax.broadcasted_iota(jnp.int32, (tq, tq), 0)
    col = lax.broadcasted_iota(jnp.int32, (tq, tq), 1)
    causal = row >= col
    dims = (((1,), (1,)), ((), ()))

    for c in range(seq // tq):
        lo, hi = c * tq, (c + 1) * tq
        q = q_ref[lo:hi, :]
        t_d = lax.dot_general(q, keys[lo:hi, :], dims, preferred_element_type=F32)
        if fox:
            t_d = t_d - ck_all[:, lo:hi]
        t_d = jnp.where(causal, t_d, MASKED)
        mt = jnp.max(t_d, axis=1, keepdims=True)
        if c:
            t_o = lax.dot_general(q, keys[0:lo, :], dims, preferred_element_type=F32)
            if fox:
                t_o = t_o - ck_all[:, 0:lo]
            mt = jnp.maximum(mt, jnp.max(t_o, axis=1, keepdims=True))
        if fox:
            cq = cq_all[lo:hi, :]
            shift = (mt + cq) - cq
        else:
            shift = mt
        acc = jnp.dot(jnp.exp2(t_d - shift).astype(BF16), vaug_ref[lo:hi, :],
                      preferred_element_type=F32)
        if c:
            acc = acc + jnp.dot(jnp.exp2(t_o - shift).astype(BF16), vaug_ref[0:lo, :],
                                preferred_element_type=F32)
        inv_l = 1.0 / acc[:, LANES:LANES + 1]
        o_ref[lo:hi, :] = (acc[:, 0:LANES] * inv_l * z_ref[lo:hi, :].astype(F32)).astype(o_ref.dtype)


ATTN_TQ = 256


def _attention_a(q_a, kv_a, k_rope, proj, batch, seq):
    t = q_a.shape[0]
    kern = functools.partial(_causal_attn_kernel, tq=ATTN_TQ, fox=False)
    z_col = CH_AZ * CHUNK // LANES
    need = 2 * (seq * A_QPAD * 2 + 5 * seq * LANES * 2) + seq * A_QPAD * 2
    return pl.pallas_call(
        kern,
        grid=(batch, HA),
        in_specs=[pl.BlockSpec((seq, A_QPAD), lambda b, h: (b, h)),
                  pl.BlockSpec((seq, LANES), lambda b, h: (b, h)),
                  pl.BlockSpec((seq, LANES), lambda b, h: (b, 0)),
                  pl.BlockSpec((seq, LANES), lambda b, h: (b, HA + h)),
                  pl.BlockSpec((seq, LANES), lambda b, h: (b, z_col + h))],
        out_specs=pl.BlockSpec((seq, LANES), lambda b, h: (b, h)),
        out_shape=jax.ShapeDtypeStruct((t, HA * LANES), BF16),
        scratch_shapes=[pltpu.VMEM((seq, A_QPAD), BF16), pltpu.VMEM((seq, 2 * LANES), BF16)],
        compiler_params=_params(("parallel", "arbitrary"), need + (24 << 20)),
        name="attn_mla",
    )(q_a, kv_a, k_rope, kv_a, proj)


def _attention_c(proj, ccol, crow, batch, seq):
    t = proj.shape[0]
    kern = functools.partial(_causal_attn_kernel, tq=ATTN_TQ, fox=True)
    q_col, k_col, v_col, z_col = (c * CHUNK // LANES for c in (CH_CQ, CH_CK, CH_CV, CH_CZ))
    need = 2 * (5 * seq * LANES * 2 + 2 * seq * LANES * 4)
    return pl.pallas_call(
        kern,
        grid=(batch, HC),
        in_specs=[pl.BlockSpec((seq, LANES), lambda b, h: (b, q_col + h)),
                  pl.BlockSpec((seq, LANES), lambda b, h: (b, k_col + h)),
                  pl.BlockSpec((seq, LANES), lambda b, h: (b, v_col + h)),
                  pl.BlockSpec((seq, LANES), lambda b, h: (b, z_col + h)),
                  pl.BlockSpec((seq, LANES), lambda b, h: (b, 0)),
                  pl.BlockSpec((1, LANES, seq), lambda b, h: (b, 0, 0))],
        out_specs=pl.BlockSpec((seq, LANES), lambda b, h: (b, h)),
        out_shape=jax.ShapeDtypeStruct((t, HC * DC), BF16),
        scratch_shapes=[pltpu.VMEM((seq, 2 * LANES), BF16)],
        compiler_params=_params(("parallel", "arbitrary"), need + (24 << 20)),
        name="attn_fox",
    )(proj, proj, proj, proj, ccol, crow)


def _swa_kernel(q_ref, kvp_ref, kvc_ref, z_ref, bias_ref, sink_ref, o_ref):
    n = pl.program_id(1)
    blk = SEQ_BLOCK
    band = 2 * blk
    rows = (GB // 2) * blk
    kv = jnp.concatenate([kvp_ref[...], kvc_ref[...]], axis=0).astype(F32)
    lane = lax.broadcasted_iota(jnp.int32, (band, LANES), 1)
    low = lane < DB

    def split(two_heads):
        lo = jnp.where(low, two_heads, 0.0)
        hi = jnp.where(low, 0.0, two_heads)
        return (lo, pltpu.roll(lo, DB, 1)), (pltpu.roll(hi, DB, 1), hi)

    k_heads = split(kv[:, 0:LANES]) + split(kv[:, LANES:2 * LANES])
    v_heads = split(kv[:, 2 * LANES:3 * LANES]) + split(kv[:, 3 * LANES:4 * LANES])

    r = lax.broadcasted_iota(jnp.int32, (rows, band), 0) % blk
    c = lax.broadcasted_iota(jnp.int32, (rows, band), 1)
    dist = r + blk - c
    first_key = jnp.where(n > 0, 0, blk)
    valid = (dist >= 0) & (dist < blk) & (c >= first_key)

    for kh in range(KVB):
        k_cat = jnp.concatenate(k_heads[kh], axis=0).astype(BF16)
        v_cat = jnp.concatenate(v_heads[kh], axis=0).astype(BF16)
        q4 = jnp.concatenate(
            [q_ref[:, (kh * (GB // 2) + jj) * LANES:(kh * (GB // 2) + jj + 1) * LANES]
             for jj in range(GB // 2)], axis=0)
        s = lax.dot_general(q4, k_cat, (((1,), (1,)), ((), ())), preferred_element_type=F32)
        s = s + bias_ref[kh]
        halves, inv = [], []
        for e in range(2):
            se = jnp.where(valid, s[:, e * band:(e + 1) * band], MASKED)
            sink = sink_ref[kh, :, e:e + 1]
            m = jnp.maximum(jnp.max(se, axis=1, keepdims=True), sink)
            p = jnp.exp2(se - m)
            denom = jnp.sum(p, axis=1, keepdims=True) + jnp.exp2(sink - m)
            halves.append(p.astype(BF16))
            inv.append(1.0 / denom)
        p_cat = jnp.concatenate(halves, axis=1)
        o = jnp.dot(p_cat, v_cat, preferred_element_type=F32)
        o = o * jnp.where(lax.broadcasted_iota(jnp.int32, o.shape, 1) < DB, inv[0], inv[1])
        for jj in range(GB // 2):
            c0 = (kh * (GB // 2) + jj) * LANES
            o_ref[:, c0:c0 + LANES] = (o[jj * blk:(jj + 1) * blk, :]
                                       * z_ref[:, c0:c0 + LANES].astype(F32)).astype(o_ref.dtype)


def _attention_b(proj, bias, sink_cols, batch, seq):
    t = proj.shape[0]
    blk = SEQ_BLOCK
    nb = seq // blk
    width = HB * DB
    q_col = CH_BQ * CHUNK // width
    kv_col = CH_BKV
    z_col = CH_BZ * CHUNK // width
    assert CH_BQ * CHUNK % width == 0 and CH_BZ * CHUNK % width == 0
    need = 2 * (3 * blk * width * 2 + 2 * blk * CHUNK * 2 + bias.size * 4 + sink_cols.size * 4)
    return pl.pallas_call(
        _swa_kernel,
        grid=(batch, nb),
        in_specs=[pl.BlockSpec((blk, width), lambda b, n: (b * nb + n, q_col)),
                  pl.BlockSpec((blk, CHUNK), lambda b, n: (b * nb + jnp.maximum(n - 1, 0), kv_col)),
                  pl.BlockSpec((blk, CHUNK), lambda b, n: (b * nb + n, kv_col)),
                  pl.BlockSpec((blk, width), lambda b, n: (b * nb + n, z_col)),
                  pl.BlockSpec(bias.shape, lambda b, n: (0, 0, 0)),
                  pl.BlockSpec(sink_cols.shape, lambda b, n: (0, 0, 0))],
        out_specs=pl.BlockSpec((blk, width), lambda b, n: (b * nb + n, 0)),
        out_shape=jax.ShapeDtypeStruct((t, width), BF16),
        compiler_params=_params(("parallel", "arbitrary"), need + (24 << 20)),
        name="attn_swa",
    )(proj, proj, proj, proj, bias, sink_cols)


def _merge_kernel(oa_ref, ob_ref, oc_ref, wa_ref, wb_ref, wc_ref, ga_ref, gb_ref, gc_ref, y_ref):
    branches = ((oa_ref, wa_ref, ga_ref), (ob_ref, wb_ref, gb_ref), (oc_ref, wc_ref, gc_ref))
    for r in range(y_ref.shape[0] // ROW_CHUNK):
        rows = slice(r * ROW_CHUNK, (r + 1) * ROW_CHUNK)
        y = None
        for o_ref, w_ref, g_ref in branches:
            term = g_ref[rows, :].astype(F32) * jnp.dot(o_ref[rows, :], w_ref[...],
                                                        preferred_element_type=F32)
            y = term if y is None else y + term
        y_ref[rows, :] = y.astype(y_ref.dtype)


def _merge_proj(oa, ob, oc, wa, wb, wc, layer, proj):
    t, k = oa.shape
    tm, tn = 1024, CHUNK
    gcols = D // tn
    o_spec = pl.BlockSpec((tm, k), lambda i, j: (i, 0))
    w_spec = pl.BlockSpec((None, k, tn), lambda i, j: (layer, 0, j))

    def gate_spec(branch):
        return pl.BlockSpec((tm, tn), lambda i, j: (i, CH_GATE + branch * gcols + j))

    need = 2 * (3 * tm * k * 2 + 3 * k * tn * 2 + 4 * tm * tn * 2) + 4 * tm * tn * 4
    return pl.pallas_call(
        _merge_kernel,
        grid=(t // tm, D // tn),
        in_specs=[o_spec, o_spec, o_spec, w_spec, w_spec, w_spec,
                  gate_spec(0), gate_spec(1), gate_spec(2)],
        out_specs=pl.BlockSpec((tm, tn), lambda i, j: (i, j)),
        out_shape=jax.ShapeDtypeStruct((t, D), BF16),
        compiler_params=_params(("parallel", "arbitrary"), need + (8 << 20)),
        name="merge_proj",
    )(oa, ob, oc, wa, wb, wc, proj, proj, proj)


def _out_kernel(y_ref, w_ref, x_ref, g_ref, o_ref, acc_ref, ssq_ref):
    i = pl.program_id(0)
    j = pl.program_id(1)
    slot = i % 2
    prev = 1 - slot

    @pl.when((i == 0) & (j == 0))
    def _():
        acc_ref[...] = jnp.zeros_like(acc_ref)
        ssq_ref[...] = jnp.zeros_like(ssq_ref)

    rs = lax.rsqrt(ssq_ref[prev] * (1.0 / D) + NORM_EPS)
    o_ref[...] = x_ref[...] + acc_ref[j] * rs * g_ref[...]

    ss = jnp.where(j == 0, 0.0, ssq_ref[slot])
    for r in range(y_ref.shape[0] // ROW_CHUNK):
        rows = slice(r * ROW_CHUNK, (r + 1) * ROW_CHUNK)
        acc = jnp.dot(y_ref[rows, :], w_ref[...], preferred_element_type=F32)
        acc_ref[j, rows, :] = acc
        ssq_ref[slot, rows, :] = ss[rows, :] + jnp.sum(acc * acc, axis=1, keepdims=True)


def _out_proj(y, w_o, layer, x2, g):
    t = y.shape[0]
    tm, tn = 1024, CHUNK
    ni, nj = t // tm, D // tn
    need = 2 * (tm * D * 2 + D * tn * 2 + 2 * tm * tn * 4) + tm * D * 4 + 4 * tm * tn * 4
    return pl.pallas_call(
        _out_kernel,
        grid=(ni + 1, nj),
        in_specs=[pl.BlockSpec((tm, D), lambda i, j: (jnp.minimum(i, ni - 1), 0)),
                  pl.BlockSpec((None, D, tn), lambda i, j: (layer, 0, j)),
                  pl.BlockSpec((tm, tn), lambda i, j: (jnp.maximum(i - 1, 0), j)),
                  pl.BlockSpec((1, tn), lambda i, j: (0, j))],
        out_specs=pl.BlockSpec((tm, tn), lambda i, j: (jnp.maximum(i - 1, 0), j)),
        out_shape=jax.ShapeDtypeStruct((t, D), F32),
        scratch_shapes=[pltpu.VMEM((nj, tm, tn), F32), pltpu.VMEM((2, tm, 1), F32)],
        compiler_params=_params(("arbitrary", "arbitrary"), need + (8 << 20)),
        name="out_proj",
    )(y, w_o, x2, g.reshape(1, D))


IN_SIZES = (A_QR, A_KVR, A_ROPE, HA * A_NOPE, HB * DB, KVB * DB, KVB * DB, HB * DB,
            HC * DC, HC * DC, HC * DC, HC, HC * DC)
IN_NAMES = ("a_cq", "a_ckv", "a_kr", "a_z", "b_q", "b_k", "b_v", "b_z", "c_q", "c_k", "c_v", "c_f", "c_z")
IN_DIM = sum(IN_SIZES)
IN_START = {name: sum(IN_SIZES[:i]) for i, name in enumerate(IN_NAMES)}
IN_WIDTH = dict(zip(IN_NAMES, IN_SIZES))
MAIN_ORDER = ("a_cq", "a_ckv", "b_q", "c_q", "c_k", "c_v", "b_z", "a_z", "c_z", "b_k", "b_v")
MAIN_WIDTH = sum(IN_WIDTH[n] for n in MAIN_ORDER)


def _cast_kernel(x_ref, o_ref):
    o_ref[...] = x_ref[...].astype(o_ref.dtype)


def _to_bf16(w):
    nl, r, c = w.shape
    tr, tc = 512, 2048
    spec = pl.BlockSpec((None, tr, tc), lambda l, i, j: (l, i, j))
    return pl.pallas_call(
        _cast_kernel,
        grid=(nl, r // tr, c // tc),
        in_specs=[spec],
        out_specs=spec,
        out_shape=jax.ShapeDtypeStruct(w.shape, BF16),
        compiler_params=_params(("arbitrary",) * 3, 6 * tr * tc * 4),
        name="cast_bf16",
    )(w)


def _relayout_kernel(w_ref, main_ref, misc_ref):
    def columns(name):
        start, width = IN_START[name], IN_WIDTH[name]
        a0 = start // LANES * LANES
        a1 = min(-(-(start + width) // LANES) * LANES, IN_DIM)
        return w_ref[:, a0:a1][:, start - a0:start - a0 + width]

    dst = 0
    for name in MAIN_ORDER:
        width = IN_WIDTH[name]
        main_ref[:, dst:dst + width] = columns(name).astype(main_ref.dtype)
        dst += width
    rows = w_ref.shape[0]
    misc = jnp.concatenate([columns("a_kr"), columns("c_f"),
                            jnp.zeros((rows, LANES - F_LANE - HC), F32)], axis=1)
    misc_ref[...] = misc.astype(misc_ref.dtype)


def _relayout_w_in(w_in):
    nl = w_in.shape[0]
    tr = 128
    return pl.pallas_call(
        _relayout_kernel,
        grid=(nl, D // tr),
        in_specs=[pl.BlockSpec((None, tr, IN_DIM), lambda l, i: (l, i, 0))],
        out_specs=[pl.BlockSpec((None, tr, MAIN_WIDTH), lambda l, i: (l, i, 0)),
                   pl.BlockSpec((None, tr, LANES), lambda l, i: (l, i, 0))],
        out_shape=[jax.ShapeDtypeStruct((nl, D, MAIN_WIDTH), BF16),
                   jax.ShapeDtypeStruct((nl, D, LANES), BF16)],
        compiler_params=_params(("arbitrary", "arbitrary"), 6 * tr * IN_DIM * 4),
        name="relayout_w_in",
    )(w_in)


def _layer_weights(w_uq, w_uk, w_uv, b_f):
    col_scale = jnp.ones((NP,), F32)
    col_scale = col_scale.at[CH_BQ * CHUNK:CH_CQ * CHUNK].set(DB ** -0.5 * LOG2E)
    col_scale = col_scale.at[CH_CQ * CHUNK:CH_CK * CHUNK].set(DC ** -0.5 * LOG2E)
    col_scale = col_scale.reshape(1, NP)
    bf_row = jnp.concatenate([jnp.zeros((F_LANE,), F32), b_f.astype(F32),
                              jnp.zeros((LANES - F_LANE - HC,), F32)]).reshape(1, LANES)
    wq = jnp.pad(w_uq, ((0, 0), (0, 0), (0, A_QPAD - A_NOPE - A_ROPE)))
    wq = wq.reshape(A_QR, HA * A_QPAD).astype(BF16)
    wkv = jnp.concatenate([w_uk.reshape(A_KVR, HA * A_NOPE), w_uv.reshape(A_KVR, HA * A_NOPE)],
                          axis=1).astype(BF16)
    return col_scale, bf_row, wq, wkv


def _sink_columns(sinks):
    s = (sinks.astype(F32) * LOG2E).reshape(KVB, GB // 2, 1, 2)
    s = jnp.broadcast_to(s, (KVB, GB // 2, SEQ_BLOCK, 2)).reshape(KVB, (GB // 2) * SEQ_BLOCK, 2)
    return jnp.pad(s, ((0, 0), (0, 0), (0, LANES - 2)))


def kernel(x, positions, rel_table, pre_norm, w_in, q_a_norm, kv_a_norm, w_uq, w_uk, w_uv,
           sinks, b_f, w_proj_a, w_proj_b, w_proj_c, w_merge, w_o, post_norm):
    batch, seq, _ = x.shape
    depth = w_in.shape[0]
    x2 = x.reshape(batch * seq, D)
    tables = _rope_tables(positions)
    bias = _rel_bias(rel_table)
    wm16, wo16 = _to_bf16(w_merge), w_o.astype(BF16)
    wa16, wb16, wc16 = (w.astype(BF16) for w in (w_proj_a, w_proj_b, w_proj_c))
    w_inr, w_misc = _relayout_w_in(w_in)
    for l in range(depth):
        col_scale, bf_row, wq, wkv = _layer_weights(w_uq[l], w_uk[l], w_uv[l], b_f[l])
        h = _prenorm(x2, pre_norm[l])
        proj = _fused_proj(h, w_inr, wm16, l, col_scale)
        k_rope, ccol, crow = _misc_proj(h, w_misc, l, bf_row, tables[0], tables[1], batch, seq)
        q_a = _latent_proj(proj, 0, A_QR, q_a_norm[l], wq, tables,
                           (A_NOPE + A_ROPE) ** -0.5 * LOG2E)
        kv_a = _latent_proj(proj, CH_ACKV, A_KVR, kv_a_norm[l], wkv, None, 1.0)
        o_a = _attention_a(q_a, kv_a, k_rope, proj, batch, seq)
        o_b = _attention_b(proj, bias, _sink_columns(sinks[l]), batch, seq)
        o_c = _attention_c(proj, ccol, crow, batch, seq)
        y = _merge_proj(o_a, o_b, o_c, wa16, wb16, wc16, l, proj)
        x2 = _out_proj(y, wo16, l, x2, post_norm[l])
    return x2.reshape(batch, seq, D)
```

```python
import functools
import math

import jax
import jax.numpy as jnp
from jax import lax
from jax.experimental import pallas as pl
from jax.experimental.pallas import tpu as pltpu

F32 = jnp.float32
BF16 = jnp.bfloat16

D = 4096
SEQ_BLOCK = 128
NORM_EPS = 1e-6
MASKED = -1e30
LOG2E = math.log2(math.e)
HA = 16
A_QR = 1536
A_KVR = 512
A_NOPE = 128
A_ROPE = 64
A_HALF = A_ROPE // 2
A_QPAD = 256
THETA = 10000.0
HB = 32
KVB = 4
DB = 64
GB = HB // KVB
HC = 16
DC = 128
N_BUCKETS = 32
MAX_DIST = 128

LANES = 128
CHUNK = 512

CH_ACQ, CH_ACKV, CH_BQ, CH_CQ, CH_CK, CH_CV = 0, 3, 4, 8, 12, 16
CH_BZ, CH_AZ, CH_CZ = 20, 24, 28
CH_GATE = 32
CH_BKV = 56
N_CHUNKS = 57
NP = N_CHUNKS * CHUNK
BF16_SUBLANES = 16

IN_SIZES = (A_QR, A_KVR, A_ROPE, HA * A_NOPE, HB * DB, KVB * DB, KVB * DB, HB * DB,
            HC * DC, HC * DC, HC * DC, HC, HC * DC)
IN_NAMES = ("a_cq", "a_ckv", "a_kr", "a_z", "b_q", "b_k", "b_v", "b_z", "c_q", "c_k", "c_v", "c_f", "c_z")
IN_START = {name: sum(IN_SIZES[:i]) for i, name in enumerate(IN_NAMES)}
IN_WIDTH = dict(zip(IN_NAMES, IN_SIZES))
F_LANE = 64

VMEM_CAP = 60 * 1024 * 1024


def _vmem(nbytes):
    return int(min(VMEM_CAP, nbytes))


def _params(sem, nbytes):
    return pltpu.CompilerParams(dimension_semantics=sem, vmem_limit_bytes=_vmem(nbytes))


def _rope_table_kernel(pos_ref, freq_ref, cos_ref, sin_ref):
    ang = pos_ref[...].astype(F32) * freq_ref[...]
    lane = lax.broadcasted_iota(jnp.int32, ang.shape, 1)
    live = lane < A_ROPE
    s = jnp.sin(ang)
    cos_ref[...] = jnp.where(live, jnp.cos(ang), 0.0)
    sin_ref[...] = jnp.where(live, jnp.where(lane < A_HALF, -s, s), 0.0)


def _rope_tables(positions):
    t = positions.size
    tm = 2048
    half = jnp.arange(A_HALF, dtype=F32)
    inv_freq = THETA ** (-half / A_HALF)
    freq = jnp.concatenate([inv_freq, inv_freq, jnp.zeros((LANES - A_ROPE,), F32)]).reshape(1, LANES)
    pos = positions.reshape(t, 1)
    return pl.pallas_call(
        _rope_table_kernel,
        grid=(t // tm,),
        in_specs=[pl.BlockSpec((tm, 1), lambda i: (i, 0)),
                  pl.BlockSpec((1, LANES), lambda i: (0, 0))],
        out_specs=[pl.BlockSpec((tm, LANES), lambda i: (i, 0)),
                   pl.BlockSpec((tm, LANES), lambda i: (i, 0))],
        out_shape=[jax.ShapeDtypeStruct((t, LANES), F32)] * 2,
        compiler_params=_params(("arbitrary",), 32 << 20),
        name="rope_tables",
    )(pos, freq)


def _rope(t, cos, sin):
    lane = lax.broadcasted_iota(jnp.int32, t.shape, 1)
    partner = jnp.where(lane < A_HALF,
                        pltpu.roll(t, LANES - A_HALF, 1),
                        pltpu.roll(t, A_HALF, 1))
    return t * cos + partner * sin


def _relbias_kernel(table_ref, o_ref):
    h = pl.program_id(0)
    qi = lax.broadcasted_iota(jnp.int32, (SEQ_BLOCK, 2 * SEQ_BLOCK), 0)
    si = lax.broadcasted_iota(jnp.int32, (SEQ_BLOCK, 2 * SEQ_BLOCK), 1)
    dist = qi + SEQ_BLOCK - si
    max_exact = N_BUCKETS // 2
    d = jnp.maximum(dist, 0)
    large = max_exact + (jnp.log(jnp.maximum(d, 1).astype(F32) / max_exact)
                         / math.log(MAX_DIST / max_exact)
                         * (N_BUCKETS - max_exact)).astype(jnp.int32)
    large = jnp.minimum(large, N_BUCKETS - 1)
    bucket = jnp.where(d < max_exact, d, large)
    acc = jnp.zeros(bucket.shape, F32)
    for b in range(N_BUCKETS):
        acc = jnp.where(bucket == b, table_ref[b, h], acc)
    o_ref[0] = acc * LOG2E


def _rel_bias(rel_table):
    def out_map(h):
        return (h // GB, (h % GB) // 2, h % 2)
    return pl.pallas_call(
        _relbias_kernel,
        grid=(HB,),
        in_specs=[pl.BlockSpec(memory_space=pltpu.SMEM)],
        out_specs=pl.BlockSpec((1, SEQ_BLOCK, 2 * SEQ_BLOCK), out_map),
        out_shape=jax.ShapeDtypeStruct((KVB, (GB // 2) * SEQ_BLOCK, 4 * SEQ_BLOCK), F32),
        compiler_params=_params(("arbitrary",), 16 << 20),
        name="rel_bias",
    )(rel_table)


def _prenorm_kernel(x_ref, g_ref, o_ref):
    x = x_ref[...]
    r = lax.rsqrt(jnp.mean(x * x, axis=-1, keepdims=True) + NORM_EPS)
    o_ref[...] = (x * r * g_ref[...]).astype(o_ref.dtype)


def _prenorm(x2, g):
    t = x2.shape[0]
    tm = 256
    return pl.pallas_call(
        _prenorm_kernel,
        grid=(t // tm,),
        in_specs=[pl.BlockSpec((tm, D), lambda i: (i, 0)),
                  pl.BlockSpec((1, D), lambda i: (0, 0))],
        out_specs=pl.BlockSpec((tm, D), lambda i: (i, 0)),
        out_shape=jax.ShapeDtypeStruct((t, D), BF16),
        compiler_params=_params(("arbitrary",), 32 << 20),
        name="prenorm",
    )(x2, g.reshape(1, D))


ROW_CHUNK = 256


def _chunked_dot(x_ref, w_ref, o_ref, epilogue, trans_w=False):
    dims = (((1,), (1,) if trans_w else (0,)), ((), ()))
    for r in range(x_ref.shape[0] // ROW_CHUNK):
        rows = slice(r * ROW_CHUNK, (r + 1) * ROW_CHUNK)
        acc = lax.dot_general(x_ref[rows, :], w_ref[...], dims, preferred_element_type=F32)
        o_ref[rows, :] = epilogue(acc, rows).astype(o_ref.dtype)


def _proj_kernel(rows_ref, h_ref, wi_ref, wm_ref, cs_ref, o_ref, *,
                 silu_start, gate_start, gate_end):
    del rows_ref
    j = pl.program_id(1)
    wi = wi_ref.at[0]

    @pl.when((j < silu_start) | (j >= gate_end))
    def _():
        _chunked_dot(h_ref, wi, o_ref, lambda a, rows: a * cs_ref[...], trans_w=True)

    @pl.when((j >= silu_start) & (j < gate_start))
    def _():
        _chunked_dot(h_ref, wi, o_ref, lambda a, rows: a * jax.nn.sigmoid(a), trans_w=True)

    @pl.when((j >= gate_start) & (j < gate_end))
    def _():
        _chunked_dot(h_ref, wm_ref, o_ref, lambda a, rows: jax.nn.sigmoid(a))


def _source_rows():
    starts = {CH_ACQ: "a_cq", CH_ACKV: "a_ckv", CH_BQ: "b_q", CH_CQ: "c_q", CH_CK: "c_k",
              CH_CV: "c_v", CH_BZ: "b_z", CH_AZ: "a_z", CH_CZ: "c_z", CH_BKV: "b_k"}
    rows, row = [], 0
    for j in range(N_CHUNKS):
        if CH_GATE <= j < CH_BKV:
            rows.append(rows[-1])
            continue
        if j in starts:
            row = IN_START[starts[j]]
        rows.append(row)
        row += CHUNK
    return jnp.asarray(rows, jnp.int32)


def _fused_proj(h, w_t, w_merge, layer, col_scale):
    t = h.shape[0]
    tm, tn = 2048, CHUNK
    n_gate = CH_BKV - CH_GATE
    kern = functools.partial(_proj_kernel, silu_start=CH_BZ, gate_start=CH_GATE, gate_end=CH_BKV)

    def wi_map(i, j, rows):
        return (layer, pl.multiple_of(rows[j], BF16_SUBLANES), 0)

    def wm_map(i, j, rows):
        return (layer, 0, jnp.clip(j - CH_GATE, 0, n_gate - 1))

    need = 2 * (tm * D * 2 + 2 * D * tn * 2 + tm * tn * 2) + 4 * tm * tn * 4
    grid_spec = pltpu.PrefetchScalarGridSpec(
        num_scalar_prefetch=1,
        grid=(t // tm, NP // tn),
        in_specs=[pl.BlockSpec((tm, D), lambda i, j, rows: (i, 0)),
                  pl.BlockSpec((pl.Element(1), pl.Element(tn), pl.Element(D)), wi_map),
                  pl.BlockSpec((None, D, tn), wm_map),
                  pl.BlockSpec((1, tn), lambda i, j, rows: (0, j))],
        out_specs=pl.BlockSpec((tm, tn), lambda i, j, rows: (i, j)),
    )
    return pl.pallas_call(
        kern,
        grid_spec=grid_spec,
        out_shape=jax.ShapeDtypeStruct((t, NP), BF16),
        compiler_params=_params(("parallel", "arbitrary"), need + (8 << 20)),
        name="fused_proj",
    )(_source_rows(), h, w_t, w_merge, col_scale)


def _misc_kernel(h_ref, w_ref, bf_ref, cos_ref, sin_ref, kr_ref, ccol_ref, crow_ref, carry_ref):
    t = pl.program_id(1)
    acc = lax.dot_general(h_ref[...], w_ref[...], (((1,), (1,)), ((), ())),
                          preferred_element_type=F32)
    kr_ref[...] = _rope(acc, cos_ref[...], sin_ref[...]).astype(kr_ref.dtype)

    xx = acc + bf_ref[...]
    log_f = jnp.minimum(xx, 0.0) - jnp.log1p(jnp.exp(-jnp.abs(xx)))
    hi = log_f.astype(BF16)
    r1 = log_f - hi.astype(F32)
    mid = r1.astype(BF16)
    lo = (r1 - mid.astype(F32)).astype(BF16)
    tm = acc.shape[0]
    ri = lax.broadcasted_iota(jnp.int32, (tm, tm), 0)
    ci = lax.broadcasted_iota(jnp.int32, (tm, tm), 1)
    tri = jnp.where(ri >= ci, 1.0, 0.0).astype(BF16)
    cs = (jnp.dot(tri, hi, preferred_element_type=F32)
          + jnp.dot(tri, mid, preferred_element_type=F32)
          + jnp.dot(tri, lo, preferred_element_type=F32))

    @pl.when(t == 0)
    def _():
        carry_ref[...] = jnp.zeros_like(carry_ref)

    cs = cs + carry_ref[...]
    carry_ref[...] = cs[tm - 1:tm, :]
    cs2 = cs * LOG2E
    ccol_ref[...] = cs2
    crow_ref[0] = cs2.T


def _misc_proj(h, w_misc, layer, bf_row, cos_t, sin_t, batch, seq):
    t = h.shape[0]
    tm = 512
    nt = seq // tm
    need = 2 * (tm * D * 2 + D * LANES * 2 + 5 * tm * LANES * 4) + 4 * tm * tm * 4
    return pl.pallas_call(
        _misc_kernel,
        grid=(batch, nt),
        in_specs=[pl.BlockSpec((tm, D), lambda b, i: (b * nt + i, 0)),
                  pl.BlockSpec((None, LANES, D), lambda b, i: (layer, 0, 0)),
                  pl.BlockSpec((1, LANES), lambda b, i: (0, 0)),
                  pl.BlockSpec((tm, LANES), lambda b, i: (b * nt + i, 0)),
                  pl.BlockSpec((tm, LANES), lambda b, i: (b * nt + i, 0))],
        out_specs=[pl.BlockSpec((tm, LANES), lambda b, i: (b * nt + i, 0)),
                   pl.BlockSpec((tm, LANES), lambda b, i: (b * nt + i, 0)),
                   pl.BlockSpec((1, LANES, tm), lambda b, i: (b, 0, i))],
        out_shape=[jax.ShapeDtypeStruct((t, LANES), BF16),
                   jax.ShapeDtypeStruct((t, LANES), F32),
                   jax.ShapeDtypeStruct((batch, LANES, seq), F32)],
        scratch_shapes=[pltpu.VMEM((1, LANES), F32)],
        compiler_params=_params(("arbitrary", "arbitrary"), need + (8 << 20)),
        name="misc_proj",
    )(h, w_misc, bf_row, cos_t, sin_t)


def _latent_kernel(*refs, rope, out_scale):
    if rope:
        x_ref, g_ref, w_ref, cos_ref, sin_ref, o_ref, xn_ref = refs
    else:
        x_ref, g_ref, w_ref, o_ref, xn_ref = refs

    @pl.when(pl.program_id(1) == 0)
    def _():
        x = x_ref[...].astype(F32)
        r = lax.rsqrt(jnp.mean(x * x, axis=-1, keepdims=True) + NORM_EPS)
        xn_ref[...] = (x * r * g_ref[...]).astype(xn_ref.dtype)

    def rotary(acc, rows):
        cos = cos_ref[rows, :]
        sin = sin_ref[rows, :]
        pieces = []
        for hh in range(acc.shape[1] // A_QPAD):
            base = hh * A_QPAD
            pieces.append(acc[:, base:base + A_NOPE])
            pieces.append(_rope(acc[:, base + A_NOPE:base + A_QPAD], cos, sin))
        return jnp.concatenate(pieces, axis=1) * out_scale

    _chunked_dot(xn_ref, w_ref, o_ref, rotary if rope else (lambda acc, rows: acc))


def _latent_proj(proj, col_block, k, g, w, tables, out_scale):
    t = proj.shape[0]
    n = w.shape[1]
    tm, tn = 1024, 1024
    rope = tables is not None
    kern = functools.partial(_latent_kernel, rope=rope, out_scale=out_scale)
    in_specs = [pl.BlockSpec((tm, k), lambda i, j: (i, col_block)),
                pl.BlockSpec((1, k), lambda i, j: (0, 0)),
                pl.BlockSpec((k, tn), lambda i, j: (0, j))]
    args = [proj, g.reshape(1, k), w]
    if rope:
        in_specs += [pl.BlockSpec((tm, LANES), lambda i, j: (i, 0))] * 2
        args += list(tables)
    need = 2 * (tm * k * 2 + k * tn * 2 + tm * tn * 2 + 2 * tm * LANES * 4) + tm * k * 2 + 4 * tm * tn * 4
    return pl.pallas_call(
        kern,
        grid=(t // tm, n // tn),
        in_specs=in_specs,
        out_specs=pl.BlockSpec((tm, tn), lambda i, j: (i, j)),
        out_shape=jax.ShapeDtypeStruct((t, n), BF16),
        scratch_shapes=[pltpu.VMEM((tm, k), BF16)],
        compiler_params=_params(("parallel", "arbitrary"), need + (8 << 20)),
        name="latent_q" if rope else "latent_kv",
    )(*args)


def _causal_attn_kernel(*refs, tq, fox):
    if fox:
        q_ref, k_ref, v_ref, z_ref, ccol_ref, crow_ref, o_ref, vaug_ref = refs
        keys = k_ref
    else:
        q_ref, kn_ref, kr_ref, v_ref, z_ref, o_ref, kcat_ref, vaug_ref = refs
        kcat_ref[:, 0:LANES] = kn_ref[...]
        kcat_ref[:, LANES:2 * LANES] = kr_ref[...]
        keys = kcat_ref
    seq = q_ref.shape[0]
    lane = lax.broadcasted_iota(jnp.int32, (seq, LANES), 1)
    vaug_ref[:, 0:LANES] = v_ref[...]
    vaug_ref[:, LANES:2 * LANES] = jnp.where(lane == 0, 1.0, 0.0).astype(BF16)
    if fox:
        h = pl.program_id(1)
        cq_all = jnp.sum(jnp.where(lane == F_LANE + h, ccol_ref[...], 0.0), axis=1, keepdims=True)
        ck_all = crow_ref[0, pl.ds(F_LANE + h, 1), :]
    row = lax.broadcasted_iota(jnp.int32, (tq, tq), 0)
    col = lax.broadcasted_iota(jnp.int32, (tq, tq), 1)
    causal = row >= col
    dims = (((1,), (1,)), ((), ()))

    def logits(c):
        lo, hi = c * tq, (c + 1) * tq
        q = q_ref[lo:hi, :]
        t_d = lax.dot_general(q, keys[lo:hi, :], dims, preferred_element_type=F32)
        if fox:
            t_d = t_d - ck_all[:, lo:hi]
        t_d = jnp.where(causal, t_d, MASKED)
        t_o = None
        if c:
            t_o = lax.dot_general(q, keys[0:lo, :], dims, preferred_element_type=F32)
            if fox:
                t_o = t_o - ck_all[:, 0:lo]
        return t_d, t_o

    nq = seq // tq
    order = list(range(nq - 1, -1, -1))
    queue = [logits(c) for c in order[:LOOKAHEAD]]
    for i, c in enumerate(order):
        lo, hi = c * tq, (c + 1) * tq
        t_d, t_o = queue.pop(0)
        if i + LOOKAHEAD < nq:
            queue.append(logits(order[i + LOOKAHEAD]))
        mt = jnp.max(t_d, axis=1, keepdims=True)
        if c:
            mt = jnp.maximum(mt, jnp.max(t_o, axis=1, keepdims=True))
        if fox:
            cq = cq_all[lo:hi, :]
            shift = (mt + cq) - cq
        else:
            shift = mt
        acc = jnp.dot(jnp.exp2(t_d - shift).astype(BF16), vaug_ref[lo:hi, :],
                      preferred_element_type=F32)
        if c:
            acc = acc + jnp.dot(jnp.exp2(t_o - shift).astype(BF16), vaug_ref[0:lo, :],
                                preferred_element_type=F32)
        inv_l = 1.0 / acc[:, LANES:LANES + 1]
        o_ref[lo:hi, :] = (acc[:, 0:LANES] * inv_l * z_ref[lo:hi, :].astype(F32)).astype(o_ref.dtype)


ATTN_TQ = 256
LOOKAHEAD = 2


def _attention_a(q_a, kv_a, k_rope, proj, batch, seq):
    t = q_a.shape[0]
    kern = functools.partial(_causal_attn_kernel, tq=ATTN_TQ, fox=False)
    z_col = CH_AZ * CHUNK // LANES
    need = 2 * (seq * A_QPAD * 2 + 5 * seq * LANES * 2) + seq * A_QPAD * 2
    return pl.pallas_call(
        kern,
        grid=(batch, HA),
        in_specs=[pl.BlockSpec((seq, A_QPAD), lambda b, h: (b, h)),
                  pl.BlockSpec((seq, LANES), lambda b, h: (b, h)),
                  pl.BlockSpec((seq, LANES), lambda b, h: (b, 0)),
                  pl.BlockSpec((seq, LANES), lambda b, h: (b, HA + h)),
                  pl.BlockSpec((seq, LANES), lambda b, h: (b, z_col + h))],
        out_specs=pl.BlockSpec((seq, LANES), lambda b, h: (b, h)),
        out_shape=jax.ShapeDtypeStruct((t, HA * LANES), BF16),
        scratch_shapes=[pltpu.VMEM((seq, A_QPAD), BF16), pltpu.VMEM((seq, 2 * LANES), BF16)],
        compiler_params=_params(("parallel", "arbitrary"), need + (24 << 20)),
        name="attn_mla",
    )(q_a, kv_a, k_rope, kv_a, proj)


def _attention_c(proj, ccol, crow, batch, seq):
    t = proj.shape[0]
    kern = functools.partial(_causal_attn_kernel, tq=ATTN_TQ, fox=True)
    q_col, k_col, v_col, z_col = (c * CHUNK // LANES for c in (CH_CQ, CH_CK, CH_CV, CH_CZ))
    need = 2 * (5 * seq * LANES * 2 + 2 * seq * LANES * 4)
    return pl.pallas_call(
        kern,
        grid=(batch, HC),
        in_specs=[pl.BlockSpec((seq, LANES), lambda b, h: (b, q_col + h)),
                  pl.BlockSpec((seq, LANES), lambda b, h: (b, k_col + h)),
                  pl.BlockSpec((seq, LANES), lambda b, h: (b, v_col + h)),
                  pl.BlockSpec((seq, LANES), lambda b, h: (b, z_col + h)),
                  pl.BlockSpec((seq, LANES), lambda b, h: (b, 0)),
                  pl.BlockSpec((1, LANES, seq), lambda b, h: (b, 0, 0))],
        out_specs=pl.BlockSpec((seq, LANES), lambda b, h: (b, h)),
        out_shape=jax.ShapeDtypeStruct((t, HC * DC), BF16),
        scratch_shapes=[pltpu.VMEM((seq, 2 * LANES), BF16)],
        compiler_params=_params(("parallel", "arbitrary"), need + (24 << 20)),
        name="attn_fox",
    )(proj, proj, proj, proj, ccol, crow)


def _swa_kernel(q_ref, kvp_ref, kvc_ref, z_ref, bias_ref, sink_ref, o_ref):
    n = pl.program_id(1)
    blk = SEQ_BLOCK
    band = 2 * blk
    rows = (GB // 2) * blk
    kv = jnp.concatenate([kvp_ref[...], kvc_ref[...]], axis=0).astype(F32)
    lane = lax.broadcasted_iota(jnp.int32, (band, LANES), 1)
    low = lane < DB

    def split(two_heads):
        lo = jnp.where(low, two_heads, 0.0)
        hi = jnp.where(low, 0.0, two_heads)
        return (lo, pltpu.roll(lo, DB, 1)), (pltpu.roll(hi, DB, 1), hi)

    k_heads = split(kv[:, 0:LANES]) + split(kv[:, LANES:2 * LANES])
    v_heads = split(kv[:, 2 * LANES:3 * LANES]) + split(kv[:, 3 * LANES:4 * LANES])

    r = lax.broadcasted_iota(jnp.int32, (rows, band), 0) % blk
    c = lax.broadcasted_iota(jnp.int32, (rows, band), 1)
    dist = r + blk - c
    first_key = jnp.where(n > 0, 0, blk)
    valid = (dist >= 0) & (dist < blk) & (c >= first_key)

    def logits(kh):
        k_cat = jnp.concatenate(k_heads[kh], axis=0).astype(BF16)
        q4 = jnp.concatenate(
            [q_ref[:, (kh * (GB // 2) + jj) * LANES:(kh * (GB // 2) + jj + 1) * LANES]
             for jj in range(GB // 2)], axis=0)
        s = lax.dot_general(q4, k_cat, (((1,), (1,)), ((), ())), preferred_element_type=F32)
        return s + bias_ref[kh]

    nxt = logits(0)
    for kh in range(KVB):
        s = nxt
        if kh + 1 < KVB:
            nxt = logits(kh + 1)
        v_cat = jnp.concatenate(v_heads[kh], axis=0).astype(BF16)
        halves, inv = [], []
        for e in range(2):
            se = jnp.where(valid, s[:, e * band:(e + 1) * band], MASKED)
            sink = sink_ref[kh, :, e:e + 1]
            m = jnp.maximum(jnp.max(se, axis=1, keepdims=True), sink)
            p = jnp.exp2(se - m)
            denom = jnp.sum(p, axis=1, keepdims=True) + jnp.exp2(sink - m)
            halves.append(p.astype(BF16))
            inv.append(1.0 / denom)
        p_cat = jnp.concatenate(halves, axis=1)
        o = jnp.dot(p_cat, v_cat, preferred_element_type=F32)
        o = o * jnp.where(lax.broadcasted_iota(jnp.int32, o.shape, 1) < DB, inv[0], inv[1])
        for jj in range(GB // 2):
            c0 = (kh * (GB // 2) + jj) * LANES
            o_ref[:, c0:c0 + LANES] = (o[jj * blk:(jj + 1) * blk, :]
                                       * z_ref[:, c0:c0 + LANES].astype(F32)).astype(o_ref.dtype)


def _attention_b(proj, bias, sink_cols, batch, seq):
    t = proj.shape[0]
    blk = SEQ_BLOCK
    nb = seq // blk
    width = HB * DB
    q_col = CH_BQ * CHUNK // width
    kv_col = CH_BKV
    z_col = CH_BZ * CHUNK // width
    assert CH_BQ * CHUNK % width == 0 and CH_BZ * CHUNK % width == 0
    need = 2 * (3 * blk * width * 2 + 2 * blk * CHUNK * 2 + bias.size * 4 + sink_cols.size * 4)
    return pl.pallas_call(
        _swa_kernel,
        grid=(batch, nb),
        in_specs=[pl.BlockSpec((blk, width), lambda b, n: (b * nb + n, q_col)),
                  pl.BlockSpec((blk, CHUNK), lambda b, n: (b * nb + jnp.maximum(n - 1, 0), kv_col)),
                  pl.BlockSpec((blk, CHUNK), lambda b, n: (b * nb + n, kv_col)),
                  pl.BlockSpec((blk, width), lambda b, n: (b * nb + n, z_col)),
                  pl.BlockSpec(bias.shape, lambda b, n: (0, 0, 0)),
                  pl.BlockSpec(sink_cols.shape, lambda b, n: (0, 0, 0))],
        out_specs=pl.BlockSpec((blk, width), lambda b, n: (b * nb + n, 0)),
        out_shape=jax.ShapeDtypeStruct((t, width), BF16),
        compiler_params=_params(("parallel", "arbitrary"), need + (24 << 20)),
        name="attn_swa",
    )(proj, proj, proj, proj, bias, sink_cols)


def _merge_kernel(oa_ref, ob_ref, oc_ref, wa_ref, wb_ref, wc_ref, ga_ref, gb_ref, gc_ref, y_ref):
    branches = ((oa_ref, wa_ref, ga_ref), (ob_ref, wb_ref, gb_ref), (oc_ref, wc_ref, gc_ref))
    for r in range(y_ref.shape[0] // ROW_CHUNK):
        rows = slice(r * ROW_CHUNK, (r + 1) * ROW_CHUNK)
        y = None
        for o_ref, w_ref, g_ref in branches:
            term = g_ref[rows, :].astype(F32) * jnp.dot(o_ref[rows, :], w_ref[...],
                                                        preferred_element_type=F32)
            y = term if y is None else y + term
        y_ref[rows, :] = y.astype(y_ref.dtype)


def _merge_proj(oa, ob, oc, wa, wb, wc, layer, proj):
    t, k = oa.shape
    tm, tn = 1024, CHUNK
    gcols = D // tn
    o_spec = pl.BlockSpec((tm, k), lambda i, j: (i, 0))
    w_spec = pl.BlockSpec((None, k, tn), lambda i, j: (layer, 0, j))

    def gate_spec(branch):
        return pl.BlockSpec((tm, tn), lambda i, j: (i, CH_GATE + branch * gcols + j))

    need = 2 * (3 * tm * k * 2 + 3 * k * tn * 2 + 4 * tm * tn * 2) + 4 * tm * tn * 4
    return pl.pallas_call(
        _merge_kernel,
        grid=(t // tm, D // tn),
        in_specs=[o_spec, o_spec, o_spec, w_spec, w_spec, w_spec,
                  gate_spec(0), gate_spec(1), gate_spec(2)],
        out_specs=pl.BlockSpec((tm, tn), lambda i, j: (i, j)),
        out_shape=jax.ShapeDtypeStruct((t, D), BF16),
        compiler_params=_params(("parallel", "arbitrary"), need + (8 << 20)),
        name="merge_proj",
    )(oa, ob, oc, wa, wb, wc, proj, proj, proj)


def _out_kernel(y_ref, w_ref, x_ref, g_ref, o_ref, acc_ref, ssq_ref):
    i = pl.program_id(0)
    j = pl.program_id(1)
    slot = i % 2
    prev = 1 - slot

    @pl.when((i == 0) & (j == 0))
    def _():
        acc_ref[...] = jnp.zeros_like(acc_ref)
        ssq_ref[...] = jnp.zeros_like(ssq_ref)

    rs = lax.rsqrt(ssq_ref[prev] * (1.0 / D) + NORM_EPS)
    o_ref[...] = x_ref[...] + acc_ref[j] * rs * g_ref[...]

    ss = jnp.where(j == 0, 0.0, ssq_ref[slot])
    for r in range(y_ref.shape[0] // ROW_CHUNK):
        rows = slice(r * ROW_CHUNK, (r + 1) * ROW_CHUNK)
        acc = jnp.dot(y_ref[rows, :], w_ref[...], preferred_element_type=F32)
        acc_ref[j, rows, :] = acc
        ssq_ref[slot, rows, :] = ss[rows, :] + jnp.sum(acc * acc, axis=1, keepdims=True)


def _out_proj(y, w_o, layer, x2, g):
    t = y.shape[0]
    tm, tn = 1024, CHUNK
    ni, nj = t // tm, D // tn
    need = 2 * (tm * D * 2 + D * tn * 2 + 2 * tm * tn * 4) + tm * D * 4 + 4 * tm * tn * 4
    return pl.pallas_call(
        _out_kernel,
        grid=(ni + 1, nj),
        in_specs=[pl.BlockSpec((tm, D), lambda i, j: (jnp.minimum(i, ni - 1), 0)),
                  pl.BlockSpec((None, D, tn), lambda i, j: (layer, 0, j)),
                  pl.BlockSpec((tm, tn), lambda i, j: (jnp.maximum(i - 1, 0), j)),
                  pl.BlockSpec((1, tn), lambda i, j: (0, j))],
        out_specs=pl.BlockSpec((tm, tn), lambda i, j: (jnp.maximum(i - 1, 0), j)),
        out_shape=jax.ShapeDtypeStruct((t, D), F32),
        scratch_shapes=[pltpu.VMEM((nj, tm, tn), F32), pltpu.VMEM((2, tm, 1), F32)],
        compiler_params=_params(("arbitrary", "arbitrary"), need + (8 << 20)),
        name="out_proj",
    )(y, w_o, x2, g.reshape(1, D))


def _cast_kernel(x_ref, o_ref):
    o_ref[...] = x_ref[...].astype(o_ref.dtype)


def _to_bf16(w):
    nl, r, c = w.shape
    tr, tc = 512, 2048
    spec = pl.BlockSpec((None, tr, tc), lambda l, i, j: (l, i, j))
    return pl.pallas_call(
        _cast_kernel,
        grid=(nl, r // tr, c // tc),
        in_specs=[spec],
        out_specs=spec,
        out_shape=jax.ShapeDtypeStruct(w.shape, BF16),
        compiler_params=_params(("arbitrary",) * 3, 6 * tr * tc * 4),
        name="cast_bf16",
    )(w)


def _transposed_w_in(w_in):
    w_t = jnp.swapaxes(w_in, 1, 2).astype(BF16)

    def rows(name):
        return w_t[:, IN_START[name]:IN_START[name] + IN_WIDTH[name], :]

    pad = jnp.zeros((w_in.shape[0], LANES - F_LANE - HC, D), BF16)
    misc = jnp.concatenate([rows("a_kr"), rows("c_f"), pad], axis=1)
    return w_t, misc


def _layer_weights(w_uq, w_uk, w_uv, b_f):
    col_scale = jnp.ones((NP,), F32)
    col_scale = col_scale.at[CH_BQ * CHUNK:CH_CQ * CHUNK].set(DB ** -0.5 * LOG2E)
    col_scale = col_scale.at[CH_CQ * CHUNK:CH_CK * CHUNK].set(DC ** -0.5 * LOG2E)
    col_scale = col_scale.reshape(1, NP)
    bf_row = jnp.concatenate([jnp.zeros((F_LANE,), F32), b_f.astype(F32),
                              jnp.zeros((LANES - F_LANE - HC,), F32)]).reshape(1, LANES)
    wq = jnp.pad(w_uq, ((0, 0), (0, 0), (0, A_QPAD - A_NOPE - A_ROPE)))
    wq = wq.reshape(A_QR, HA * A_QPAD).astype(BF16)
    wkv = jnp.concatenate([w_uk.reshape(A_KVR, HA * A_NOPE), w_uv.reshape(A_KVR, HA * A_NOPE)],
                          axis=1).astype(BF16)
    return col_scale, bf_row, wq, wkv


def _sink_columns(sinks):
    s = (sinks.astype(F32) * LOG2E).reshape(KVB, GB // 2, 1, 2)
    s = jnp.broadcast_to(s, (KVB, GB // 2, SEQ_BLOCK, 2)).reshape(KVB, (GB // 2) * SEQ_BLOCK, 2)
    return jnp.pad(s, ((0, 0), (0, 0), (0, LANES - 2)))


def kernel(x, positions, rel_table, pre_norm, w_in, q_a_norm, kv_a_norm, w_uq, w_uk, w_uv,
           sinks, b_f, w_proj_a, w_proj_b, w_proj_c, w_merge, w_o, post_norm):
    batch, seq, _ = x.shape
    depth = w_in.shape[0]
    x2 = x.reshape(batch * seq, D)
    tables = _rope_tables(positions)
    bias = _rel_bias(rel_table)
    wm16, wo16 = _to_bf16(w_merge), w_o.astype(BF16)
    wa16, wb16, wc16 = (w.astype(BF16) for w in (w_proj_a, w_proj_b, w_proj_c))
    w_inr, w_misc = _transposed_w_in(w_in)
    for l in range(depth):
        col_scale, bf_row, wq, wkv = _layer_weights(w_uq[l], w_uk[l], w_uv[l], b_f[l])
        h = _prenorm(x2, pre_norm[l])
        proj = _fused_proj(h, w_inr, wm16, l, col_scale)
        k_rope, ccol, crow = _misc_proj(h, w_misc, l, bf_row, tables[0], tables[1], batch, seq)
        q_a = _latent_proj(proj, 0, A_QR, q_a_norm[l], wq, tables,
                           (A_NOPE + A_ROPE) ** -0.5 * LOG2E)
        kv_a = _latent_proj(proj, CH_ACKV, A_KVR, kv_a_norm[l], wkv, None, 1.0)
        o_a = _attention_a(q_a, kv_a, k_rope, proj, batch, seq)
        o_b = _attention_b(proj, bias, _sink_columns(sinks[l]), batch, seq)
        o_c = _attention_c(proj, ccol, crow, batch, seq)
        y = _merge_proj(o_a, o_b, o_c, wa16, wb16, wc16, l, proj)
        x2 = _out_proj(y, wo16, l, x2, post_norm[l])
    return x2.reshape(batch, seq, D)
```

```python
import functools
import math

import jax
import jax.numpy as jnp
from jax import lax
from jax.experimental import pallas as pl
from jax.experimental.pallas import tpu as pltpu

F32 = jnp.float32
BF16 = jnp.bfloat16

D = 4096
SEQ_BLOCK = 128
NORM_EPS = 1e-6
MASKED = -1e30
LOG2E = math.log2(math.e)
HA = 16
A_QR = 1536
A_KVR = 512
A_NOPE = 128
A_ROPE = 64
A_HALF = A_ROPE // 2
A_QPAD = 256
THETA = 10000.0
HB = 32
KVB = 4
DB = 64
GB = HB // KVB
HC = 16
DC = 128
N_BUCKETS = 32
MAX_DIST = 128

LANES = 128
CHUNK = 512

CH_ACQ, CH_ACKV, CH_BQ, CH_CQ, CH_CK, CH_CV = 0, 3, 4, 8, 12, 16
CH_BZ, CH_AZ, CH_CZ = 20, 24, 28
CH_GATE = 32
CH_BKV = 56
N_CHUNKS = 57
NP = N_CHUNKS * CHUNK
BF16_SUBLANES = 16

IN_SIZES = (A_QR, A_KVR, A_ROPE, HA * A_NOPE, HB * DB, KVB * DB, KVB * DB, HB * DB,
            HC * DC, HC * DC, HC * DC, HC, HC * DC)
IN_NAMES = ("a_cq", "a_ckv", "a_kr", "a_z", "b_q", "b_k", "b_v", "b_z", "c_q", "c_k", "c_v", "c_f", "c_z")
IN_START = {name: sum(IN_SIZES[:i]) for i, name in enumerate(IN_NAMES)}
IN_WIDTH = dict(zip(IN_NAMES, IN_SIZES))
F_LANE = 64

VMEM_CAP = 60 * 1024 * 1024


def _vmem(nbytes):
    return int(min(VMEM_CAP, nbytes))


def _params(sem, nbytes):
    return pltpu.CompilerParams(dimension_semantics=sem, vmem_limit_bytes=_vmem(nbytes))


def _rope_table_kernel(pos_ref, freq_ref, cos_ref, sin_ref):
    ang = pos_ref[...].astype(F32) * freq_ref[...]
    lane = lax.broadcasted_iota(jnp.int32, ang.shape, 1)
    live = lane < A_ROPE
    s = jnp.sin(ang)
    cos_ref[...] = jnp.where(live, jnp.cos(ang), 0.0)
    sin_ref[...] = jnp.where(live, jnp.where(lane < A_HALF, -s, s), 0.0)


def _rope_tables(positions):
    t = positions.size
    tm = 2048
    half = jnp.arange(A_HALF, dtype=F32)
    inv_freq = THETA ** (-half / A_HALF)
    freq = jnp.concatenate([inv_freq, inv_freq, jnp.zeros((LANES - A_ROPE,), F32)]).reshape(1, LANES)
    pos = positions.reshape(t, 1)
    return pl.pallas_call(
        _rope_table_kernel,
        grid=(t // tm,),
        in_specs=[pl.BlockSpec((tm, 1), lambda i: (i, 0)),
                  pl.BlockSpec((1, LANES), lambda i: (0, 0))],
        out_specs=[pl.BlockSpec((tm, LANES), lambda i: (i, 0)),
                   pl.BlockSpec((tm, LANES), lambda i: (i, 0))],
        out_shape=[jax.ShapeDtypeStruct((t, LANES), F32)] * 2,
        compiler_params=_params(("arbitrary",), 32 << 20),
        name="rope_tables",
    )(pos, freq)


def _rope(t, cos, sin):
    lane = lax.broadcasted_iota(jnp.int32, t.shape, 1)
    partner = jnp.where(lane < A_HALF,
                        pltpu.roll(t, LANES - A_HALF, 1),
                        pltpu.roll(t, A_HALF, 1))
    return t * cos + partner * sin


def _relbias_kernel(table_ref, o_ref):
    h = pl.program_id(0)
    qi = lax.broadcasted_iota(jnp.int32, (SEQ_BLOCK, 2 * SEQ_BLOCK), 0)
    si = lax.broadcasted_iota(jnp.int32, (SEQ_BLOCK, 2 * SEQ_BLOCK), 1)
    dist = qi + SEQ_BLOCK - si
    max_exact = N_BUCKETS // 2
    d = jnp.maximum(dist, 0)
    large = max_exact + (jnp.log(jnp.maximum(d, 1).astype(F32) / max_exact)
                         / math.log(MAX_DIST / max_exact)
                         * (N_BUCKETS - max_exact)).astype(jnp.int32)
    large = jnp.minimum(large, N_BUCKETS - 1)
    bucket = jnp.where(d < max_exact, d, large)
    acc = jnp.zeros(bucket.shape, F32)
    for b in range(N_BUCKETS):
        acc = jnp.where(bucket == b, table_ref[b, h], acc)
    o_ref[0] = acc * LOG2E


def _rel_bias(rel_table):
    def out_map(h):
        return (h // GB, (h % GB) // 2, h % 2)
    return pl.pallas_call(
        _relbias_kernel,
        grid=(HB,),
        in_specs=[pl.BlockSpec(memory_space=pltpu.SMEM)],
        out_specs=pl.BlockSpec((1, SEQ_BLOCK, 2 * SEQ_BLOCK), out_map),
        out_shape=jax.ShapeDtypeStruct((KVB, (GB // 2) * SEQ_BLOCK, 4 * SEQ_BLOCK), F32),
        compiler_params=_params(("arbitrary",), 16 << 20),
        name="rel_bias",
    )(rel_table)


def _prenorm_kernel(x_ref, g_ref, o_ref):
    x = x_ref[...]
    r = lax.rsqrt(jnp.mean(x * x, axis=-1, keepdims=True) + NORM_EPS)
    o_ref[...] = (x * r * g_ref[...]).astype(o_ref.dtype)


def _prenorm(x2, g):
    t = x2.shape[0]
    tm = 256
    return pl.pallas_call(
        _prenorm_kernel,
        grid=(t // tm,),
        in_specs=[pl.BlockSpec((tm, D), lambda i: (i, 0)),
                  pl.BlockSpec((1, D), lambda i: (0, 0))],
        out_specs=pl.BlockSpec((tm, D), lambda i: (i, 0)),
        out_shape=jax.ShapeDtypeStruct((t, D), BF16),
        compiler_params=_params(("arbitrary",), 32 << 20),
        name="prenorm",
    )(x2, g.reshape(1, D))


ROW_CHUNK = 256


def _chunked_dot(x_ref, w_ref, o_ref, epilogue, trans_w=False):
    dims = (((1,), (1,) if trans_w else (0,)), ((), ()))
    for r in range(x_ref.shape[0] // ROW_CHUNK):
        rows = slice(r * ROW_CHUNK, (r + 1) * ROW_CHUNK)
        acc = lax.dot_general(x_ref[rows, :], w_ref[...], dims, preferred_element_type=F32)
        o_ref[rows, :] = epilogue(acc, rows).astype(o_ref.dtype)


def _proj_kernel(rows_ref, h_ref, wi_ref, wm_ref, cs_ref, o_ref, *,
                 silu_start, gate_start, gate_end):
    del rows_ref
    j = pl.program_id(1)
    wi = wi_ref.at[0]

    @pl.when((j < silu_start) | (j >= gate_end))
    def _():
        _chunked_dot(h_ref, wi, o_ref, lambda a, rows: a * cs_ref[...], trans_w=True)

    @pl.when((j >= silu_start) & (j < gate_start))
    def _():
        _chunked_dot(h_ref, wi, o_ref, lambda a, rows: a * jax.nn.sigmoid(a), trans_w=True)

    @pl.when((j >= gate_start) & (j < gate_end))
    def _():
        _chunked_dot(h_ref, wm_ref, o_ref, lambda a, rows: jax.nn.sigmoid(a))


def _source_rows():
    starts = {CH_ACQ: "a_cq", CH_ACKV: "a_ckv", CH_BQ: "b_q", CH_CQ: "c_q", CH_CK: "c_k",
              CH_CV: "c_v", CH_BZ: "b_z", CH_AZ: "a_z", CH_CZ: "c_z", CH_BKV: "b_k"}
    rows, row = [], 0
    for j in range(N_CHUNKS):
        if CH_GATE <= j < CH_BKV:
            rows.append(rows[-1])
            continue
        if j in starts:
            row = IN_START[starts[j]]
        rows.append(row)
        row += CHUNK
    return jnp.asarray(rows, jnp.int32)


def _fused_proj(h, w_t, w_merge, layer, col_scale):
    t = h.shape[0]
    tm, tn = 2048, CHUNK
    n_gate = CH_BKV - CH_GATE
    kern = functools.partial(_proj_kernel, silu_start=CH_BZ, gate_start=CH_GATE, gate_end=CH_BKV)

    def wi_map(i, j, rows):
        return (layer, pl.multiple_of(rows[j], BF16_SUBLANES), 0)

    def wm_map(i, j, rows):
        return (layer, 0, jnp.clip(j - CH_GATE, 0, n_gate - 1))

    need = 2 * (tm * D * 2 + 2 * D * tn * 2 + tm * tn * 2) + 4 * tm * tn * 4
    grid_spec = pltpu.PrefetchScalarGridSpec(
        num_scalar_prefetch=1,
        grid=(t // tm, NP // tn),
        in_specs=[pl.BlockSpec((tm, D), lambda i, j, rows: (i, 0)),
                  pl.BlockSpec((pl.Element(1), pl.Element(tn), pl.Element(D)), wi_map),
                  pl.BlockSpec((None, D, tn), wm_map),
                  pl.BlockSpec((1, tn), lambda i, j, rows: (0, j))],
        out_specs=pl.BlockSpec((tm, tn), lambda i, j, rows: (i, j)),
    )
    return pl.pallas_call(
        kern,
        grid_spec=grid_spec,
        out_shape=jax.ShapeDtypeStruct((t, NP), BF16),
        compiler_params=_params(("parallel", "arbitrary"), need + (8 << 20)),
        name="fused_proj",
    )(_source_rows(), h, w_t, w_merge, col_scale)


def _misc_kernel(h_ref, w_ref, bf_ref, cos_ref, sin_ref, kr_ref, ccol_ref, crow_ref, carry_ref):
    t = pl.program_id(1)
    acc = lax.dot_general(h_ref[...], w_ref[...], (((1,), (1,)), ((), ())),
                          preferred_element_type=F32)
    kr_ref[...] = _rope(acc, cos_ref[...], sin_ref[...]).astype(kr_ref.dtype)

    xx = acc + bf_ref[...]
    log_f = jnp.minimum(xx, 0.0) - jnp.log1p(jnp.exp(-jnp.abs(xx)))
    hi = log_f.astype(BF16)
    r1 = log_f - hi.astype(F32)
    mid = r1.astype(BF16)
    lo = (r1 - mid.astype(F32)).astype(BF16)
    tm = acc.shape[0]
    ri = lax.broadcasted_iota(jnp.int32, (tm, tm), 0)
    ci = lax.broadcasted_iota(jnp.int32, (tm, tm), 1)
    tri = jnp.where(ri >= ci, 1.0, 0.0).astype(BF16)
    cs = (jnp.dot(tri, hi, preferred_element_type=F32)
          + jnp.dot(tri, mid, preferred_element_type=F32)
          + jnp.dot(tri, lo, preferred_element_type=F32))

    @pl.when(t == 0)
    def _():
        carry_ref[...] = jnp.zeros_like(carry_ref)

    cs = cs + carry_ref[...]
    carry_ref[...] = cs[tm - 1:tm, :]
    cs2 = cs * LOG2E
    ccol_ref[...] = cs2
    crow_ref[0] = cs2.T


def _misc_proj(h, w_misc, layer, bf_row, cos_t, sin_t, batch, seq):
    t = h.shape[0]
    tm = 512
    nt = seq // tm
    need = 2 * (tm * D * 2 + D * LANES * 2 + 5 * tm * LANES * 4) + 4 * tm * tm * 4
    return pl.pallas_call(
        _misc_kernel,
        grid=(batch, nt),
        in_specs=[pl.BlockSpec((tm, D), lambda b, i: (b * nt + i, 0)),
                  pl.BlockSpec((None, LANES, D), lambda b, i: (layer, 0, 0)),
                  pl.BlockSpec((1, LANES), lambda b, i: (0, 0)),
                  pl.BlockSpec((tm, LANES), lambda b, i: (b * nt + i, 0)),
                  pl.BlockSpec((tm, LANES), lambda b, i: (b * nt + i, 0))],
        out_specs=[pl.BlockSpec((tm, LANES), lambda b, i: (b * nt + i, 0)),
                   pl.BlockSpec((tm, LANES), lambda b, i: (b * nt + i, 0)),
                   pl.BlockSpec((1, LANES, tm), lambda b, i: (b, 0, i))],
        out_shape=[jax.ShapeDtypeStruct((t, LANES), BF16),
                   jax.ShapeDtypeStruct((t, LANES), F32),
                   jax.ShapeDtypeStruct((batch, LANES, seq), F32)],
        scratch_shapes=[pltpu.VMEM((1, LANES), F32)],
        compiler_params=_params(("arbitrary", "arbitrary"), need + (8 << 20)),
        name="misc_proj",
    )(h, w_misc, bf_row, cos_t, sin_t)


def _latent_kernel(*refs, rope, out_scale):
    if rope:
        x_ref, g_ref, w_ref, cos_ref, sin_ref, o_ref, xn_ref = refs
    else:
        x_ref, g_ref, w_ref, o_ref, xn_ref = refs

    @pl.when(pl.program_id(1) == 0)
    def _():
        x = x_ref[...].astype(F32)
        r = lax.rsqrt(jnp.mean(x * x, axis=-1, keepdims=True) + NORM_EPS)
        xn_ref[...] = (x * r * g_ref[...]).astype(xn_ref.dtype)

    def rotary(acc, rows):
        cos = cos_ref[rows, :]
        sin = sin_ref[rows, :]
        pieces = []
        for hh in range(acc.shape[1] // A_QPAD):
            base = hh * A_QPAD
            pieces.append(acc[:, base:base + A_NOPE])
            pieces.append(_rope(acc[:, base + A_NOPE:base + A_QPAD], cos, sin))
        return jnp.concatenate(pieces, axis=1) * out_scale

    _chunked_dot(xn_ref, w_ref, o_ref, rotary if rope else (lambda acc, rows: acc))


def _latent_proj(proj, col_block, k, g, w, tables, out_scale):
    t = proj.shape[0]
    n = w.shape[1]
    tm, tn = 1024, 1024
    rope = tables is not None
    kern = functools.partial(_latent_kernel, rope=rope, out_scale=out_scale)
    in_specs = [pl.BlockSpec((tm, k), lambda i, j: (i, col_block)),
                pl.BlockSpec((1, k), lambda i, j: (0, 0)),
                pl.BlockSpec((k, tn), lambda i, j: (0, j))]
    args = [proj, g.reshape(1, k), w]
    if rope:
        in_specs += [pl.BlockSpec((tm, LANES), lambda i, j: (i, 0))] * 2
        args += list(tables)
    need = 2 * (tm * k * 2 + k * tn * 2 + tm * tn * 2 + 2 * tm * LANES * 4) + tm * k * 2 + 4 * tm * tn * 4
    return pl.pallas_call(
        kern,
        grid=(t // tm, n // tn),
        in_specs=in_specs,
        out_specs=pl.BlockSpec((tm, tn), lambda i, j: (i, j)),
        out_shape=jax.ShapeDtypeStruct((t, n), BF16),
        scratch_shapes=[pltpu.VMEM((tm, k), BF16)],
        compiler_params=_params(("parallel", "arbitrary"), need + (8 << 20)),
        name="latent_q" if rope else "latent_kv",
    )(*args)


def _causal_attn_kernel(*refs, tq, fox, heads):
    if fox:
        q_ref, k_ref, v_ref, z_ref, ccol_ref, crow_ref, o_ref, vaug_ref = refs
    else:
        q_ref, kn_ref, kr_ref, v_ref, z_ref, o_ref, kcat_ref, vaug_ref = refs
    seq = q_ref.shape[0]
    dk = q_ref.shape[1] // heads
    lane = lax.broadcasted_iota(jnp.int32, (seq, LANES), 1)
    ones_col = jnp.where(lane == 0, 1.0, 0.0).astype(BF16)
    keys, cq_all, ck_all = [], [], []
    for hh in range(heads):
        cols = slice(hh * LANES, (hh + 1) * LANES)
        vaug_ref[hh, :, 0:LANES] = v_ref[:, cols]
        vaug_ref[hh, :, LANES:2 * LANES] = ones_col
        if fox:
            keys.append(k_ref.at[:, cols])
            h = pl.program_id(1) * heads + hh
            cq_all.append(jnp.sum(jnp.where(lane == F_LANE + h, ccol_ref[...], 0.0),
                                  axis=1, keepdims=True))
            ck_all.append(crow_ref[0, pl.ds(F_LANE + h, 1), :])
        else:
            kcat_ref[hh, :, 0:LANES] = kn_ref[:, cols]
            kcat_ref[hh, :, LANES:2 * LANES] = kr_ref[...]
            keys.append(kcat_ref.at[hh])
    row = lax.broadcasted_iota(jnp.int32, (tq, tq), 0)
    col = lax.broadcasted_iota(jnp.int32, (tq, tq), 1)
    causal = row >= col
    dims = (((1,), (1,)), ((), ()))

    def logits(item):
        hh, c = item
        lo, hi = c * tq, (c + 1) * tq
        q = q_ref[lo:hi, hh * dk:(hh + 1) * dk]
        t_d = lax.dot_general(q, keys[hh][lo:hi, :], dims, preferred_element_type=F32)
        if fox:
            t_d = t_d - ck_all[hh][:, lo:hi]
        t_d = jnp.where(causal, t_d, MASKED)
        t_o = None
        if c:
            t_o = lax.dot_general(q, keys[hh][0:lo, :], dims, preferred_element_type=F32)
            if fox:
                t_o = t_o - ck_all[hh][:, 0:lo]
        return t_d, t_o

    nq = seq // tq
    order = [(hh, c) for hh in range(heads) for c in range(nq - 1, -1, -1)]
    queue = [logits(item) for item in order[:LOOKAHEAD]]
    for i, (hh, c) in enumerate(order):
        lo, hi = c * tq, (c + 1) * tq
        cols = slice(hh * LANES, (hh + 1) * LANES)
        t_d, t_o = queue.pop(0)
        if i + LOOKAHEAD < len(order):
            queue.append(logits(order[i + LOOKAHEAD]))
        mt = jnp.max(t_d, axis=1, keepdims=True)
        if c:
            mt = jnp.maximum(mt, jnp.max(t_o, axis=1, keepdims=True))
        if fox:
            cq = cq_all[hh][lo:hi, :]
            shift = (mt + cq) - cq
        else:
            shift = mt
        acc = jnp.dot(jnp.exp2(t_d - shift).astype(BF16), vaug_ref[hh, lo:hi, :],
                      preferred_element_type=F32)
        if c:
            acc = acc + jnp.dot(jnp.exp2(t_o - shift).astype(BF16), vaug_ref[hh, 0:lo, :],
                                preferred_element_type=F32)
        inv_l = 1.0 / acc[:, LANES:LANES + 1]
        o_ref[lo:hi, cols] = (acc[:, 0:LANES] * inv_l
                              * z_ref[lo:hi, cols].astype(F32)).astype(o_ref.dtype)


ATTN_TQ = 256
LOOKAHEAD = 2
ATTN_HEADS = 4


def _attention_a(q_a, kv_a, k_rope, proj, batch, seq):
    t = q_a.shape[0]
    hs = ATTN_HEADS
    kern = functools.partial(_causal_attn_kernel, tq=ATTN_TQ, fox=False, heads=hs)
    z_col = CH_AZ * CHUNK // LANES
    need = 2 * (hs * seq * A_QPAD * 2 + (4 * hs + 1) * seq * LANES * 2) + 2 * hs * seq * A_QPAD * 2
    wide = pl.BlockSpec((seq, hs * LANES), lambda b, h: (b, h))
    return pl.pallas_call(
        kern,
        grid=(batch, HA // hs),
        in_specs=[pl.BlockSpec((seq, hs * A_QPAD), lambda b, h: (b, h)),
                  wide,
                  pl.BlockSpec((seq, LANES), lambda b, h: (b, 0)),
                  pl.BlockSpec((seq, hs * LANES), lambda b, h: (b, HA // hs + h)),
                  pl.BlockSpec((seq, hs * LANES), lambda b, h: (b, z_col // hs + h))],
        out_specs=wide,
        out_shape=jax.ShapeDtypeStruct((t, HA * LANES), BF16),
        scratch_shapes=[pltpu.VMEM((hs, seq, A_QPAD), BF16), pltpu.VMEM((hs, seq, 2 * LANES), BF16)],
        compiler_params=_params(("parallel", "arbitrary"), need + (24 << 20)),
        name="attn_mla",
    )(q_a, kv_a, k_rope, kv_a, proj)


def _attention_c(proj, ccol, crow, batch, seq):
    t = proj.shape[0]
    hs = ATTN_HEADS
    kern = functools.partial(_causal_attn_kernel, tq=ATTN_TQ, fox=True, heads=hs)
    q_col, k_col, v_col, z_col = (c * CHUNK // (hs * LANES) for c in (CH_CQ, CH_CK, CH_CV, CH_CZ))
    need = 2 * (5 * hs * seq * LANES * 2 + 2 * seq * LANES * 4) + hs * seq * 2 * LANES * 2

    def head_cols(first):
        return pl.BlockSpec((seq, hs * LANES), lambda b, h: (b, first + h))

    return pl.pallas_call(
        kern,
        grid=(batch, HC // hs),
        in_specs=[head_cols(q_col), head_cols(k_col), head_cols(v_col), head_cols(z_col),
                  pl.BlockSpec((seq, LANES), lambda b, h: (b, 0)),
                  pl.BlockSpec((1, LANES, seq), lambda b, h: (b, 0, 0))],
        out_specs=head_cols(0),
        out_shape=jax.ShapeDtypeStruct((t, HC * DC), BF16),
        scratch_shapes=[pltpu.VMEM((hs, seq, 2 * LANES), BF16)],
        compiler_params=_params(("parallel", "arbitrary"), need + (24 << 20)),
        name="attn_fox",
    )(proj, proj, proj, proj, ccol, crow)


def _swa_kernel(q_ref, kvp_ref, kvc_ref, z_ref, bias_ref, sink_ref, o_ref):
    n = pl.program_id(1)
    blk = SEQ_BLOCK
    band = 2 * blk
    rows = (GB // 2) * blk
    kv = jnp.concatenate([kvp_ref[...], kvc_ref[...]], axis=0).astype(F32)
    lane = lax.broadcasted_iota(jnp.int32, (band, LANES), 1)
    low = lane < DB

    def split(two_heads):
        lo = jnp.where(low, two_heads, 0.0)
        hi = jnp.where(low, 0.0, two_heads)
        return (lo, pltpu.roll(lo, DB, 1)), (pltpu.roll(hi, DB, 1), hi)

    k_heads = split(kv[:, 0:LANES]) + split(kv[:, LANES:2 * LANES])
    v_heads = split(kv[:, 2 * LANES:3 * LANES]) + split(kv[:, 3 * LANES:4 * LANES])

    r = lax.broadcasted_iota(jnp.int32, (rows, band), 0) % blk
    c = lax.broadcasted_iota(jnp.int32, (rows, band), 1)
    dist = r + blk - c
    first_key = jnp.where(n > 0, 0, blk)
    valid = (dist >= 0) & (dist < blk) & (c >= first_key)

    def logits(kh):
        k_cat = jnp.concatenate(k_heads[kh], axis=0).astype(BF16)
        q4 = jnp.concatenate(
            [q_ref[:, (kh * (GB // 2) + jj) * LANES:(kh * (GB // 2) + jj + 1) * LANES]
             for jj in range(GB // 2)], axis=0)
        s = lax.dot_general(q4, k_cat, (((1,), (1,)), ((), ())), preferred_element_type=F32)
        return s + bias_ref[kh]

    nxt = logits(0)
    for kh in range(KVB):
        s = nxt
        if kh + 1 < KVB:
            nxt = logits(kh + 1)
        v_cat = jnp.concatenate(v_heads[kh], axis=0).astype(BF16)
        halves, inv = [], []
        for e in range(2):
            se = jnp.where(valid, s[:, e * band:(e + 1) * band], MASKED)
            sink = sink_ref[kh, :, e:e + 1]
            m = jnp.maximum(jnp.max(se, axis=1, keepdims=True), sink)
            p = jnp.exp2(se - m)
            denom = jnp.sum(p, axis=1, keepdims=True) + jnp.exp2(sink - m)
            halves.append(p.astype(BF16))
            inv.append(1.0 / denom)
        p_cat = jnp.concatenate(halves, axis=1)
        o = jnp.dot(p_cat, v_cat, preferred_element_type=F32)
        o = o * jnp.where(lax.broadcasted_iota(jnp.int32, o.shape, 1) < DB, inv[0], inv[1])
        for jj in range(GB // 2):
            c0 = (kh * (GB // 2) + jj) * LANES
            o_ref[:, c0:c0 + LANES] = (o[jj * blk:(jj + 1) * blk, :]
                                       * z_ref[:, c0:c0 + LANES].astype(F32)).astype(o_ref.dtype)


def _attention_b(proj, bias, sink_cols, batch, seq):
    t = proj.shape[0]
    blk = SEQ_BLOCK
    nb = seq // blk
    width = HB * DB
    q_col = CH_BQ * CHUNK // width
    kv_col = CH_BKV
    z_col = CH_BZ * CHUNK // width
    assert CH_BQ * CHUNK % width == 0 and CH_BZ * CHUNK % width == 0
    need = 2 * (3 * blk * width * 2 + 2 * blk * CHUNK * 2 + bias.size * 4 + sink_cols.size * 4)
    return pl.pallas_call(
        _swa_kernel,
        grid=(batch, nb),
        in_specs=[pl.BlockSpec((blk, width), lambda b, n: (b * nb + n, q_col)),
                  pl.BlockSpec((blk, CHUNK), lambda b, n: (b * nb + jnp.maximum(n - 1, 0), kv_col)),
                  pl.BlockSpec((blk, CHUNK), lambda b, n: (b * nb + n, kv_col)),
                  pl.BlockSpec((blk, width), lambda b, n: (b * nb + n, z_col)),
                  pl.BlockSpec(bias.shape, lambda b, n: (0, 0, 0)),
                  pl.BlockSpec(sink_cols.shape, lambda b, n: (0, 0, 0))],
        out_specs=pl.BlockSpec((blk, width), lambda b, n: (b * nb + n, 0)),
        out_shape=jax.ShapeDtypeStruct((t, width), BF16),
        compiler_params=_params(("parallel", "arbitrary"), need + (24 << 20)),
        name="attn_swa",
    )(proj, proj, proj, proj, bias, sink_cols)


def _merge_kernel(oa_ref, ob_ref, oc_ref, wa_ref, wb_ref, wc_ref, ga_ref, gb_ref, gc_ref, y_ref):
    branches = ((oa_ref, wa_ref, ga_ref), (ob_ref, wb_ref, gb_ref), (oc_ref, wc_ref, gc_ref))
    for r in range(y_ref.shape[0] // ROW_CHUNK):
        rows = slice(r * ROW_CHUNK, (r + 1) * ROW_CHUNK)
        y = None
        for o_ref, w_ref, g_ref in branches:
            term = g_ref[rows, :].astype(F32) * jnp.dot(o_ref[rows, :], w_ref[...],
                                                        preferred_element_type=F32)
            y = term if y is None else y + term
        y_ref[rows, :] = y.astype(y_ref.dtype)


def _merge_proj(oa, ob, oc, wa, wb, wc, layer, proj):
    t, k = oa.shape
    tm, tn = 1024, CHUNK
    gcols = D // tn
    o_spec = pl.BlockSpec((tm, k), lambda i, j: (i, 0))
    w_spec = pl.BlockSpec((None, k, tn), lambda i, j: (layer, 0, j))

    def gate_spec(branch):
        return pl.BlockSpec((tm, tn), lambda i, j: (i, CH_GATE + branch * gcols + j))

    need = 2 * (3 * tm * k * 2 + 3 * k * tn * 2 + 4 * tm * tn * 2) + 4 * tm * tn * 4
    return pl.pallas_call(
        _merge_kernel,
        grid=(t // tm, D // tn),
        in_specs=[o_spec, o_spec, o_spec, w_spec, w_spec, w_spec,
                  gate_spec(0), gate_spec(1), gate_spec(2)],
        out_specs=pl.BlockSpec((tm, tn), lambda i, j: (i, j)),
        out_shape=jax.ShapeDtypeStruct((t, D), BF16),
        compiler_params=_params(("parallel", "arbitrary"), need + (8 << 20)),
        name="merge_proj",
    )(oa, ob, oc, wa, wb, wc, proj, proj, proj)


def _out_kernel(y_ref, w_ref, x_ref, g_ref, o_ref, acc_ref, ssq_ref):
    i = pl.program_id(0)
    j = pl.program_id(1)
    slot = i % 2
    prev = 1 - slot

    @pl.when((i == 0) & (j == 0))
    def _():
        acc_ref[...] = jnp.zeros_like(acc_ref)
        ssq_ref[...] = jnp.zeros_like(ssq_ref)

    def finish():
        rs = lax.rsqrt(ssq_ref[prev] * (1.0 / D) + NORM_EPS)
        o_ref[...] = x_ref[...] + acc_ref[j] * rs * g_ref[...]

    @pl.when(i < pl.num_programs(0) - 1)
    def _():
        finish()
        ss = jnp.where(j == 0, 0.0, ssq_ref[slot])
        for r in range(y_ref.shape[0] // ROW_CHUNK):
            rows = slice(r * ROW_CHUNK, (r + 1) * ROW_CHUNK)
            acc = jnp.dot(y_ref[rows, :], w_ref[...], preferred_element_type=F32)
            acc_ref[j, rows, :] = acc
            ssq_ref[slot, rows, :] = ss[rows, :] + jnp.sum(acc * acc, axis=1, keepdims=True)

    @pl.when(i == pl.num_programs(0) - 1)
    def _():
        finish()


def _out_proj(y, w_o, layer, x2, g):
    t = y.shape[0]
    tm, tn = 1024, CHUNK
    ni, nj = t // tm, D // tn
    need = 2 * (tm * D * 2 + D * tn * 2 + 2 * tm * tn * 4) + tm * D * 4 + 4 * tm * tn * 4
    return pl.pallas_call(
        _out_kernel,
        grid=(ni + 1, nj),
        in_specs=[pl.BlockSpec((tm, D), lambda i, j: (jnp.minimum(i, ni - 1), 0)),
                  pl.BlockSpec((None, D, tn), lambda i, j: (layer, 0, j)),
                  pl.BlockSpec((tm, tn), lambda i, j: (jnp.maximum(i - 1, 0), j)),
                  pl.BlockSpec((1, tn), lambda i, j: (0, j))],
        out_specs=pl.BlockSpec((tm, tn), lambda i, j: (jnp.maximum(i - 1, 0), j)),
        out_shape=jax.ShapeDtypeStruct((t, D), F32),
        scratch_shapes=[pltpu.VMEM((nj, tm, tn), F32), pltpu.VMEM((2, tm, 1), F32)],
        compiler_params=_params(("arbitrary", "arbitrary"), need + (8 << 20)),
        name="out_proj",
    )(y, w_o, x2, g.reshape(1, D))


def _cast_kernel(x_ref, o_ref):
    o_ref[...] = x_ref[...].astype(o_ref.dtype)


def _to_bf16(w):
    nl, r, c = w.shape
    tr, tc = 512, 2048
    spec = pl.BlockSpec((None, tr, tc), lambda l, i, j: (l, i, j))
    return pl.pallas_call(
        _cast_kernel,
        grid=(nl, r // tr, c // tc),
        in_specs=[spec],
        out_specs=spec,
        out_shape=jax.ShapeDtypeStruct(w.shape, BF16),
        compiler_params=_params(("arbitrary",) * 3, 6 * tr * tc * 4),
        name="cast_bf16",
    )(w)


def _transposed_w_in(w_in):
    w_t = jnp.swapaxes(w_in, 1, 2).astype(BF16)

    def rows(name):
        return w_t[:, IN_START[name]:IN_START[name] + IN_WIDTH[name], :]

    pad = jnp.zeros((w_in.shape[0], LANES - F_LANE - HC, D), BF16)
    misc = jnp.concatenate([rows("a_kr"), rows("c_f"), pad], axis=1)
    return w_t, misc


def _layer_weights(w_uq, w_uk, w_uv, b_f):
    col_scale = jnp.ones((NP,), F32)
    col_scale = col_scale.at[CH_BQ * CHUNK:CH_CQ * CHUNK].set(DB ** -0.5 * LOG2E)
    col_scale = col_scale.at[CH_CQ * CHUNK:CH_CK * CHUNK].set(DC ** -0.5 * LOG2E)
    col_scale = col_scale.reshape(1, NP)
    bf_row = jnp.concatenate([jnp.zeros((F_LANE,), F32), b_f.astype(F32),
                              jnp.zeros((LANES - F_LANE - HC,), F32)]).reshape(1, LANES)
    wq = jnp.pad(w_uq, ((0, 0), (0, 0), (0, A_QPAD - A_NOPE - A_ROPE)))
    wq = wq.reshape(A_QR, HA * A_QPAD).astype(BF16)
    wkv = jnp.concatenate([w_uk.reshape(A_KVR, HA * A_NOPE), w_uv.reshape(A_KVR, HA * A_NOPE)],
                          axis=1).astype(BF16)
    return col_scale, bf_row, wq, wkv


def _sink_columns(sinks):
    s = (sinks.astype(F32) * LOG2E).reshape(KVB, GB // 2, 1, 2)
    s = jnp.broadcast_to(s, (KVB, GB // 2, SEQ_BLOCK, 2)).reshape(KVB, (GB // 2) * SEQ_BLOCK, 2)
    return jnp.pad(s, ((0, 0), (0, 0), (0, LANES - 2)))


def kernel(x, positions, rel_table, pre_norm, w_in, q_a_norm, kv_a_norm, w_uq, w_uk, w_uv,
           sinks, b_f, w_proj_a, w_proj_b, w_proj_c, w_merge, w_o, post_norm):
    batch, seq, _ = x.shape
    depth = w_in.shape[0]
    x2 = x.reshape(batch * seq, D)
    tables = _rope_tables(positions)
    bias = _rel_bias(rel_table)
    wm16, wo16 = _to_bf16(w_merge), w_o.astype(BF16)
    wa16, wb16, wc16 = (w.astype(BF16) for w in (w_proj_a, w_proj_b, w_proj_c))
    w_inr, w_misc = _transposed_w_in(w_in)
    for l in range(depth):
        col_scale, bf_row, wq, wkv = _layer_weights(w_uq[l], w_uk[l], w_uv[l], b_f[l])
        h = _prenorm(x2, pre_norm[l])
        proj = _fused_proj(h, w_inr, wm16, l, col_scale)
        k_rope, ccol, crow = _misc_proj(h, w_misc, l, bf_row, tables[0], tables[1], batch, seq)
        q_a = _latent_proj(proj, 0, A_QR, q_a_norm[l], wq, tables,
                           (A_NOPE + A_ROPE) ** -0.5 * LOG2E)
        kv_a = _latent_proj(proj, CH_ACKV, A_KVR, kv_a_norm[l], wkv, None, 1.0)
        o_a = _attention_a(q_a, kv_a, k_rope, proj, batch, seq)
        o_b = _attention_b(proj, bias, _sink_columns(sinks[l]), batch, seq)
        o_c = _attention_c(proj, ccol, crow, batch, seq)
        y = _merge_proj(o_a, o_b, o_c, wa16, wb16, wc16, l, proj)
        x2 = _out_proj(y, wo16, l, x2, post_norm[l])
    return x2.reshape(batch, seq, D)
```

```python
import functools
import math

import jax
import jax.numpy as jnp
from jax import lax
from jax.experimental import pallas as pl
from jax.experimental.pallas import tpu as pltpu

F32 = jnp.float32
BF16 = jnp.bfloat16

D = 4096
SEQ_BLOCK = 128
NORM_EPS = 1e-6
MASKED = -1e30
LOG2E = math.log2(math.e)
HA = 16
A_QR = 1536
A_KVR = 512
A_NOPE = 128
A_ROPE = 64
A_HALF = A_ROPE // 2
A_QPAD = 256
THETA = 10000.0
HB = 32
KVB = 4
DB = 64
GB = HB // KVB
HC = 16
DC = 128
N_BUCKETS = 32
MAX_DIST = 128

LANES = 128
CHUNK = 512

CH_ACQ, CH_ACKV, CH_BQ, CH_CQ, CH_CK, CH_CV = 0, 3, 4, 8, 12, 16
CH_BZ, CH_AZ, CH_CZ = 20, 24, 28
CH_GATE = 32
CH_BKV = 56
N_CHUNKS = 57
NP = N_CHUNKS * CHUNK
BF16_SUBLANES = 16

IN_SIZES = (A_QR, A_KVR, A_ROPE, HA * A_NOPE, HB * DB, KVB * DB, KVB * DB, HB * DB,
            HC * DC, HC * DC, HC * DC, HC, HC * DC)
IN_NAMES = ("a_cq", "a_ckv", "a_kr", "a_z", "b_q", "b_k", "b_v", "b_z", "c_q", "c_k", "c_v", "c_f", "c_z")
IN_START = {name: sum(IN_SIZES[:i]) for i, name in enumerate(IN_NAMES)}
IN_WIDTH = dict(zip(IN_NAMES, IN_SIZES))
F_LANE = 64

VMEM_CAP = 60 * 1024 * 1024


def _vmem(nbytes):
    return int(min(VMEM_CAP, nbytes))


def _params(sem, nbytes):
    return pltpu.CompilerParams(dimension_semantics=sem, vmem_limit_bytes=_vmem(nbytes))


def _rope_table_kernel(pos_ref, freq_ref, cos_ref, sin_ref):
    ang = pos_ref[...].astype(F32) * freq_ref[...]
    lane = lax.broadcasted_iota(jnp.int32, ang.shape, 1)
    live = lane < A_ROPE
    s = jnp.sin(ang)
    cos_ref[...] = jnp.where(live, jnp.cos(ang), 0.0)
    sin_ref[...] = jnp.where(live, jnp.where(lane < A_HALF, -s, s), 0.0)


def _rope_tables(positions):
    t = positions.size
    tm = 2048
    half = jnp.arange(A_HALF, dtype=F32)
    inv_freq = THETA ** (-half / A_HALF)
    freq = jnp.concatenate([inv_freq, inv_freq, jnp.zeros((LANES - A_ROPE,), F32)]).reshape(1, LANES)
    pos = positions.reshape(t, 1)
    return pl.pallas_call(
        _rope_table_kernel,
        grid=(t // tm,),
        in_specs=[pl.BlockSpec((tm, 1), lambda i: (i, 0)),
                  pl.BlockSpec((1, LANES), lambda i: (0, 0))],
        out_specs=[pl.BlockSpec((tm, LANES), lambda i: (i, 0)),
                   pl.BlockSpec((tm, LANES), lambda i: (i, 0))],
        out_shape=[jax.ShapeDtypeStruct((t, LANES), F32)] * 2,
        compiler_params=_params(("arbitrary",), 32 << 20),
        name="rope_tables",
    )(pos, freq)


def _rope(t, cos, sin):
    lane = lax.broadcasted_iota(jnp.int32, t.shape, 1)
    partner = jnp.where(lane < A_HALF,
                        pltpu.roll(t, LANES - A_HALF, 1),
                        pltpu.roll(t, A_HALF, 1))
    return t * cos + partner * sin


def _relbias_kernel(table_ref, o_ref):
    h = pl.program_id(0)
    qi = lax.broadcasted_iota(jnp.int32, (SEQ_BLOCK, 2 * SEQ_BLOCK), 0)
    si = lax.broadcasted_iota(jnp.int32, (SEQ_BLOCK, 2 * SEQ_BLOCK), 1)
    dist = qi + SEQ_BLOCK - si
    max_exact = N_BUCKETS // 2
    d = jnp.maximum(dist, 0)
    large = max_exact + (jnp.log(jnp.maximum(d, 1).astype(F32) / max_exact)
                         / math.log(MAX_DIST / max_exact)
                         * (N_BUCKETS - max_exact)).astype(jnp.int32)
    large = jnp.minimum(large, N_BUCKETS - 1)
    bucket = jnp.where(d < max_exact, d, large)
    acc = jnp.zeros(bucket.shape, F32)
    for b in range(N_BUCKETS):
        acc = jnp.where(bucket == b, table_ref[b, h], acc)
    o_ref[0] = acc * LOG2E


def _rel_bias(rel_table):
    def out_map(h):
        return (h // GB, (h % GB) // 2, h % 2)
    return pl.pallas_call(
        _relbias_kernel,
        grid=(HB,),
        in_specs=[pl.BlockSpec(memory_space=pltpu.SMEM)],
        out_specs=pl.BlockSpec((1, SEQ_BLOCK, 2 * SEQ_BLOCK), out_map),
        out_shape=jax.ShapeDtypeStruct((KVB, (GB // 2) * SEQ_BLOCK, 4 * SEQ_BLOCK), F32),
        compiler_params=_params(("arbitrary",), 16 << 20),
        name="rel_bias",
    )(rel_table)


def _prenorm_kernel(x_ref, g_ref, o_ref):
    x = x_ref[...]
    r = lax.rsqrt(jnp.mean(x * x, axis=-1, keepdims=True) + NORM_EPS)
    o_ref[...] = (x * r * g_ref[...]).astype(o_ref.dtype)


def _prenorm(x2, g):
    t = x2.shape[0]
    tm = 256
    return pl.pallas_call(
        _prenorm_kernel,
        grid=(t // tm,),
        in_specs=[pl.BlockSpec((tm, D), lambda i: (i, 0)),
                  pl.BlockSpec((1, D), lambda i: (0, 0))],
        out_specs=pl.BlockSpec((tm, D), lambda i: (i, 0)),
        out_shape=jax.ShapeDtypeStruct((t, D), BF16),
        compiler_params=_params(("arbitrary",), 32 << 20),
        name="prenorm",
    )(x2, g.reshape(1, D))


ROW_CHUNK = 256


def _chunked_dot(x_ref, w_ref, o_ref, epilogue, trans_w=False):
    dims = (((1,), (1,) if trans_w else (0,)), ((), ()))
    for r in range(x_ref.shape[0] // ROW_CHUNK):
        rows = slice(r * ROW_CHUNK, (r + 1) * ROW_CHUNK)
        acc = lax.dot_general(x_ref[rows, :], w_ref[...], dims, preferred_element_type=F32)
        o_ref[rows, :] = epilogue(acc, rows).astype(o_ref.dtype)


def _proj_kernel(rows_ref, h_ref, wi_ref, wm_ref, cs_ref, o_ref, *,
                 silu_start, gate_start, gate_end):
    del rows_ref
    j = pl.program_id(1)
    wi = wi_ref.at[0]

    @pl.when((j < silu_start) | (j >= gate_end))
    def _():
        _chunked_dot(h_ref, wi, o_ref, lambda a, rows: a * cs_ref[...], trans_w=True)

    @pl.when((j >= silu_start) & (j < gate_start))
    def _():
        _chunked_dot(h_ref, wi, o_ref, lambda a, rows: a * jax.nn.sigmoid(a), trans_w=True)

    @pl.when((j >= gate_start) & (j < gate_end))
    def _():
        _chunked_dot(h_ref, wm_ref, o_ref, lambda a, rows: jax.nn.sigmoid(a))


def _source_rows():
    starts = {CH_ACQ: "a_cq", CH_ACKV: "a_ckv", CH_BQ: "b_q", CH_CQ: "c_q", CH_CK: "c_k",
              CH_CV: "c_v", CH_BZ: "b_z", CH_AZ: "a_z", CH_CZ: "c_z", CH_BKV: "b_k"}
    rows, row = [], 0
    for j in range(N_CHUNKS):
        if CH_GATE <= j < CH_BKV:
            rows.append(rows[-1])
            continue
        if j in starts:
            row = IN_START[starts[j]]
        rows.append(row)
        row += CHUNK
    return jnp.asarray(rows, jnp.int32)


def _fused_proj(h, w_t, w_merge, layer, col_scale):
    t = h.shape[0]
    tm, tn = 2048, CHUNK
    n_gate = CH_BKV - CH_GATE
    kern = functools.partial(_proj_kernel, silu_start=CH_BZ, gate_start=CH_GATE, gate_end=CH_BKV)

    def wi_map(i, j, rows):
        return (layer, pl.multiple_of(rows[j], BF16_SUBLANES), 0)

    def wm_map(i, j, rows):
        return (layer, 0, jnp.clip(j - CH_GATE, 0, n_gate - 1))

    need = 2 * (tm * D * 2 + 2 * D * tn * 2 + tm * tn * 2) + 4 * tm * tn * 4
    grid_spec = pltpu.PrefetchScalarGridSpec(
        num_scalar_prefetch=1,
        grid=(t // tm, NP // tn),
        in_specs=[pl.BlockSpec((tm, D), lambda i, j, rows: (i, 0)),
                  pl.BlockSpec((pl.Element(1), pl.Element(tn), pl.Element(D)), wi_map),
                  pl.BlockSpec((None, D, tn), wm_map),
                  pl.BlockSpec((1, tn), lambda i, j, rows: (0, j))],
        out_specs=pl.BlockSpec((tm, tn), lambda i, j, rows: (i, j)),
    )
    return pl.pallas_call(
        kern,
        grid_spec=grid_spec,
        out_shape=jax.ShapeDtypeStruct((t, NP), BF16),
        compiler_params=_params(("parallel", "arbitrary"), need + (8 << 20)),
        name="fused_proj",
    )(_source_rows(), h, w_t, w_merge, col_scale)


def _misc_kernel(h_ref, w_ref, bf_ref, cos_ref, sin_ref, kr_ref, ccol_ref, crow_ref, carry_ref):
    t = pl.program_id(1)
    acc = lax.dot_general(h_ref[...], w_ref[...], (((1,), (1,)), ((), ())),
                          preferred_element_type=F32)
    kr_ref[...] = _rope(acc, cos_ref[...], sin_ref[...]).astype(kr_ref.dtype)

    xx = acc + bf_ref[...]
    log_f = jnp.minimum(xx, 0.0) - jnp.log1p(jnp.exp(-jnp.abs(xx)))
    hi = log_f.astype(BF16)
    r1 = log_f - hi.astype(F32)
    mid = r1.astype(BF16)
    lo = (r1 - mid.astype(F32)).astype(BF16)
    tm = acc.shape[0]
    ri = lax.broadcasted_iota(jnp.int32, (tm, tm), 0)
    ci = lax.broadcasted_iota(jnp.int32, (tm, tm), 1)
    tri = jnp.where(ri >= ci, 1.0, 0.0).astype(BF16)
    cs = (jnp.dot(tri, hi, preferred_element_type=F32)
          + jnp.dot(tri, mid, preferred_element_type=F32)
          + jnp.dot(tri, lo, preferred_element_type=F32))

    @pl.when(t == 0)
    def _():
        carry_ref[...] = jnp.zeros_like(carry_ref)

    cs = cs + carry_ref[...]
    carry_ref[...] = cs[tm - 1:tm, :]
    cs2 = cs * LOG2E
    ccol_ref[...] = cs2
    crow_ref[0] = cs2.T


def _misc_proj(h, w_misc, layer, bf_row, cos_t, sin_t, batch, seq):
    t = h.shape[0]
    tm = 512
    nt = seq // tm
    need = 2 * (tm * D * 2 + D * LANES * 2 + 5 * tm * LANES * 4) + 4 * tm * tm * 4
    return pl.pallas_call(
        _misc_kernel,
        grid=(batch, nt),
        in_specs=[pl.BlockSpec((tm, D), lambda b, i: (b * nt + i, 0)),
                  pl.BlockSpec((None, LANES, D), lambda b, i: (layer, 0, 0)),
                  pl.BlockSpec((1, LANES), lambda b, i: (0, 0)),
                  pl.BlockSpec((tm, LANES), lambda b, i: (b * nt + i, 0)),
                  pl.BlockSpec((tm, LANES), lambda b, i: (b * nt + i, 0))],
        out_specs=[pl.BlockSpec((tm, LANES), lambda b, i: (b * nt + i, 0)),
                   pl.BlockSpec((tm, LANES), lambda b, i: (b * nt + i, 0)),
                   pl.BlockSpec((1, LANES, tm), lambda b, i: (b, 0, i))],
        out_shape=[jax.ShapeDtypeStruct((t, LANES), BF16),
                   jax.ShapeDtypeStruct((t, LANES), F32),
                   jax.ShapeDtypeStruct((batch, LANES, seq), F32)],
        scratch_shapes=[pltpu.VMEM((1, LANES), F32)],
        compiler_params=_params(("arbitrary", "arbitrary"), need + (8 << 20)),
        name="misc_proj",
    )(h, w_misc, bf_row, cos_t, sin_t)


def _latent_kernel(*refs, rope, out_scale):
    if rope:
        x_ref, g_ref, w_ref, cos_ref, sin_ref, o_ref, xn_ref = refs
    else:
        x_ref, g_ref, w_ref, o_ref, xn_ref = refs

    @pl.when(pl.program_id(1) == 0)
    def _():
        x = x_ref[...].astype(F32)
        r = lax.rsqrt(jnp.mean(x * x, axis=-1, keepdims=True) + NORM_EPS)
        xn_ref[...] = (x * r * g_ref[...]).astype(xn_ref.dtype)

    def rotary(acc, rows):
        cos = cos_ref[rows, :]
        sin = sin_ref[rows, :]
        pieces = []
        for hh in range(acc.shape[1] // A_QPAD):
            base = hh * A_QPAD
            pieces.append(acc[:, base:base + A_NOPE])
            pieces.append(_rope(acc[:, base + A_NOPE:base + A_QPAD], cos, sin))
        return jnp.concatenate(pieces, axis=1) * out_scale

    _chunked_dot(xn_ref, w_ref, o_ref, rotary if rope else (lambda acc, rows: acc))


def _latent_proj(proj, col_block, k, g, w, tables, out_scale):
    t = proj.shape[0]
    n = w.shape[1]
    tm, tn = 1024, 1024
    rope = tables is not None
    kern = functools.partial(_latent_kernel, rope=rope, out_scale=out_scale)
    in_specs = [pl.BlockSpec((tm, k), lambda i, j: (i, col_block)),
                pl.BlockSpec((1, k), lambda i, j: (0, 0)),
                pl.BlockSpec((k, tn), lambda i, j: (0, j))]
    args = [proj, g.reshape(1, k), w]
    if rope:
        in_specs += [pl.BlockSpec((tm, LANES), lambda i, j: (i, 0))] * 2
        args += list(tables)
    need = 2 * (tm * k * 2 + k * tn * 2 + tm * tn * 2 + 2 * tm * LANES * 4) + tm * k * 2 + 4 * tm * tn * 4
    return pl.pallas_call(
        kern,
        grid=(t // tm, n // tn),
        in_specs=in_specs,
        out_specs=pl.BlockSpec((tm, tn), lambda i, j: (i, j)),
        out_shape=jax.ShapeDtypeStruct((t, n), BF16),
        scratch_shapes=[pltpu.VMEM((tm, k), BF16)],
        compiler_params=_params(("parallel", "arbitrary"), need + (8 << 20)),
        name="latent_q" if rope else "latent_kv",
    )(*args)


def _causal_attn_kernel(*refs, tq, fox, heads):
    if fox:
        q_ref, k_ref, v_ref, z_ref, ccol_ref, crow_ref, o_ref, vaug_ref = refs
    else:
        q_ref, kn_ref, kr_ref, v_ref, z_ref, o_ref, kcat_ref, vaug_ref = refs
    seq = q_ref.shape[0]
    dk = q_ref.shape[1] // heads
    lane = lax.broadcasted_iota(jnp.int32, (seq, LANES), 1)
    ones_col = jnp.where(lane == 0, 1.0, 0.0).astype(BF16)
    keys, cq_all, ck_all = [], [], []
    for hh in range(heads):
        cols = slice(hh * LANES, (hh + 1) * LANES)
        vaug_ref[hh, :, 0:LANES] = v_ref[:, cols]
        vaug_ref[hh, :, LANES:2 * LANES] = ones_col
        if fox:
            keys.append(k_ref.at[:, cols])
            h = pl.program_id(1) * heads + hh
            cq_all.append(jnp.sum(jnp.where(lane == F_LANE + h, ccol_ref[...], 0.0),
                                  axis=1, keepdims=True))
            ck_all.append(crow_ref[0, pl.ds(F_LANE + h, 1), :])
        else:
            kcat_ref[hh, :, 0:LANES] = kn_ref[:, cols]
            kcat_ref[hh, :, LANES:2 * LANES] = kr_ref[...]
            keys.append(kcat_ref.at[hh])
    row = lax.broadcasted_iota(jnp.int32, (tq, tq), 0)
    col = lax.broadcasted_iota(jnp.int32, (tq, tq), 1)
    causal = row >= col
    dims = (((1,), (1,)), ((), ()))

    def logits(item):
        hh, c = item
        lo, hi = c * tq, (c + 1) * tq
        q = q_ref[lo:hi, hh * dk:(hh + 1) * dk]
        t_d = lax.dot_general(q, keys[hh][lo:hi, :], dims, preferred_element_type=F32)
        if fox:
            t_d = t_d - ck_all[hh][:, lo:hi]
        t_d = jnp.where(causal, t_d, MASKED)
        t_o = None
        if c:
            t_o = lax.dot_general(q, keys[hh][0:lo, :], dims, preferred_element_type=F32)
            if fox:
                t_o = t_o - ck_all[hh][:, 0:lo]
        return t_d, t_o

    nq = seq // tq
    order = [(hh, c) for hh in range(heads) for c in range(nq - 1, -1, -1)]
    queue = [logits(item) for item in order[:LOOKAHEAD]]
    for i, (hh, c) in enumerate(order):
        lo, hi = c * tq, (c + 1) * tq
        cols = slice(hh * LANES, (hh + 1) * LANES)
        t_d, t_o = queue.pop(0)
        if i + LOOKAHEAD < len(order):
            queue.append(logits(order[i + LOOKAHEAD]))
        mt = jnp.max(t_d, axis=1, keepdims=True)
        if c:
            mt = jnp.maximum(mt, jnp.max(t_o, axis=1, keepdims=True))
        if fox:
            cq = cq_all[hh][lo:hi, :]
            shift = (mt + cq) - cq
        else:
            shift = mt
        acc = jnp.dot(jnp.exp2(t_d - shift).astype(BF16), vaug_ref[hh, lo:hi, :],
                      preferred_element_type=F32)
        if c:
            acc = acc + jnp.dot(jnp.exp2(t_o - shift).astype(BF16), vaug_ref[hh, 0:lo, :],
                                preferred_element_type=F32)
        inv_l = 1.0 / acc[:, LANES:LANES + 1]
        o_ref[lo:hi, cols] = (acc[:, 0:LANES] * inv_l
                              * z_ref[lo:hi, cols].astype(F32)).astype(o_ref.dtype)


ATTN_TQ = 256
LOOKAHEAD = 2
ATTN_HEADS = 4


def _attention_a(q_a, kv_a, k_rope, proj, batch, seq):
    t = q_a.shape[0]
    hs = ATTN_HEADS
    kern = functools.partial(_causal_attn_kernel, tq=ATTN_TQ, fox=False, heads=hs)
    z_col = CH_AZ * CHUNK // LANES
    need = 2 * (hs * seq * A_QPAD * 2 + (4 * hs + 1) * seq * LANES * 2) + 2 * hs * seq * A_QPAD * 2
    wide = pl.BlockSpec((seq, hs * LANES), lambda b, h: (b, h))
    return pl.pallas_call(
        kern,
        grid=(batch, HA // hs),
        in_specs=[pl.BlockSpec((seq, hs * A_QPAD), lambda b, h: (b, h)),
                  wide,
                  pl.BlockSpec((seq, LANES), lambda b, h: (b, 0)),
                  pl.BlockSpec((seq, hs * LANES), lambda b, h: (b, HA // hs + h)),
                  pl.BlockSpec((seq, hs * LANES), lambda b, h: (b, z_col // hs + h))],
        out_specs=wide,
        out_shape=jax.ShapeDtypeStruct((t, HA * LANES), BF16),
        scratch_shapes=[pltpu.VMEM((hs, seq, A_QPAD), BF16), pltpu.VMEM((hs, seq, 2 * LANES), BF16)],
        compiler_params=_params(("parallel", "arbitrary"), need + (24 << 20)),
        name="attn_mla",
    )(q_a, kv_a, k_rope, kv_a, proj)


def _attention_c(proj, ccol, crow, batch, seq):
    t = proj.shape[0]
    hs = ATTN_HEADS
    kern = functools.partial(_causal_attn_kernel, tq=ATTN_TQ, fox=True, heads=hs)
    q_col, k_col, v_col, z_col = (c * CHUNK // (hs * LANES) for c in (CH_CQ, CH_CK, CH_CV, CH_CZ))
    need = 2 * (5 * hs * seq * LANES * 2 + 2 * seq * LANES * 4) + hs * seq * 2 * LANES * 2

    def head_cols(first):
        return pl.BlockSpec((seq, hs * LANES), lambda b, h: (b, first + h))

    return pl.pallas_call(
        kern,
        grid=(batch, HC // hs),
        in_specs=[head_cols(q_col), head_cols(k_col), head_cols(v_col), head_cols(z_col),
                  pl.BlockSpec((seq, LANES), lambda b, h: (b, 0)),
                  pl.BlockSpec((1, LANES, seq), lambda b, h: (b, 0, 0))],
        out_specs=head_cols(0),
        out_shape=jax.ShapeDtypeStruct((t, HC * DC), BF16),
        scratch_shapes=[pltpu.VMEM((hs, seq, 2 * LANES), BF16)],
        compiler_params=_params(("parallel", "arbitrary"), need + (24 << 20)),
        name="attn_fox",
    )(proj, proj, proj, proj, ccol, crow)


def _swa_kernel(q_ref, kvp_ref, kvc_ref, z_ref, bias_ref, sink_ref, o_ref, vaug_ref):
    n = pl.program_id(1)
    blk = SEQ_BLOCK
    band = 2 * blk
    rows = (GB // 2) * blk
    kv = jnp.concatenate([kvp_ref[...], kvc_ref[...]], axis=0).astype(F32)
    lane = lax.broadcasted_iota(jnp.int32, (band, LANES), 1)
    low = lane < DB

    def split(two_heads):
        lo = jnp.where(low, two_heads, 0.0)
        hi = jnp.where(low, 0.0, two_heads)
        return (lo, pltpu.roll(lo, DB, 1)), (pltpu.roll(hi, DB, 1), hi)

    k_heads = split(kv[:, 0:LANES]) + split(kv[:, LANES:2 * LANES])
    v_heads = split(kv[:, 2 * LANES:3 * LANES]) + split(kv[:, 3 * LANES:4 * LANES])

    r = lax.broadcasted_iota(jnp.int32, (rows, band), 0) % blk
    c = lax.broadcasted_iota(jnp.int32, (rows, band), 1)
    dist = r + blk - c
    first_key = jnp.where(n > 0, 0, blk)
    valid = (dist >= 0) & (dist < blk) & (c >= first_key)

    def logits(kh):
        k_cat = jnp.concatenate(k_heads[kh], axis=0).astype(BF16)
        q4 = jnp.concatenate(
            [q_ref[:, (kh * (GB // 2) + jj) * LANES:(kh * (GB // 2) + jj + 1) * LANES]
             for jj in range(GB // 2)], axis=0)
        s = lax.dot_general(q4, k_cat, (((1,), (1,)), ((), ())), preferred_element_type=F32)
        return s + bias_ref[kh]

    r2 = lax.broadcasted_iota(jnp.int32, (2 * band, LANES), 0)
    l2 = lax.broadcasted_iota(jnp.int32, (2 * band, LANES), 1)
    ones_cols = jnp.where(((l2 == 0) & (r2 < band)) | ((l2 == 1) & (r2 >= band)), 1.0, 0.0)
    for kh in range(KVB):
        vaug_ref[kh, :, LANES:2 * LANES] = ones_cols.astype(BF16)
        vaug_ref[kh, :, 0:LANES] = jnp.concatenate(v_heads[kh], axis=0).astype(BF16)
    even_lanes = lax.broadcasted_iota(jnp.int32, (rows, LANES), 1) < DB

    kvs = range(KVB)
    s_all = [logits(kh) for kh in kvs]
    se = [[jnp.where(valid, s_all[kh][:, e * band:(e + 1) * band], MASKED) for e in range(2)]
          for kh in kvs]
    sink = [[sink_ref[kh, :, e:e + 1] for e in range(2)] for kh in kvs]
    m = [[jnp.maximum(jnp.max(se[kh][e], axis=1, keepdims=True), sink[kh][e]) for e in range(2)]
         for kh in kvs]
    p = [jnp.concatenate([jnp.exp2(se[kh][e] - m[kh][e]).astype(BF16) for e in range(2)], axis=1)
         for kh in kvs]
    pv = [jnp.dot(p[kh], vaug_ref[kh], preferred_element_type=F32) for kh in kvs]
    for kh in kvs:
        inv_even = 1.0 / (pv[kh][:, LANES:LANES + 1] + jnp.exp2(sink[kh][0] - m[kh][0]))
        inv_odd = 1.0 / (pv[kh][:, LANES + 1:LANES + 2] + jnp.exp2(sink[kh][1] - m[kh][1]))
        o = pv[kh][:, 0:LANES] * jnp.where(even_lanes, inv_even, inv_odd)
        for jj in range(GB // 2):
            c0 = (kh * (GB // 2) + jj) * LANES
            o_ref[:, c0:c0 + LANES] = (o[jj * blk:(jj + 1) * blk, :]
                                       * z_ref[:, c0:c0 + LANES].astype(F32)).astype(o_ref.dtype)


def _attention_b(proj, bias, sink_cols, batch, seq):
    t = proj.shape[0]
    blk = SEQ_BLOCK
    nb = seq // blk
    width = HB * DB
    q_col = CH_BQ * CHUNK // width
    kv_col = CH_BKV
    z_col = CH_BZ * CHUNK // width
    assert CH_BQ * CHUNK % width == 0 and CH_BZ * CHUNK % width == 0
    need = 2 * (3 * blk * width * 2 + 2 * blk * CHUNK * 2 + bias.size * 4 + sink_cols.size * 4)
    return pl.pallas_call(
        _swa_kernel,
        grid=(batch, nb),
        in_specs=[pl.BlockSpec((blk, width), lambda b, n: (b * nb + n, q_col)),
                  pl.BlockSpec((blk, CHUNK), lambda b, n: (b * nb + jnp.maximum(n - 1, 0), kv_col)),
                  pl.BlockSpec((blk, CHUNK), lambda b, n: (b * nb + n, kv_col)),
                  pl.BlockSpec((blk, width), lambda b, n: (b * nb + n, z_col)),
                  pl.BlockSpec(bias.shape, lambda b, n: (0, 0, 0)),
                  pl.BlockSpec(sink_cols.shape, lambda b, n: (0, 0, 0))],
        out_specs=pl.BlockSpec((blk, width), lambda b, n: (b * nb + n, 0)),
        out_shape=jax.ShapeDtypeStruct((t, width), BF16),
        scratch_shapes=[pltpu.VMEM((KVB, 4 * blk, 2 * LANES), BF16)],
        compiler_params=_params(("parallel", "arbitrary"), need + (24 << 20)),
        name="attn_swa",
    )(proj, proj, proj, proj, bias, sink_cols)


def _merge_kernel(oa_ref, ob_ref, oc_ref, wa_ref, wb_ref, wc_ref, ga_ref, gb_ref, gc_ref, y_ref):
    branches = ((oa_ref, wa_ref, ga_ref), (ob_ref, wb_ref, gb_ref), (oc_ref, wc_ref, gc_ref))
    for r in range(y_ref.shape[0] // ROW_CHUNK):
        rows = slice(r * ROW_CHUNK, (r + 1) * ROW_CHUNK)
        y = None
        for o_ref, w_ref, g_ref in branches:
            term = g_ref[rows, :].astype(F32) * jnp.dot(o_ref[rows, :], w_ref[...],
                                                        preferred_element_type=F32)
            y = term if y is None else y + term
        y_ref[rows, :] = y.astype(y_ref.dtype)


def _merge_proj(oa, ob, oc, wa, wb, wc, layer, proj):
    t, k = oa.shape
    tm, tn = 1024, CHUNK
    gcols = D // tn
    o_spec = pl.BlockSpec((tm, k), lambda i, j: (i, 0))
    w_spec = pl.BlockSpec((None, k, tn), lambda i, j: (layer, 0, j))

    def gate_spec(branch):
        return pl.BlockSpec((tm, tn), lambda i, j: (i, CH_GATE + branch * gcols + j))

    need = 2 * (3 * tm * k * 2 + 3 * k * tn * 2 + 4 * tm * tn * 2) + 4 * tm * tn * 4
    return pl.pallas_call(
        _merge_kernel,
        grid=(t // tm, D // tn),
        in_specs=[o_spec, o_spec, o_spec, w_spec, w_spec, w_spec,
                  gate_spec(0), gate_spec(1), gate_spec(2)],
        out_specs=pl.BlockSpec((tm, tn), lambda i, j: (i, j)),
        out_shape=jax.ShapeDtypeStruct((t, D), BF16),
        compiler_params=_params(("parallel", "arbitrary"), need + (8 << 20)),
        name="merge_proj",
    )(oa, ob, oc, wa, wb, wc, proj, proj, proj)


def _out_kernel(y_ref, w_ref, x_ref, g_ref, o_ref, acc_ref, ssq_ref):
    i = pl.program_id(0)
    j = pl.program_id(1)
    slot = i % 2
    prev = 1 - slot

    @pl.when((i == 0) & (j == 0))
    def _():
        acc_ref[...] = jnp.zeros_like(acc_ref)
        ssq_ref[...] = jnp.zeros_like(ssq_ref)

    def finish():
        rs = lax.rsqrt(ssq_ref[prev] * (1.0 / D) + NORM_EPS)
        o_ref[...] = x_ref[...] + acc_ref[j] * rs * g_ref[...]

    @pl.when(i < pl.num_programs(0) - 1)
    def _():
        finish()
        ss = jnp.where(j == 0, 0.0, ssq_ref[slot])
        for r in range(y_ref.shape[0] // ROW_CHUNK):
            rows = slice(r * ROW_CHUNK, (r + 1) * ROW_CHUNK)
            acc = jnp.dot(y_ref[rows, :], w_ref[...], preferred_element_type=F32)
            acc_ref[j, rows, :] = acc
            ssq_ref[slot, rows, :] = ss[rows, :] + jnp.sum(acc * acc, axis=1, keepdims=True)

    @pl.when(i == pl.num_programs(0) - 1)
    def _():
        finish()


def _out_proj(y, w_o, layer, x2, g):
    t = y.shape[0]
    tm, tn = 1024, CHUNK
    ni, nj = t // tm, D // tn
    need = 2 * (tm * D * 2 + D * tn * 2 + 2 * tm * tn * 4) + tm * D * 4 + 4 * tm * tn * 4
    return pl.pallas_call(
        _out_kernel,
        grid=(ni + 1, nj),
        in_specs=[pl.BlockSpec((tm, D), lambda i, j: (jnp.minimum(i, ni - 1), 0)),
                  pl.BlockSpec((None, D, tn), lambda i, j: (layer, 0, j)),
                  pl.BlockSpec((tm, tn), lambda i, j: (jnp.maximum(i - 1, 0), j)),
                  pl.BlockSpec((1, tn), lambda i, j: (0, j))],
        out_specs=pl.BlockSpec((tm, tn), lambda i, j: (jnp.maximum(i - 1, 0), j)),
        out_shape=jax.ShapeDtypeStruct((t, D), F32),
        scratch_shapes=[pltpu.VMEM((nj, tm, tn), F32), pltpu.VMEM((2, tm, 1), F32)],
        compiler_params=_params(("arbitrary", "arbitrary"), need + (8 << 20)),
        name="out_proj",
    )(y, w_o, x2, g.reshape(1, D))


def _cast_kernel(x_ref, o_ref):
    o_ref[...] = x_ref[...].astype(o_ref.dtype)


def _to_bf16(w):
    nl, r, c = w.shape
    tr, tc = 512, 2048
    spec = pl.BlockSpec((None, tr, tc), lambda l, i, j: (l, i, j))
    return pl.pallas_call(
        _cast_kernel,
        grid=(nl, r // tr, c // tc),
        in_specs=[spec],
        out_specs=spec,
        out_shape=jax.ShapeDtypeStruct(w.shape, BF16),
        compiler_params=_params(("arbitrary",) * 3, 6 * tr * tc * 4),
        name="cast_bf16",
    )(w)


def _transposed_w_in(w_in):
    w_t = jnp.swapaxes(w_in, 1, 2).astype(BF16)

    def rows(name):
        return w_t[:, IN_START[name]:IN_START[name] + IN_WIDTH[name], :]

    pad = jnp.zeros((w_in.shape[0], LANES - F_LANE - HC, D), BF16)
    misc = jnp.concatenate([rows("a_kr"), rows("c_f"), pad], axis=1)
    return w_t, misc


def _layer_weights(w_uq, w_uk, w_uv, b_f):
    col_scale = jnp.ones((NP,), F32)
    col_scale = col_scale.at[CH_BQ * CHUNK:CH_CQ * CHUNK].set(DB ** -0.5 * LOG2E)
    col_scale = col_scale.at[CH_CQ * CHUNK:CH_CK * CHUNK].set(DC ** -0.5 * LOG2E)
    col_scale = col_scale.reshape(1, NP)
    bf_row = jnp.concatenate([jnp.zeros((F_LANE,), F32), b_f.astype(F32),
                              jnp.zeros((LANES - F_LANE - HC,), F32)]).reshape(1, LANES)
    wq = jnp.pad(w_uq, ((0, 0), (0, 0), (0, A_QPAD - A_NOPE - A_ROPE)))
    wq = wq.reshape(A_QR, HA * A_QPAD).astype(BF16)
    wkv = jnp.concatenate([w_uk.reshape(A_KVR, HA * A_NOPE), w_uv.reshape(A_KVR, HA * A_NOPE)],
                          axis=1).astype(BF16)
    return col_scale, bf_row, wq, wkv


def _sink_columns(sinks):
    s = (sinks.astype(F32) * LOG2E).reshape(KVB, GB // 2, 1, 2)
    s = jnp.broadcast_to(s, (KVB, GB // 2, SEQ_BLOCK, 2)).reshape(KVB, (GB // 2) * SEQ_BLOCK, 2)
    return jnp.pad(s, ((0, 0), (0, 0), (0, LANES - 2)))


def kernel(x, positions, rel_table, pre_norm, w_in, q_a_norm, kv_a_norm, w_uq, w_uk, w_uv,
           sinks, b_f, w_proj_a, w_proj_b, w_proj_c, w_merge, w_o, post_norm):
    batch, seq, _ = x.shape
    depth = w_in.shape[0]
    x2 = x.reshape(batch * seq, D)
    tables = _rope_tables(positions)
    bias = _rel_bias(rel_table)
    wm16, wo16 = _to_bf16(w_merge), w_o.astype(BF16)
    wa16, wb16, wc16 = (w.astype(BF16) for w in (w_proj_a, w_proj_b, w_proj_c))
    w_inr, w_misc = _transposed_w_in(w_in)
    for l in range(depth):
        col_scale, bf_row, wq, wkv = _layer_weights(w_uq[l], w_uk[l], w_uv[l], b_f[l])
        h = _prenorm(x2, pre_norm[l])
        proj = _fused_proj(h, w_inr, wm16, l, col_scale)
        k_rope, ccol, crow = _misc_proj(h, w_misc, l, bf_row, tables[0], tables[1], batch, seq)
        q_a = _latent_proj(proj, 0, A_QR, q_a_norm[l], wq, tables,
                           (A_NOPE + A_ROPE) ** -0.5 * LOG2E)
        kv_a = _latent_proj(proj, CH_ACKV, A_KVR, kv_a_norm[l], wkv, None, 1.0)
        o_a = _attention_a(q_a, kv_a, k_rope, proj, batch, seq)
        o_b = _attention_b(proj, bias, _sink_columns(sinks[l]), batch, seq)
        o_c = _attention_c(proj, ccol, crow, batch, seq)
        y = _merge_proj(o_a, o_b, o_c, wa16, wb16, wc16, l, proj)
        x2 = _out_proj(y, wo16, l, x2, post_norm[l])
    return x2.reshape(batch, seq, D)
```

```python
import functools
import math

import jax
import jax.numpy as jnp
from jax import lax
from jax.experimental import pallas as pl
from jax.experimental.pallas import tpu as pltpu

F32 = jnp.float32
BF16 = jnp.bfloat16

D = 4096
SEQ_BLOCK = 128
NORM_EPS = 1e-6
MASKED = -1e30
LOG2E = math.log2(math.e)
HA = 16
A_QR = 1536
A_KVR = 512
A_NOPE = 128
A_ROPE = 64
A_HALF = A_ROPE // 2
A_QPAD = 256
THETA = 10000.0
HB = 32
KVB = 4
DB = 64
GB = HB // KVB
HC = 16
DC = 128
N_BUCKETS = 32
MAX_DIST = 128

LANES = 128
CHUNK = 512

CH_ACQ, CH_ACKV, CH_BQ, CH_CQ, CH_CK, CH_CV = 0, 3, 4, 8, 12, 16
CH_BZ, CH_AZ, CH_CZ = 20, 24, 28
CH_GATE = 32
CH_BKV = 56
N_CHUNKS = 57
NP = N_CHUNKS * CHUNK
BF16_SUBLANES = 16

IN_SIZES = (A_QR, A_KVR, A_ROPE, HA * A_NOPE, HB * DB, KVB * DB, KVB * DB, HB * DB,
            HC * DC, HC * DC, HC * DC, HC, HC * DC)
IN_NAMES = ("a_cq", "a_ckv", "a_kr", "a_z", "b_q", "b_k", "b_v", "b_z", "c_q", "c_k", "c_v", "c_f", "c_z")
IN_START = {name: sum(IN_SIZES[:i]) for i, name in enumerate(IN_NAMES)}
IN_WIDTH = dict(zip(IN_NAMES, IN_SIZES))
F_LANE = 64

VMEM_CAP = 60 * 1024 * 1024


def _vmem(nbytes):
    return int(min(VMEM_CAP, nbytes))


def _params(sem, nbytes):
    return pltpu.CompilerParams(dimension_semantics=sem, vmem_limit_bytes=_vmem(nbytes))


def _rope_table_kernel(pos_ref, freq_ref, cos_ref, sin_ref):
    ang = pos_ref[...].astype(F32) * freq_ref[...]
    lane = lax.broadcasted_iota(jnp.int32, ang.shape, 1)
    live = lane < A_ROPE
    s = jnp.sin(ang)
    cos_ref[...] = jnp.where(live, jnp.cos(ang), 0.0)
    sin_ref[...] = jnp.where(live, jnp.where(lane < A_HALF, -s, s), 0.0)


def _rope_tables(positions):
    t = positions.size
    tm = 2048
    half = jnp.arange(A_HALF, dtype=F32)
    inv_freq = THETA ** (-half / A_HALF)
    freq = jnp.concatenate([inv_freq, inv_freq, jnp.zeros((LANES - A_ROPE,), F32)]).reshape(1, LANES)
    pos = positions.reshape(t, 1)
    return pl.pallas_call(
        _rope_table_kernel,
        grid=(t // tm,),
        in_specs=[pl.BlockSpec((tm, 1), lambda i: (i, 0)),
                  pl.BlockSpec((1, LANES), lambda i: (0, 0))],
        out_specs=[pl.BlockSpec((tm, LANES), lambda i: (i, 0)),
                   pl.BlockSpec((tm, LANES), lambda i: (i, 0))],
        out_shape=[jax.ShapeDtypeStruct((t, LANES), F32)] * 2,
        compiler_params=_params(("arbitrary",), 32 << 20),
        name="rope_tables",
    )(pos, freq)


def _rope(t, cos, sin):
    lane = lax.broadcasted_iota(jnp.int32, t.shape, 1)
    partner = jnp.where(lane < A_HALF,
                        pltpu.roll(t, LANES - A_HALF, 1),
                        pltpu.roll(t, A_HALF, 1))
    return t * cos + partner * sin


def _relbias_kernel(table_ref, o_ref):
    h = pl.program_id(0)
    qi = lax.broadcasted_iota(jnp.int32, (SEQ_BLOCK, 2 * SEQ_BLOCK), 0)
    si = lax.broadcasted_iota(jnp.int32, (SEQ_BLOCK, 2 * SEQ_BLOCK), 1)
    dist = qi + SEQ_BLOCK - si
    max_exact = N_BUCKETS // 2
    d = jnp.maximum(dist, 0)
    large = max_exact + (jnp.log(jnp.maximum(d, 1).astype(F32) / max_exact)
                         / math.log(MAX_DIST / max_exact)
                         * (N_BUCKETS - max_exact)).astype(jnp.int32)
    large = jnp.minimum(large, N_BUCKETS - 1)
    bucket = jnp.where(d < max_exact, d, large)
    acc = jnp.zeros(bucket.shape, F32)
    for b in range(N_BUCKETS):
        acc = jnp.where(bucket == b, table_ref[b, h], acc)
    o_ref[0] = acc * LOG2E


def _rel_bias(rel_table):
    def out_map(h):
        return (h // GB, (h % GB) // 2, h % 2)
    return pl.pallas_call(
        _relbias_kernel,
        grid=(HB,),
        in_specs=[pl.BlockSpec(memory_space=pltpu.SMEM)],
        out_specs=pl.BlockSpec((1, SEQ_BLOCK, 2 * SEQ_BLOCK), out_map),
        out_shape=jax.ShapeDtypeStruct((KVB, (GB // 2) * SEQ_BLOCK, 4 * SEQ_BLOCK), F32),
        compiler_params=_params(("arbitrary",), 16 << 20),
        name="rel_bias",
    )(rel_table)


def _prenorm_kernel(x_ref, g_ref, o_ref):
    x = x_ref[...]
    r = lax.rsqrt(jnp.mean(x * x, axis=-1, keepdims=True) + NORM_EPS)
    o_ref[...] = (x * r * g_ref[...]).astype(o_ref.dtype)


def _prenorm(x2, g):
    t = x2.shape[0]
    tm = 256
    return pl.pallas_call(
        _prenorm_kernel,
        grid=(t // tm,),
        in_specs=[pl.BlockSpec((tm, D), lambda i: (i, 0)),
                  pl.BlockSpec((1, D), lambda i: (0, 0))],
        out_specs=pl.BlockSpec((tm, D), lambda i: (i, 0)),
        out_shape=jax.ShapeDtypeStruct((t, D), BF16),
        compiler_params=_params(("arbitrary",), 32 << 20),
        name="prenorm",
    )(x2, g.reshape(1, D))


ROW_CHUNK = 256


def _chunked_dot(x_ref, w_ref, o_ref, epilogue, trans_w=False):
    dims = (((1,), (1,) if trans_w else (0,)), ((), ()))
    for r in range(x_ref.shape[0] // ROW_CHUNK):
        rows = slice(r * ROW_CHUNK, (r + 1) * ROW_CHUNK)
        acc = lax.dot_general(x_ref[rows, :], w_ref[...], dims, preferred_element_type=F32)
        o_ref[rows, :] = epilogue(acc, rows).astype(o_ref.dtype)


def _proj_kernel(rows_ref, h_ref, wi_ref, wm_ref, cs_ref, o_ref, *,
                 silu_start, gate_start, gate_end):
    del rows_ref
    j = pl.program_id(1)
    wi = wi_ref.at[0]

    @pl.when((j < silu_start) | (j >= gate_end))
    def _():
        _chunked_dot(h_ref, wi, o_ref, lambda a, rows: a * cs_ref[...], trans_w=True)

    @pl.when((j >= silu_start) & (j < gate_start))
    def _():
        _chunked_dot(h_ref, wi, o_ref, lambda a, rows: a * jax.nn.sigmoid(a), trans_w=True)

    @pl.when((j >= gate_start) & (j < gate_end))
    def _():
        _chunked_dot(h_ref, wm_ref, o_ref, lambda a, rows: jax.nn.sigmoid(a))


def _source_rows():
    starts = {CH_ACQ: "a_cq", CH_ACKV: "a_ckv", CH_BQ: "b_q", CH_CQ: "c_q", CH_CK: "c_k",
              CH_CV: "c_v", CH_BZ: "b_z", CH_AZ: "a_z", CH_CZ: "c_z", CH_BKV: "b_k"}
    rows, row = [], 0
    for j in range(N_CHUNKS):
        if CH_GATE <= j < CH_BKV:
            rows.append(rows[-1])
            continue
        if j in starts:
            row = IN_START[starts[j]]
        rows.append(row)
        row += CHUNK
    return jnp.asarray(rows, jnp.int32)


def _fused_proj(h, w_t, w_merge, layer, col_scale):
    t = h.shape[0]
    tm, tn = 2048, CHUNK
    n_gate = CH_BKV - CH_GATE
    kern = functools.partial(_proj_kernel, silu_start=CH_BZ, gate_start=CH_GATE, gate_end=CH_BKV)

    def wi_map(i, j, rows):
        return (layer, pl.multiple_of(rows[j], BF16_SUBLANES), 0)

    def wm_map(i, j, rows):
        return (layer, 0, jnp.clip(j - CH_GATE, 0, n_gate - 1))

    need = 2 * (tm * D * 2 + 2 * D * tn * 2 + tm * tn * 2) + 4 * tm * tn * 4
    grid_spec = pltpu.PrefetchScalarGridSpec(
        num_scalar_prefetch=1,
        grid=(t // tm, NP // tn),
        in_specs=[pl.BlockSpec((tm, D), lambda i, j, rows: (i, 0)),
                  pl.BlockSpec((pl.Element(1), pl.Element(tn), pl.Element(D)), wi_map),
                  pl.BlockSpec((None, D, tn), wm_map),
                  pl.BlockSpec((1, tn), lambda i, j, rows: (0, j))],
        out_specs=pl.BlockSpec((tm, tn), lambda i, j, rows: (i, j)),
    )
    return pl.pallas_call(
        kern,
        grid_spec=grid_spec,
        out_shape=jax.ShapeDtypeStruct((t, NP), BF16),
        compiler_params=_params(("parallel", "arbitrary"), need + (8 << 20)),
        name="fused_proj",
    )(_source_rows(), h, w_t, w_merge, col_scale)


def _misc_kernel(h_ref, w_ref, bf_ref, cos_ref, sin_ref, kr_ref, ccol_ref, crow_ref, carry_ref):
    t = pl.program_id(1)
    acc = lax.dot_general(h_ref[...], w_ref[...], (((1,), (1,)), ((), ())),
                          preferred_element_type=F32)
    kr_ref[...] = _rope(acc, cos_ref[...], sin_ref[...]).astype(kr_ref.dtype)

    xx = acc + bf_ref[...]
    log_f = jnp.minimum(xx, 0.0) - jnp.log1p(jnp.exp(-jnp.abs(xx)))
    hi = log_f.astype(BF16)
    r1 = log_f - hi.astype(F32)
    mid = r1.astype(BF16)
    lo = (r1 - mid.astype(F32)).astype(BF16)
    tm = acc.shape[0]
    ri = lax.broadcasted_iota(jnp.int32, (tm, tm), 0)
    ci = lax.broadcasted_iota(jnp.int32, (tm, tm), 1)
    tri = jnp.where(ri >= ci, 1.0, 0.0).astype(BF16)
    cs = (jnp.dot(tri, hi, preferred_element_type=F32)
          + jnp.dot(tri, mid, preferred_element_type=F32)
          + jnp.dot(tri, lo, preferred_element_type=F32))

    @pl.when(t == 0)
    def _():
        carry_ref[...] = jnp.zeros_like(carry_ref)

    cs = cs + carry_ref[...]
    carry_ref[...] = cs[tm - 1:tm, :]
    cs2 = cs * LOG2E
    ccol_ref[...] = cs2
    crow_ref[0] = cs2.T


def _misc_proj(h, w_misc, layer, bf_row, cos_t, sin_t, batch, seq):
    t = h.shape[0]
    tm = 512
    nt = seq // tm
    need = 2 * (tm * D * 2 + D * LANES * 2 + 5 * tm * LANES * 4) + 4 * tm * tm * 4
    return pl.pallas_call(
        _misc_kernel,
        grid=(batch, nt),
        in_specs=[pl.BlockSpec((tm, D), lambda b, i: (b * nt + i, 0)),
                  pl.BlockSpec((None, LANES, D), lambda b, i: (layer, 0, 0)),
                  pl.BlockSpec((1, LANES), lambda b, i: (0, 0)),
                  pl.BlockSpec((tm, LANES), lambda b, i: (b * nt + i, 0)),
                  pl.BlockSpec((tm, LANES), lambda b, i: (b * nt + i, 0))],
        out_specs=[pl.BlockSpec((tm, LANES), lambda b, i: (b * nt + i, 0)),
                   pl.BlockSpec((tm, LANES), lambda b, i: (b * nt + i, 0)),
                   pl.BlockSpec((1, LANES, tm), lambda b, i: (b, 0, i))],
        out_shape=[jax.ShapeDtypeStruct((t, LANES), BF16),
                   jax.ShapeDtypeStruct((t, LANES), F32),
                   jax.ShapeDtypeStruct((batch, LANES, seq), F32)],
        scratch_shapes=[pltpu.VMEM((1, LANES), F32)],
        compiler_params=_params(("arbitrary", "arbitrary"), need + (8 << 20)),
        name="misc_proj",
    )(h, w_misc, bf_row, cos_t, sin_t)


def _latent_kernel(*refs, rope, out_scale):
    if rope:
        x_ref, g_ref, w_ref, cos_ref, sin_ref, o_ref, xn_ref = refs
    else:
        x_ref, g_ref, w_ref, o_ref, xn_ref = refs

    @pl.when(pl.program_id(1) == 0)
    def _():
        x = x_ref[...].astype(F32)
        r = lax.rsqrt(jnp.mean(x * x, axis=-1, keepdims=True) + NORM_EPS)
        xn_ref[...] = (x * r * g_ref[...]).astype(xn_ref.dtype)

    def rotary(acc, rows):
        cos = cos_ref[rows, :]
        sin = sin_ref[rows, :]
        pieces = []
        for hh in range(acc.shape[1] // A_QPAD):
            base = hh * A_QPAD
            pieces.append(acc[:, base:base + A_NOPE])
            pieces.append(_rope(acc[:, base + A_NOPE:base + A_QPAD], cos, sin))
        return jnp.concatenate(pieces, axis=1) * out_scale

    _chunked_dot(xn_ref, w_ref, o_ref, rotary if rope else (lambda acc, rows: acc))


def _latent_proj(proj, col_block, k, g, w, tables, out_scale):
    t = proj.shape[0]
    n = w.shape[1]
    tm, tn = 1024, 1024
    rope = tables is not None
    kern = functools.partial(_latent_kernel, rope=rope, out_scale=out_scale)
    in_specs = [pl.BlockSpec((tm, k), lambda i, j: (i, col_block)),
                pl.BlockSpec((1, k), lambda i, j: (0, 0)),
                pl.BlockSpec((k, tn), lambda i, j: (0, j))]
    args = [proj, g.reshape(1, k), w]
    if rope:
        in_specs += [pl.BlockSpec((tm, LANES), lambda i, j: (i, 0))] * 2
        args += list(tables)
    need = 2 * (tm * k * 2 + k * tn * 2 + tm * tn * 2 + 2 * tm * LANES * 4) + tm * k * 2 + 4 * tm * tn * 4
    return pl.pallas_call(
        kern,
        grid=(t // tm, n // tn),
        in_specs=in_specs,
        out_specs=pl.BlockSpec((tm, tn), lambda i, j: (i, j)),
        out_shape=jax.ShapeDtypeStruct((t, n), BF16),
        scratch_shapes=[pltpu.VMEM((tm, k), BF16)],
        compiler_params=_params(("parallel", "arbitrary"), need + (8 << 20)),
        name="latent_q" if rope else "latent_kv",
    )(*args)


def _causal_attn_kernel(*refs, tq, fox, heads):
    if fox:
        q_ref, k_ref, v_ref, z_ref, ccol_ref, crow_ref, o_ref, vaug_ref = refs
    else:
        q_ref, kn_ref, kr_ref, v_ref, z_ref, o_ref, kcat_ref, vaug_ref = refs
    seq = q_ref.shape[0]
    dk = q_ref.shape[1] // heads
    lane = lax.broadcasted_iota(jnp.int32, (seq, LANES), 1)
    ones_col = jnp.where(lane == 0, 1.0, 0.0).astype(BF16)
    keys, cq_all, ck_all = [], [], []
    for hh in range(heads):
        cols = slice(hh * LANES, (hh + 1) * LANES)
        vaug_ref[hh, :, 0:LANES] = v_ref[:, cols]
        vaug_ref[hh, :, LANES:2 * LANES] = ones_col
        if fox:
            keys.append(k_ref.at[:, cols])
            h = pl.program_id(1) * heads + hh
            cq_all.append(jnp.sum(jnp.where(lane == F_LANE + h, ccol_ref[...], 0.0),
                                  axis=1, keepdims=True))
            ck_all.append(crow_ref[0, pl.ds(F_LANE + h, 1), :])
        else:
            kcat_ref[hh, :, 0:LANES] = kn_ref[:, cols]
            kcat_ref[hh, :, LANES:2 * LANES] = kr_ref[...]
            keys.append(kcat_ref.at[hh])
    row = lax.broadcasted_iota(jnp.int32, (tq, tq), 0)
    col = lax.broadcasted_iota(jnp.int32, (tq, tq), 1)
    causal = row >= col
    dims = (((1,), (1,)), ((), ()))

    def logits(item):
        hh, c = item
        lo, hi = c * tq, (c + 1) * tq
        q = q_ref[lo:hi, hh * dk:(hh + 1) * dk]
        t_d = lax.dot_general(q, keys[hh][lo:hi, :], dims, preferred_element_type=F32)
        if fox:
            t_d = t_d - ck_all[hh][:, lo:hi]
        t_d = jnp.where(causal, t_d, MASKED)
        t_o = None
        if c:
            t_o = lax.dot_general(q, keys[hh][0:lo, :], dims, preferred_element_type=F32)
            if fox:
                t_o = t_o - ck_all[hh][:, 0:lo]
        return t_d, t_o

    nq = seq // tq
    order = [(hh, c) for hh in range(heads) for c in range(nq - 1, -1, -1)]
    queue = [logits(item) for item in order[:LOOKAHEAD]]
    for i, (hh, c) in enumerate(order):
        lo, hi = c * tq, (c + 1) * tq
        cols = slice(hh * LANES, (hh + 1) * LANES)
        t_d, t_o = queue.pop(0)
        if i + LOOKAHEAD < len(order):
            queue.append(logits(order[i + LOOKAHEAD]))
        mt = jnp.max(t_d, axis=1, keepdims=True)
        if c:
            mt = jnp.maximum(mt, jnp.max(t_o, axis=1, keepdims=True))
        if fox:
            cq = cq_all[hh][lo:hi, :]
            shift = (mt + cq) - cq
        else:
            shift = mt
        acc = jnp.dot(jnp.exp2(t_d - shift).astype(BF16), vaug_ref[hh, lo:hi, :],
                      preferred_element_type=F32)
        if c:
            acc = acc + jnp.dot(jnp.exp2(t_o - shift).astype(BF16), vaug_ref[hh, 0:lo, :],
                                preferred_element_type=F32)
        inv_l = 1.0 / acc[:, LANES:LANES + 1]
        o_ref[lo:hi, cols] = (acc[:, 0:LANES] * inv_l
                              * z_ref[lo:hi, cols].astype(F32)).astype(o_ref.dtype)


ATTN_TQ = 256
LOOKAHEAD = 2
ATTN_HEADS = 4


def _attention_a(q_a, kv_a, k_rope, proj, batch, seq):
    t = q_a.shape[0]
    hs = ATTN_HEADS
    kern = functools.partial(_causal_attn_kernel, tq=ATTN_TQ, fox=False, heads=hs)
    z_col = CH_AZ * CHUNK // LANES
    need = 2 * (hs * seq * A_QPAD * 2 + (4 * hs + 1) * seq * LANES * 2) + 2 * hs * seq * A_QPAD * 2
    wide = pl.BlockSpec((seq, hs * LANES), lambda b, h: (b, h))
    return pl.pallas_call(
        kern,
        grid=(batch, HA // hs),
        in_specs=[pl.BlockSpec((seq, hs * A_QPAD), lambda b, h: (b, h)),
                  wide,
                  pl.BlockSpec((seq, LANES), lambda b, h: (b, 0)),
                  pl.BlockSpec((seq, hs * LANES), lambda b, h: (b, HA // hs + h)),
                  pl.BlockSpec((seq, hs * LANES), lambda b, h: (b, z_col // hs + h))],
        out_specs=wide,
        out_shape=jax.ShapeDtypeStruct((t, HA * LANES), BF16),
        scratch_shapes=[pltpu.VMEM((hs, seq, A_QPAD), BF16), pltpu.VMEM((hs, seq, 2 * LANES), BF16)],
        compiler_params=_params(("parallel", "arbitrary"), need + (24 << 20)),
        name="attn_mla",
    )(q_a, kv_a, k_rope, kv_a, proj)


def _attention_c(proj, ccol, crow, batch, seq):
    t = proj.shape[0]
    hs = ATTN_HEADS
    kern = functools.partial(_causal_attn_kernel, tq=ATTN_TQ, fox=True, heads=hs)
    q_col, k_col, v_col, z_col = (c * CHUNK // (hs * LANES) for c in (CH_CQ, CH_CK, CH_CV, CH_CZ))
    need = 2 * (5 * hs * seq * LANES * 2 + 2 * seq * LANES * 4) + hs * seq * 2 * LANES * 2

    def head_cols(first):
        return pl.BlockSpec((seq, hs * LANES), lambda b, h: (b, first + h))

    return pl.pallas_call(
        kern,
        grid=(batch, HC // hs),
        in_specs=[head_cols(q_col), head_cols(k_col), head_cols(v_col), head_cols(z_col),
                  pl.BlockSpec((seq, LANES), lambda b, h: (b, 0)),
                  pl.BlockSpec((1, LANES, seq), lambda b, h: (b, 0, 0))],
        out_specs=head_cols(0),
        out_shape=jax.ShapeDtypeStruct((t, HC * DC), BF16),
        scratch_shapes=[pltpu.VMEM((hs, seq, 2 * LANES), BF16)],
        compiler_params=_params(("parallel", "arbitrary"), need + (24 << 20)),
        name="attn_fox",
    )(proj, proj, proj, proj, ccol, crow)


SWA_BLOCKS = 2


def _swa_kernel(q_ref, kvp_ref, kvc_ref, z_ref, bias_ref, sink_ref, o_ref, vaug_ref):
    n0 = pl.program_id(1) * SWA_BLOCKS
    blk = SEQ_BLOCK
    band = 2 * blk
    rows = (GB // 2) * blk
    kv = jnp.concatenate([kvp_ref[...], kvc_ref[...]], axis=0).astype(F32)
    lane = lax.broadcasted_iota(jnp.int32, (kv.shape[0], LANES), 1)
    low = lane < DB

    def split(two_heads):
        lo = jnp.where(low, two_heads, 0.0)
        hi = jnp.where(low, 0.0, two_heads)
        return (lo, pltpu.roll(lo, DB, 1)), (pltpu.roll(hi, DB, 1), hi)

    k_heads = split(kv[:, 0:LANES]) + split(kv[:, LANES:2 * LANES])
    v_heads = split(kv[:, 2 * LANES:3 * LANES]) + split(kv[:, 3 * LANES:4 * LANES])

    r = lax.broadcasted_iota(jnp.int32, (rows, band), 0) % blk
    c = lax.broadcasted_iota(jnp.int32, (rows, band), 1)
    dist = r + blk - c
    in_window = (dist >= 0) & (dist < blk)
    valid = [in_window & (c >= jnp.where(n0 + u > 0, 0, blk)) for u in range(SWA_BLOCKS)]

    r2 = lax.broadcasted_iota(jnp.int32, (2 * band, LANES), 0)
    l2 = lax.broadcasted_iota(jnp.int32, (2 * band, LANES), 1)
    ones_cols = jnp.where(((l2 == 0) & (r2 < band)) | ((l2 == 1) & (r2 >= band)), 1.0, 0.0)
    units = [(u, kh) for u in range(SWA_BLOCKS) for kh in range(KVB)]
    for i, (u, kh) in enumerate(units):
        keys = slice(u * blk, u * blk + band)
        vaug_ref[i, :, LANES:2 * LANES] = ones_cols.astype(BF16)
        vaug_ref[i, :, 0:LANES] = jnp.concatenate([half[keys] for half in v_heads[kh]],
                                                  axis=0).astype(BF16)
    even_lanes = lax.broadcasted_iota(jnp.int32, (rows, LANES), 1) < DB

    def logits(u, kh):
        keys = slice(u * blk, u * blk + band)
        k_cat = jnp.concatenate([half[keys] for half in k_heads[kh]], axis=0).astype(BF16)
        q4 = jnp.concatenate(
            [q_ref[u * blk:(u + 1) * blk, (kh * (GB // 2) + jj) * LANES:(kh * (GB // 2) + jj + 1) * LANES]
             for jj in range(GB // 2)], axis=0)
        s = lax.dot_general(q4, k_cat, (((1,), (1,)), ((), ())), preferred_element_type=F32)
        return s + bias_ref[kh]

    ids = range(len(units))
    s_all = [logits(u, kh) for u, kh in units]
    se = [[jnp.where(valid[units[i][0]], s_all[i][:, e * band:(e + 1) * band], MASKED)
           for e in range(2)] for i in ids]
    sink = [[sink_ref[kh, :, e:e + 1] for e in range(2)] for _, kh in units]
    m = [[jnp.maximum(jnp.max(se[i][e], axis=1, keepdims=True), sink[i][e]) for e in range(2)]
         for i in ids]
    p = [jnp.concatenate([jnp.exp2(se[i][e] - m[i][e]).astype(BF16) for e in range(2)], axis=1)
         for i in ids]
    pv = [jnp.dot(p[i], vaug_ref[i], preferred_element_type=F32) for i in ids]
    for i, (u, kh) in enumerate(units):
        inv_even = 1.0 / (pv[i][:, LANES:LANES + 1] + jnp.exp2(sink[i][0] - m[i][0]))
        inv_odd = 1.0 / (pv[i][:, LANES + 1:LANES + 2] + jnp.exp2(sink[i][1] - m[i][1]))
        o = pv[i][:, 0:LANES] * jnp.where(even_lanes, inv_even, inv_odd)
        qrows = slice(u * blk, (u + 1) * blk)
        for jj in range(GB // 2):
            c0 = (kh * (GB // 2) + jj) * LANES
            o_ref[qrows, c0:c0 + LANES] = (o[jj * blk:(jj + 1) * blk, :]
                                           * z_ref[qrows, c0:c0 + LANES].astype(F32)).astype(o_ref.dtype)


def _attention_b(proj, bias, sink_cols, batch, seq):
    t = proj.shape[0]
    blk = SEQ_BLOCK
    nb = seq // blk
    step = SWA_BLOCKS * blk
    ns = seq // step
    width = HB * DB
    q_col = CH_BQ * CHUNK // width
    kv_col = CH_BKV
    z_col = CH_BZ * CHUNK // width
    assert CH_BQ * CHUNK % width == 0 and CH_BZ * CHUNK % width == 0
    need = 2 * (3 * step * width * 2 + 3 * step * CHUNK * 2 + bias.size * 4 + sink_cols.size * 4)
    return pl.pallas_call(
        _swa_kernel,
        grid=(batch, ns),
        in_specs=[pl.BlockSpec((step, width), lambda b, n: (b * ns + n, q_col)),
                  pl.BlockSpec((blk, CHUNK),
                               lambda b, n: (b * nb + jnp.maximum(n * SWA_BLOCKS - 1, 0), kv_col)),
                  pl.BlockSpec((step, CHUNK), lambda b, n: (b * ns + n, kv_col)),
                  pl.BlockSpec((step, width), lambda b, n: (b * ns + n, z_col)),
                  pl.BlockSpec(bias.shape, lambda b, n: (0, 0, 0)),
                  pl.BlockSpec(sink_cols.shape, lambda b, n: (0, 0, 0))],
        out_specs=pl.BlockSpec((step, width), lambda b, n: (b * ns + n, 0)),
        out_shape=jax.ShapeDtypeStruct((t, width), BF16),
        scratch_shapes=[pltpu.VMEM((SWA_BLOCKS * KVB, 4 * blk, 2 * LANES), BF16)],
        compiler_params=_params(("parallel", "arbitrary"), need + (32 << 20)),
        name="attn_swa",
    )(proj, proj, proj, proj, bias, sink_cols)


def _merge_kernel(oa_ref, ob_ref, oc_ref, wa_ref, wb_ref, wc_ref, ga_ref, gb_ref, gc_ref, y_ref):
    branches = ((oa_ref, wa_ref, ga_ref), (ob_ref, wb_ref, gb_ref), (oc_ref, wc_ref, gc_ref))
    for r in range(y_ref.shape[0] // ROW_CHUNK):
        rows = slice(r * ROW_CHUNK, (r + 1) * ROW_CHUNK)
        y = None
        for o_ref, w_ref, g_ref in branches:
            term = g_ref[rows, :].astype(F32) * jnp.dot(o_ref[rows, :], w_ref[...],
                                                        preferred_element_type=F32)
            y = term if y is None else y + term
        y_ref[rows, :] = y.astype(y_ref.dtype)


def _merge_proj(oa, ob, oc, wa, wb, wc, layer, proj):
    t, k = oa.shape
    tm, tn = 1024, CHUNK
    gcols = D // tn
    o_spec = pl.BlockSpec((tm, k), lambda i, j: (i, 0))
    w_spec = pl.BlockSpec((None, k, tn), lambda i, j: (layer, 0, j))

    def gate_spec(branch):
        return pl.BlockSpec((tm, tn), lambda i, j: (i, CH_GATE + branch * gcols + j))

    need = 2 * (3 * tm * k * 2 + 3 * k * tn * 2 + 4 * tm * tn * 2) + 4 * tm * tn * 4
    return pl.pallas_call(
        _merge_kernel,
        grid=(t // tm, D // tn),
        in_specs=[o_spec, o_spec, o_spec, w_spec, w_spec, w_spec,
                  gate_spec(0), gate_spec(1), gate_spec(2)],
        out_specs=pl.BlockSpec((tm, tn), lambda i, j: (i, j)),
        out_shape=jax.ShapeDtypeStruct((t, D), BF16),
        compiler_params=_params(("parallel", "arbitrary"), need + (8 << 20)),
        name="merge_proj",
    )(oa, ob, oc, wa, wb, wc, proj, proj, proj)


def _out_kernel(y_ref, w_ref, x_ref, g_ref, o_ref, acc_ref, ssq_ref):
    i = pl.program_id(0)
    j = pl.program_id(1)
    slot = i % 2
    prev = 1 - slot

    @pl.when((i == 0) & (j == 0))
    def _():
        acc_ref[...] = jnp.zeros_like(acc_ref)
        ssq_ref[...] = jnp.zeros_like(ssq_ref)

    def finish():
        rs = lax.rsqrt(ssq_ref[prev] * (1.0 / D) + NORM_EPS)
        o_ref[...] = x_ref[...] + acc_ref[j] * rs * g_ref[...]

    @pl.when(i < pl.num_programs(0) - 1)
    def _():
        finish()
        ss = jnp.where(j == 0, 0.0, ssq_ref[slot])
        for r in range(y_ref.shape[0] // ROW_CHUNK):
            rows = slice(r * ROW_CHUNK, (r + 1) * ROW_CHUNK)
            acc = jnp.dot(y_ref[rows, :], w_ref[...], preferred_element_type=F32)
            acc_ref[j, rows, :] = acc
            ssq_ref[slot, rows, :] = ss[rows, :] + jnp.sum(acc * acc, axis=1, keepdims=True)

    @pl.when(i == pl.num_programs(0) - 1)
    def _():
        finish()


def _out_proj(y, w_o, layer, x2, g):
    t = y.shape[0]
    tm, tn = 1024, CHUNK
    ni, nj = t // tm, D // tn
    need = 2 * (tm * D * 2 + D * tn * 2 + 2 * tm * tn * 4) + tm * D * 4 + 4 * tm * tn * 4
    return pl.pallas_call(
        _out_kernel,
        grid=(ni + 1, nj),
        in_specs=[pl.BlockSpec((tm, D), lambda i, j: (jnp.minimum(i, ni - 1), 0)),
                  pl.BlockSpec((None, D, tn), lambda i, j: (layer, 0, j)),
                  pl.BlockSpec((tm, tn), lambda i, j: (jnp.maximum(i - 1, 0), j)),
                  pl.BlockSpec((1, tn), lambda i, j: (0, j))],
        out_specs=pl.BlockSpec((tm, tn), lambda i, j: (jnp.maximum(i - 1, 0), j)),
        out_shape=jax.ShapeDtypeStruct((t, D), F32),
        scratch_shapes=[pltpu.VMEM((nj, tm, tn), F32), pltpu.VMEM((2, tm, 1), F32)],
        compiler_params=_params(("arbitrary", "arbitrary"), need + (8 << 20)),
        name="out_proj",
    )(y, w_o, x2, g.reshape(1, D))


def _cast_kernel(x_ref, o_ref):
    o_ref[...] = x_ref[...].astype(o_ref.dtype)


def _to_bf16(w):
    nl, r, c = w.shape
    tr, tc = 512, 2048
    spec = pl.BlockSpec((None, tr, tc), lambda l, i, j: (l, i, j))
    return pl.pallas_call(
        _cast_kernel,
        grid=(nl, r // tr, c // tc),
        in_specs=[spec],
        out_specs=spec,
        out_shape=jax.ShapeDtypeStruct(w.shape, BF16),
        compiler_params=_params(("arbitrary",) * 3, 6 * tr * tc * 4),
        name="cast_bf16",
    )(w)


def _transposed_w_in(w_in):
    w_t = jnp.swapaxes(w_in, 1, 2).astype(BF16)

    def rows(name):
        return w_t[:, IN_START[name]:IN_START[name] + IN_WIDTH[name], :]

    pad = jnp.zeros((w_in.shape[0], LANES - F_LANE - HC, D), BF16)
    misc = jnp.concatenate([rows("a_kr"), rows("c_f"), pad], axis=1)
    return w_t, misc


def _layer_weights(w_uq, w_uk, w_uv, b_f):
    col_scale = jnp.ones((NP,), F32)
    col_scale = col_scale.at[CH_BQ * CHUNK:CH_CQ * CHUNK].set(DB ** -0.5 * LOG2E)
    col_scale = col_scale.at[CH_CQ * CHUNK:CH_CK * CHUNK].set(DC ** -0.5 * LOG2E)
    col_scale = col_scale.reshape(1, NP)
    bf_row = jnp.concatenate([jnp.zeros((F_LANE,), F32), b_f.astype(F32),
                              jnp.zeros((LANES - F_LANE - HC,), F32)]).reshape(1, LANES)
    wq = jnp.pad(w_uq, ((0, 0), (0, 0), (0, A_QPAD - A_NOPE - A_ROPE)))
    wq = wq.reshape(A_QR, HA * A_QPAD).astype(BF16)
    wkv = jnp.concatenate([w_uk.reshape(A_KVR, HA * A_NOPE), w_uv.reshape(A_KVR, HA * A_NOPE)],
                          axis=1).astype(BF16)
    return col_scale, bf_row, wq, wkv


def _sink_columns(sinks):
    s = (sinks.astype(F32) * LOG2E).reshape(KVB, GB // 2, 1, 2)
    s = jnp.broadcast_to(s, (KVB, GB // 2, SEQ_BLOCK, 2)).reshape(KVB, (GB // 2) * SEQ_BLOCK, 2)
    return jnp.pad(s, ((0, 0), (0, 0), (0, LANES - 2)))


def kernel(x, positions, rel_table, pre_norm, w_in, q_a_norm, kv_a_norm, w_uq, w_uk, w_uv,
           sinks, b_f, w_proj_a, w_proj_b, w_proj_c, w_merge, w_o, post_norm):
    batch, seq, _ = x.shape
    depth = w_in.shape[0]
    x2 = x.reshape(batch * seq, D)
    tables = _rope_tables(positions)
    bias = _rel_bias(rel_table)
    wm16, wo16 = _to_bf16(w_merge), w_o.astype(BF16)
    wa16, wb16, wc16 = (w.astype(BF16) for w in (w_proj_a, w_proj_b, w_proj_c))
    w_inr, w_misc = _transposed_w_in(w_in)
    for l in range(depth):
        col_scale, bf_row, wq, wkv = _layer_weights(w_uq[l], w_uk[l], w_uv[l], b_f[l])
        h = _prenorm(x2, pre_norm[l])
        proj = _fused_proj(h, w_inr, wm16, l, col_scale)
        k_rope, ccol, crow = _misc_proj(h, w_misc, l, bf_row, tables[0], tables[1], batch, seq)
        q_a = _latent_proj(proj, 0, A_QR, q_a_norm[l], wq, tables,
                           (A_NOPE + A_ROPE) ** -0.5 * LOG2E)
        kv_a = _latent_proj(proj, CH_ACKV, A_KVR, kv_a_norm[l], wkv, None, 1.0)
        o_a = _attention_a(q_a, kv_a, k_rope, proj, batch, seq)
        o_b = _attention_b(proj, bias, _sink_columns(sinks[l]), batch, seq)
        o_c = _attention_c(proj, ccol, crow, batch, seq)
        y = _merge_proj(o_a, o_b, o_c, wa16, wb16, wc16, l, proj)
        x2 = _out_proj(y, wo16, l, x2, post_norm[l])
    return x2.reshape(batch, seq, D)
```

```python
import functools
import math

import jax
import jax.numpy as jnp
from jax import lax
from jax.experimental import pallas as pl
from jax.experimental.pallas import tpu as pltpu

F32 = jnp.float32
BF16 = jnp.bfloat16

D = 4096
SEQ_BLOCK = 128
NORM_EPS = 1e-6
MASKED = -1e30
LOG2E = math.log2(math.e)
HA = 16
A_QR = 1536
A_KVR = 512
A_NOPE = 128
A_ROPE = 64
A_HALF = A_ROPE // 2
A_QPAD = 256
THETA = 10000.0
HB = 32
KVB = 4
DB = 64
GB = HB // KVB
HC = 16
DC = 128
N_BUCKETS = 32
MAX_DIST = 128

LANES = 128
CHUNK = 512

CH_ACQ, CH_ACKV, CH_BQ, CH_CQ, CH_CK, CH_CV = 0, 3, 4, 8, 12, 16
CH_BZ, CH_AZ, CH_CZ = 20, 24, 28
CH_GATE = 32
CH_BKV = 56
N_CHUNKS = 57
NP = N_CHUNKS * CHUNK
BF16_SUBLANES = 16

IN_SIZES = (A_QR, A_KVR, A_ROPE, HA * A_NOPE, HB * DB, KVB * DB, KVB * DB, HB * DB,
            HC * DC, HC * DC, HC * DC, HC, HC * DC)
IN_NAMES = ("a_cq", "a_ckv", "a_kr", "a_z", "b_q", "b_k", "b_v", "b_z", "c_q", "c_k", "c_v", "c_f", "c_z")
IN_START = {name: sum(IN_SIZES[:i]) for i, name in enumerate(IN_NAMES)}
IN_WIDTH = dict(zip(IN_NAMES, IN_SIZES))
F_LANE = 64

VMEM_CAP = 60 * 1024 * 1024


def _vmem(nbytes):
    return int(min(VMEM_CAP, nbytes))


def _params(sem, nbytes):
    return pltpu.CompilerParams(dimension_semantics=sem, vmem_limit_bytes=_vmem(nbytes))


def _rope_table_kernel(pos_ref, freq_ref, cos_ref, sin_ref):
    ang = pos_ref[...].astype(F32) * freq_ref[...]
    lane = lax.broadcasted_iota(jnp.int32, ang.shape, 1)
    live = lane < A_ROPE
    s = jnp.sin(ang)
    cos_ref[...] = jnp.where(live, jnp.cos(ang), 0.0)
    sin_ref[...] = jnp.where(live, jnp.where(lane < A_HALF, -s, s), 0.0)


def _rope_tables(positions):
    t = positions.size
    tm = 2048
    half = jnp.arange(A_HALF, dtype=F32)
    inv_freq = THETA ** (-half / A_HALF)
    freq = jnp.concatenate([inv_freq, inv_freq, jnp.zeros((LANES - A_ROPE,), F32)]).reshape(1, LANES)
    pos = positions.reshape(t, 1)
    return pl.pallas_call(
        _rope_table_kernel,
        grid=(t // tm,),
        in_specs=[pl.BlockSpec((tm, 1), lambda i: (i, 0)),
                  pl.BlockSpec((1, LANES), lambda i: (0, 0))],
        out_specs=[pl.BlockSpec((tm, LANES), lambda i: (i, 0)),
                   pl.BlockSpec((tm, LANES), lambda i: (i, 0))],
        out_shape=[jax.ShapeDtypeStruct((t, LANES), F32)] * 2,
        compiler_params=_params(("arbitrary",), 32 << 20),
        name="rope_tables",
    )(pos, freq)


def _rope(t, cos, sin):
    lane = lax.broadcasted_iota(jnp.int32, t.shape, 1)
    partner = jnp.where(lane < A_HALF,
                        pltpu.roll(t, LANES - A_HALF, 1),
                        pltpu.roll(t, A_HALF, 1))
    return t * cos + partner * sin


def _relbias_kernel(table_ref, o_ref):
    h = pl.program_id(0)
    qi = lax.broadcasted_iota(jnp.int32, (SEQ_BLOCK, 2 * SEQ_BLOCK), 0)
    si = lax.broadcasted_iota(jnp.int32, (SEQ_BLOCK, 2 * SEQ_BLOCK), 1)
    dist = qi + SEQ_BLOCK - si
    max_exact = N_BUCKETS // 2
    d = jnp.maximum(dist, 0)
    large = max_exact + (jnp.log(jnp.maximum(d, 1).astype(F32) / max_exact)
                         / math.log(MAX_DIST / max_exact)
                         * (N_BUCKETS - max_exact)).astype(jnp.int32)
    large = jnp.minimum(large, N_BUCKETS - 1)
    bucket = jnp.where(d < max_exact, d, large)
    acc = jnp.zeros(bucket.shape, F32)
    for b in range(N_BUCKETS):
        acc = jnp.where(bucket == b, table_ref[b, h], acc)
    o_ref[0] = acc * LOG2E


def _rel_bias(rel_table):
    def out_map(h):
        return (h // GB, (h % GB) // 2, h % 2)
    return pl.pallas_call(
        _relbias_kernel,
        grid=(HB,),
        in_specs=[pl.BlockSpec(memory_space=pltpu.SMEM)],
        out_specs=pl.BlockSpec((1, SEQ_BLOCK, 2 * SEQ_BLOCK), out_map),
        out_shape=jax.ShapeDtypeStruct((KVB, (GB // 2) * SEQ_BLOCK, 4 * SEQ_BLOCK), F32),
        compiler_params=_params(("arbitrary",), 16 << 20),
        name="rel_bias",
    )(rel_table)


def _prenorm_kernel(x_ref, g_ref, o_ref):
    x = x_ref[...]
    r = lax.rsqrt(jnp.mean(x * x, axis=-1, keepdims=True) + NORM_EPS)
    o_ref[...] = (x * r * g_ref[...]).astype(o_ref.dtype)


def _prenorm(x2, g):
    t = x2.shape[0]
    tm = 256
    return pl.pallas_call(
        _prenorm_kernel,
        grid=(t // tm,),
        in_specs=[pl.BlockSpec((tm, D), lambda i: (i, 0)),
                  pl.BlockSpec((1, D), lambda i: (0, 0))],
        out_specs=pl.BlockSpec((tm, D), lambda i: (i, 0)),
        out_shape=jax.ShapeDtypeStruct((t, D), BF16),
        compiler_params=_params(("arbitrary",), 32 << 20),
        name="prenorm",
    )(x2, g.reshape(1, D))


ROW_CHUNK = 256


def _chunked_dot(x_ref, w_ref, o_ref, epilogue, trans_w=False):
    dims = (((1,), (1,) if trans_w else (0,)), ((), ()))
    for r in range(x_ref.shape[0] // ROW_CHUNK):
        rows = slice(r * ROW_CHUNK, (r + 1) * ROW_CHUNK)
        acc = lax.dot_general(x_ref[rows, :], w_ref[...], dims, preferred_element_type=F32)
        o_ref[rows, :] = epilogue(acc, rows).astype(o_ref.dtype)


def _proj_kernel(rows_ref, h_ref, wi_ref, wm_ref, cs_ref, o_ref, *,
                 silu_start, gate_start, gate_end):
    del rows_ref
    j = pl.program_id(1)
    wi = wi_ref.at[0]

    @pl.when((j < silu_start) | (j >= gate_end))
    def _():
        _chunked_dot(h_ref, wi, o_ref, lambda a, rows: a * cs_ref[...], trans_w=True)

    @pl.when((j >= silu_start) & (j < gate_start))
    def _():
        _chunked_dot(h_ref, wi, o_ref, lambda a, rows: a * jax.nn.sigmoid(a), trans_w=True)

    @pl.when((j >= gate_start) & (j < gate_end))
    def _():
        _chunked_dot(h_ref, wm_ref, o_ref, lambda a, rows: jax.nn.sigmoid(a))


def _source_rows():
    starts = {CH_ACQ: "a_cq", CH_ACKV: "a_ckv", CH_BQ: "b_q", CH_CQ: "c_q", CH_CK: "c_k",
              CH_CV: "c_v", CH_BZ: "b_z", CH_AZ: "a_z", CH_CZ: "c_z", CH_BKV: "b_k"}
    rows, row = [], 0
    for j in range(N_CHUNKS):
        if CH_GATE <= j < CH_BKV:
            rows.append(rows[-1])
            continue
        if j in starts:
            row = IN_START[starts[j]]
        rows.append(row)
        row += CHUNK
    return jnp.asarray(rows, jnp.int32)


def _fused_proj(h, w_t, w_merge, layer, col_scale):
    t = h.shape[0]
    tm, tn = 2048, CHUNK
    n_gate = CH_BKV - CH_GATE
    kern = functools.partial(_proj_kernel, silu_start=CH_BZ, gate_start=CH_GATE, gate_end=CH_BKV)

    def wi_map(i, j, rows):
        return (layer, pl.multiple_of(rows[j], BF16_SUBLANES), 0)

    def wm_map(i, j, rows):
        return (layer, 0, jnp.clip(j - CH_GATE, 0, n_gate - 1))

    need = 2 * (tm * D * 2 + 2 * D * tn * 2 + tm * tn * 2) + 4 * tm * tn * 4
    grid_spec = pltpu.PrefetchScalarGridSpec(
        num_scalar_prefetch=1,
        grid=(t // tm, NP // tn),
        in_specs=[pl.BlockSpec((tm, D), lambda i, j, rows: (i, 0)),
                  pl.BlockSpec((pl.Element(1), pl.Element(tn), pl.Element(D)), wi_map),
                  pl.BlockSpec((None, D, tn), wm_map),
                  pl.BlockSpec((1, tn), lambda i, j, rows: (0, j))],
        out_specs=pl.BlockSpec((tm, tn), lambda i, j, rows: (i, j)),
    )
    return pl.pallas_call(
        kern,
        grid_spec=grid_spec,
        out_shape=jax.ShapeDtypeStruct((t, NP), BF16),
        compiler_params=_params(("parallel", "arbitrary"), need + (8 << 20)),
        name="fused_proj",
    )(_source_rows(), h, w_t, w_merge, col_scale)


def _misc_kernel(h_ref, w_ref, bf_ref, cos_ref, sin_ref, kr_ref, ccol_ref, crow_ref, carry_ref):
    t = pl.program_id(1)
    acc = lax.dot_general(h_ref[...], w_ref[...], (((1,), (1,)), ((), ())),
                          preferred_element_type=F32)
    kr_ref[...] = _rope(acc, cos_ref[...], sin_ref[...]).astype(kr_ref.dtype)

    xx = acc + bf_ref[...]
    log_f = jnp.minimum(xx, 0.0) - jnp.log1p(jnp.exp(-jnp.abs(xx)))
    hi = log_f.astype(BF16)
    r1 = log_f - hi.astype(F32)
    mid = r1.astype(BF16)
    lo = (r1 - mid.astype(F32)).astype(BF16)
    tm = acc.shape[0]
    ri = lax.broadcasted_iota(jnp.int32, (tm, tm), 0)
    ci = lax.broadcasted_iota(jnp.int32, (tm, tm), 1)
    tri = jnp.where(ri >= ci, 1.0, 0.0).astype(BF16)
    cs = (jnp.dot(tri, hi, preferred_element_type=F32)
          + jnp.dot(tri, mid, preferred_element_type=F32)
          + jnp.dot(tri, lo, preferred_element_type=F32))

    @pl.when(t == 0)
    def _():
        carry_ref[...] = jnp.zeros_like(carry_ref)

    cs = cs + carry_ref[...]
    carry_ref[...] = cs[tm - 1:tm, :]
    cs2 = cs * LOG2E
    ccol_ref[...] = cs2
    crow_ref[0] = cs2.T


def _misc_proj(h, w_misc, layer, bf_row, cos_t, sin_t, batch, seq):
    t = h.shape[0]
    tm = 1024
    nt = seq // tm
    need = 2 * (tm * D * 2 + D * LANES * 2 + 5 * tm * LANES * 4) + 4 * tm * tm * 4
    return pl.pallas_call(
        _misc_kernel,
        grid=(batch, nt),
        in_specs=[pl.BlockSpec((tm, D), lambda b, i: (b * nt + i, 0)),
                  pl.BlockSpec((None, LANES, D), lambda b, i: (layer, 0, 0)),
                  pl.BlockSpec((1, LANES), lambda b, i: (0, 0)),
                  pl.BlockSpec((tm, LANES), lambda b, i: (b * nt + i, 0)),
                  pl.BlockSpec((tm, LANES), lambda b, i: (b * nt + i, 0))],
        out_specs=[pl.BlockSpec((tm, LANES), lambda b, i: (b * nt + i, 0)),
                   pl.BlockSpec((tm, LANES), lambda b, i: (b * nt + i, 0)),
                   pl.BlockSpec((1, LANES, tm), lambda b, i: (b, 0, i))],
        out_shape=[jax.ShapeDtypeStruct((t, LANES), BF16),
                   jax.ShapeDtypeStruct((t, LANES), F32),
                   jax.ShapeDtypeStruct((batch, LANES, seq), F32)],
        scratch_shapes=[pltpu.VMEM((1, LANES), F32)],
        compiler_params=_params(("arbitrary", "arbitrary"), need + (8 << 20)),
        name="misc_proj",
    )(h, w_misc, bf_row, cos_t, sin_t)


def _latent_kernel(*refs, rope, out_scale):
    if rope:
        x_ref, g_ref, w_ref, cos_ref, sin_ref, o_ref, xn_ref = refs
    else:
        x_ref, g_ref, w_ref, o_ref, xn_ref = refs

    @pl.when(pl.program_id(1) == 0)
    def _():
        x = x_ref[...].astype(F32)
        r = lax.rsqrt(jnp.mean(x * x, axis=-1, keepdims=True) + NORM_EPS)
        xn_ref[...] = (x * r * g_ref[...]).astype(xn_ref.dtype)

    def rotary(acc, rows):
        cos = cos_ref[rows, :]
        sin = sin_ref[rows, :]
        pieces = []
        for hh in range(acc.shape[1] // A_QPAD):
            base = hh * A_QPAD
            pieces.append(acc[:, base:base + A_NOPE])
            pieces.append(_rope(acc[:, base + A_NOPE:base + A_QPAD], cos, sin))
        return jnp.concatenate(pieces, axis=1) * out_scale

    _chunked_dot(xn_ref, w_ref, o_ref, rotary if rope else (lambda acc, rows: acc))


def _latent_proj(proj, col_block, k, g, w, tables, out_scale):
    t = proj.shape[0]
    n = w.shape[1]
    tm, tn = 1024, 2048
    rope = tables is not None
    kern = functools.partial(_latent_kernel, rope=rope, out_scale=out_scale)
    in_specs = [pl.BlockSpec((tm, k), lambda i, j: (i, col_block)),
                pl.BlockSpec((1, k), lambda i, j: (0, 0)),
                pl.BlockSpec((k, tn), lambda i, j: (0, j))]
    args = [proj, g.reshape(1, k), w]
    if rope:
        in_specs += [pl.BlockSpec((tm, LANES), lambda i, j: (i, 0))] * 2
        args += list(tables)
    need = 2 * (tm * k * 2 + k * tn * 2 + tm * tn * 2 + 2 * tm * LANES * 4) + tm * k * 2 + 4 * tm * tn * 4
    return pl.pallas_call(
        kern,
        grid=(t // tm, n // tn),
        in_specs=in_specs,
        out_specs=pl.BlockSpec((tm, tn), lambda i, j: (i, j)),
        out_shape=jax.ShapeDtypeStruct((t, n), BF16),
        scratch_shapes=[pltpu.VMEM((tm, k), BF16)],
        compiler_params=_params(("parallel", "arbitrary"), need + (8 << 20)),
        name="latent_q" if rope else "latent_kv",
    )(*args)


def _causal_attn_kernel(*refs, tq, fox, heads):
    if fox:
        q_ref, k_ref, v_ref, z_ref, ccol_ref, crow_ref, o_ref, vaug_ref = refs
    else:
        q_ref, kn_ref, kr_ref, v_ref, z_ref, o_ref, kcat_ref, vaug_ref = refs
    seq = q_ref.shape[0]
    dk = q_ref.shape[1] // heads
    lane = lax.broadcasted_iota(jnp.int32, (seq, LANES), 1)
    ones_col = jnp.where(lane == 0, 1.0, 0.0).astype(BF16)
    keys, cq_all, ck_all = [], [], []
    for hh in range(heads):
        cols = slice(hh * LANES, (hh + 1) * LANES)
        vaug_ref[hh, :, 0:LANES] = v_ref[:, cols]
        vaug_ref[hh, :, LANES:2 * LANES] = ones_col
        if fox:
            keys.append(k_ref.at[:, cols])
            h = pl.program_id(1) * heads + hh
            cq_all.append(jnp.sum(jnp.where(lane == F_LANE + h, ccol_ref[...], 0.0),
                                  axis=1, keepdims=True))
            ck_all.append(crow_ref[0, pl.ds(F_LANE + h, 1), :])
        else:
            kcat_ref[hh, :, 0:LANES] = kn_ref[:, cols]
            kcat_ref[hh, :, LANES:2 * LANES] = kr_ref[...]
            keys.append(kcat_ref.at[hh])
    row = lax.broadcasted_iota(jnp.int32, (tq, tq), 0)
    col = lax.broadcasted_iota(jnp.int32, (tq, tq), 1)
    causal = row >= col
    dims = (((1,), (1,)), ((), ()))

    def logits(item):
        hh, c = item
        lo, hi = c * tq, (c + 1) * tq
        q = q_ref[lo:hi, hh * dk:(hh + 1) * dk]
        t_d = lax.dot_general(q, keys[hh][lo:hi, :], dims, preferred_element_type=F32)
        if fox:
            t_d = t_d - ck_all[hh][:, lo:hi]
        t_d = jnp.where(causal, t_d, MASKED)
        t_o = None
        if c:
            t_o = lax.dot_general(q, keys[hh][0:lo, :], dims, preferred_element_type=F32)
            if fox:
                t_o = t_o - ck_all[hh][:, 0:lo]
        return t_d, t_o

    nq = seq // tq
    order = [(hh, c) for hh in range(heads) for c in range(nq - 1, -1, -1)]
    queue = [logits(item) for item in order[:LOOKAHEAD]]
    for i, (hh, c) in enumerate(order):
        lo, hi = c * tq, (c + 1) * tq
        cols = slice(hh * LANES, (hh + 1) * LANES)
        t_d, t_o = queue.pop(0)
        if i + LOOKAHEAD < len(order):
            queue.append(logits(order[i + LOOKAHEAD]))
        mt = jnp.max(t_d, axis=1, keepdims=True)
        if c:
            mt = jnp.maximum(mt, jnp.max(t_o, axis=1, keepdims=True))
        if fox:
            cq = cq_all[hh][lo:hi, :]
            shift = (mt + cq) - cq
        else:
            shift = mt
        acc = jnp.dot(jnp.exp2(t_d - shift).astype(BF16), vaug_ref[hh, lo:hi, :],
                      preferred_element_type=F32)
        if c:
            acc = acc + jnp.dot(jnp.exp2(t_o - shift).astype(BF16), vaug_ref[hh, 0:lo, :],
                                preferred_element_type=F32)
        inv_l = 1.0 / acc[:, LANES:LANES + 1]
        o_ref[lo:hi, cols] = (acc[:, 0:LANES] * inv_l
                              * z_ref[lo:hi, cols].astype(F32)).astype(o_ref.dtype)


ATTN_TQ = 256
LOOKAHEAD = 2
ATTN_HEADS = 4


def _attention_a(q_a, kv_a, k_rope, proj, batch, seq):
    t = q_a.shape[0]
    hs = ATTN_HEADS
    kern = functools.partial(_causal_attn_kernel, tq=ATTN_TQ, fox=False, heads=hs)
    z_col = CH_AZ * CHUNK // LANES
    need = 2 * (hs * seq * A_QPAD * 2 + (4 * hs + 1) * seq * LANES * 2) + 2 * hs * seq * A_QPAD * 2
    wide = pl.BlockSpec((seq, hs * LANES), lambda b, h: (b, h))
    return pl.pallas_call(
        kern,
        grid=(batch, HA // hs),
        in_specs=[pl.BlockSpec((seq, hs * A_QPAD), lambda b, h: (b, h)),
                  wide,
                  pl.BlockSpec((seq, LANES), lambda b, h: (b, 0)),
                  pl.BlockSpec((seq, hs * LANES), lambda b, h: (b, HA // hs + h)),
                  pl.BlockSpec((seq, hs * LANES), lambda b, h: (b, z_col // hs + h))],
        out_specs=wide,
        out_shape=jax.ShapeDtypeStruct((t, HA * LANES), BF16),
        scratch_shapes=[pltpu.VMEM((hs, seq, A_QPAD), BF16), pltpu.VMEM((hs, seq, 2 * LANES), BF16)],
        compiler_params=_params(("parallel", "arbitrary"), need + (24 << 20)),
        name="attn_mla",
    )(q_a, kv_a, k_rope, kv_a, proj)


def _attention_c(proj, ccol, crow, batch, seq):
    t = proj.shape[0]
    hs = ATTN_HEADS
    kern = functools.partial(_causal_attn_kernel, tq=ATTN_TQ, fox=True, heads=hs)
    q_col, k_col, v_col, z_col = (c * CHUNK // (hs * LANES) for c in (CH_CQ, CH_CK, CH_CV, CH_CZ))
    need = 2 * (5 * hs * seq * LANES * 2 + 2 * seq * LANES * 4) + hs * seq * 2 * LANES * 2

    def head_cols(first):
        return pl.BlockSpec((seq, hs * LANES), lambda b, h: (b, first + h))

    return pl.pallas_call(
        kern,
        grid=(batch, HC // hs),
        in_specs=[head_cols(q_col), head_cols(k_col), head_cols(v_col), head_cols(z_col),
                  pl.BlockSpec((seq, LANES), lambda b, h: (b, 0)),
                  pl.BlockSpec((1, LANES, seq), lambda b, h: (b, 0, 0))],
        out_specs=head_cols(0),
        out_shape=jax.ShapeDtypeStruct((t, HC * DC), BF16),
        scratch_shapes=[pltpu.VMEM((hs, seq, 2 * LANES), BF16)],
        compiler_params=_params(("parallel", "arbitrary"), need + (24 << 20)),
        name="attn_fox",
    )(proj, proj, proj, proj, ccol, crow)


SWA_BLOCKS = 2


def _swa_kernel(q_ref, kvp_ref, kvc_ref, z_ref, bias_ref, sink_ref, o_ref, vaug_ref):
    n0 = pl.program_id(1) * SWA_BLOCKS
    blk = SEQ_BLOCK
    band = 2 * blk
    rows = (GB // 2) * blk
    kv = jnp.concatenate([kvp_ref[...], kvc_ref[...]], axis=0).astype(F32)
    lane = lax.broadcasted_iota(jnp.int32, (kv.shape[0], LANES), 1)
    low = lane < DB

    def split(two_heads):
        lo = jnp.where(low, two_heads, 0.0)
        hi = jnp.where(low, 0.0, two_heads)
        return (lo, pltpu.roll(lo, DB, 1)), (pltpu.roll(hi, DB, 1), hi)

    k_heads = split(kv[:, 0:LANES]) + split(kv[:, LANES:2 * LANES])
    v_heads = split(kv[:, 2 * LANES:3 * LANES]) + split(kv[:, 3 * LANES:4 * LANES])

    r = lax.broadcasted_iota(jnp.int32, (rows, band), 0) % blk
    c = lax.broadcasted_iota(jnp.int32, (rows, band), 1)
    dist = r + blk - c
    in_window = (dist >= 0) & (dist < blk)
    valid = [in_window & (c >= jnp.where(n0 + u > 0, 0, blk)) for u in range(SWA_BLOCKS)]

    r2 = lax.broadcasted_iota(jnp.int32, (2 * band, LANES), 0)
    l2 = lax.broadcasted_iota(jnp.int32, (2 * band, LANES), 1)
    ones_cols = jnp.where(((l2 == 0) & (r2 < band)) | ((l2 == 1) & (r2 >= band)), 1.0, 0.0)
    units = [(u, kh) for u in range(SWA_BLOCKS) for kh in range(KVB)]
    for i, (u, kh) in enumerate(units):
        keys = slice(u * blk, u * blk + band)
        vaug_ref[i, :, LANES:2 * LANES] = ones_cols.astype(BF16)
        vaug_ref[i, :, 0:LANES] = jnp.concatenate([half[keys] for half in v_heads[kh]],
                                                  axis=0).astype(BF16)
    even_lanes = lax.broadcasted_iota(jnp.int32, (rows, LANES), 1) < DB

    def logits(u, kh):
        keys = slice(u * blk, u * blk + band)
        k_cat = jnp.concatenate([half[keys] for half in k_heads[kh]], axis=0).astype(BF16)
        q4 = jnp.concatenate(
            [q_ref[u * blk:(u + 1) * blk, (kh * (GB // 2) + jj) * LANES:(kh * (GB // 2) + jj + 1) * LANES]
             for jj in range(GB // 2)], axis=0)
        s = lax.dot_general(q4, k_cat, (((1,), (1,)), ((), ())), preferred_element_type=F32)
        return s + bias_ref[kh]

    ids = range(len(units))
    s_all = [logits(u, kh) for u, kh in units]
    se = [[jnp.where(valid[units[i][0]], s_all[i][:, e * band:(e + 1) * band], MASKED)
           for e in range(2)] for i in ids]
    sink = [[sink_ref[kh, :, e:e + 1] for e in range(2)] for _, kh in units]
    m = [[jnp.maximum(jnp.max(se[i][e], axis=1, keepdims=True), sink[i][e]) for e in range(2)]
         for i in ids]
    p = [jnp.concatenate([jnp.exp2(se[i][e] - m[i][e]).astype(BF16) for e in range(2)], axis=1)
         for i in ids]
    pv = [jnp.dot(p[i], vaug_ref[i], preferred_element_type=F32) for i in ids]
    for i, (u, kh) in enumerate(units):
        inv_even = 1.0 / (pv[i][:, LANES:LANES + 1] + jnp.exp2(sink[i][0] - m[i][0]))
        inv_odd = 1.0 / (pv[i][:, LANES + 1:LANES + 2] + jnp.exp2(sink[i][1] - m[i][1]))
        o = pv[i][:, 0:LANES] * jnp.where(even_lanes, inv_even, inv_odd)
        qrows = slice(u * blk, (u + 1) * blk)
        for jj in range(GB // 2):
            c0 = (kh * (GB // 2) + jj) * LANES
            o_ref[qrows, c0:c0 + LANES] = (o[jj * blk:(jj + 1) * blk, :]
                                           * z_ref[qrows, c0:c0 + LANES].astype(F32)).astype(o_ref.dtype)


def _attention_b(proj, bias, sink_cols, batch, seq):
    t = proj.shape[0]
    blk = SEQ_BLOCK
    nb = seq // blk
    step = SWA_BLOCKS * blk
    ns = seq // step
    width = HB * DB
    q_col = CH_BQ * CHUNK // width
    kv_col = CH_BKV
    z_col = CH_BZ * CHUNK // width
    assert CH_BQ * CHUNK % width == 0 and CH_BZ * CHUNK % width == 0
    need = 2 * (3 * step * width * 2 + 3 * step * CHUNK * 2 + bias.size * 4 + sink_cols.size * 4)
    return pl.pallas_call(
        _swa_kernel,
        grid=(batch, ns),
        in_specs=[pl.BlockSpec((step, width), lambda b, n: (b * ns + n, q_col)),
                  pl.BlockSpec((blk, CHUNK),
                               lambda b, n: (b * nb + jnp.maximum(n * SWA_BLOCKS - 1, 0), kv_col)),
                  pl.BlockSpec((step, CHUNK), lambda b, n: (b * ns + n, kv_col)),
                  pl.BlockSpec((step, width), lambda b, n: (b * ns + n, z_col)),
                  pl.BlockSpec(bias.shape, lambda b, n: (0, 0, 0)),
                  pl.BlockSpec(sink_cols.shape, lambda b, n: (0, 0, 0))],
        out_specs=pl.BlockSpec((step, width), lambda b, n: (b * ns + n, 0)),
        out_shape=jax.ShapeDtypeStruct((t, width), BF16),
        scratch_shapes=[pltpu.VMEM((SWA_BLOCKS * KVB, 4 * blk, 2 * LANES), BF16)],
        compiler_params=_params(("parallel", "arbitrary"), need + (32 << 20)),
        name="attn_swa",
    )(proj, proj, proj, proj, bias, sink_cols)


def _merge_kernel(oa_ref, ob_ref, oc_ref, wa_ref, wb_ref, wc_ref, ga_ref, gb_ref, gc_ref, y_ref):
    branches = ((oa_ref, wa_ref, ga_ref), (ob_ref, wb_ref, gb_ref), (oc_ref, wc_ref, gc_ref))
    for r in range(y_ref.shape[0] // ROW_CHUNK):
        rows = slice(r * ROW_CHUNK, (r + 1) * ROW_CHUNK)
        y = None
        for o_ref, w_ref, g_ref in branches:
            term = g_ref[rows, :].astype(F32) * jnp.dot(o_ref[rows, :], w_ref[...],
                                                        preferred_element_type=F32)
            y = term if y is None else y + term
        y_ref[rows, :] = y.astype(y_ref.dtype)


def _merge_proj(oa, ob, oc, wa, wb, wc, layer, proj):
    t, k = oa.shape
    tm, tn = 1024, CHUNK
    gcols = D // tn
    o_spec = pl.BlockSpec((tm, k), lambda i, j: (i, 0))
    w_spec = pl.BlockSpec((None, k, tn), lambda i, j: (layer, 0, j))

    def gate_spec(branch):
        return pl.BlockSpec((tm, tn), lambda i, j: (i, CH_GATE + branch * gcols + j))

    need = 2 * (3 * tm * k * 2 + 3 * k * tn * 2 + 4 * tm * tn * 2) + 4 * tm * tn * 4
    return pl.pallas_call(
        _merge_kernel,
        grid=(t // tm, D // tn),
        in_specs=[o_spec, o_spec, o_spec, w_spec, w_spec, w_spec,
                  gate_spec(0), gate_spec(1), gate_spec(2)],
        out_specs=pl.BlockSpec((tm, tn), lambda i, j: (i, j)),
        out_shape=jax.ShapeDtypeStruct((t, D), BF16),
        compiler_params=_params(("parallel", "arbitrary"), need + (8 << 20)),
        name="merge_proj",
    )(oa, ob, oc, wa, wb, wc, proj, proj, proj)


def _out_kernel(y_ref, w_ref, x_ref, g_ref, o_ref, acc_ref, ssq_ref):
    i = pl.program_id(0)
    j = pl.program_id(1)
    slot = i % 2
    prev = 1 - slot

    @pl.when((i == 0) & (j == 0))
    def _():
        acc_ref[...] = jnp.zeros_like(acc_ref)
        ssq_ref[...] = jnp.zeros_like(ssq_ref)

    def finish():
        rs = lax.rsqrt(ssq_ref[prev] * (1.0 / D) + NORM_EPS)
        o_ref[...] = x_ref[...] + acc_ref[j] * rs * g_ref[...]

    @pl.when(i < pl.num_programs(0) - 1)
    def _():
        finish()
        ss = jnp.where(j == 0, 0.0, ssq_ref[slot])
        for r in range(y_ref.shape[0] // ROW_CHUNK):
            rows = slice(r * ROW_CHUNK, (r + 1) * ROW_CHUNK)
            acc = jnp.dot(y_ref[rows, :], w_ref[...], preferred_element_type=F32)
            acc_ref[j, rows, :] = acc
            ssq_ref[slot, rows, :] = ss[rows, :] + jnp.sum(acc * acc, axis=1, keepdims=True)

    @pl.when(i == pl.num_programs(0) - 1)
    def _():
        finish()


def _out_proj(y, w_o, layer, x2, g):
    t = y.shape[0]
    tm, tn = 1024, CHUNK
    ni, nj = t // tm, D // tn
    need = 2 * (tm * D * 2 + D * tn * 2 + 2 * tm * tn * 4) + tm * D * 4 + 4 * tm * tn * 4
    return pl.pallas_call(
        _out_kernel,
        grid=(ni + 1, nj),
        in_specs=[pl.BlockSpec((tm, D), lambda i, j: (jnp.minimum(i, ni - 1), 0)),
                  pl.BlockSpec((None, D, tn), lambda i, j: (layer, 0, j)),
                  pl.BlockSpec((tm, tn), lambda i, j: (jnp.maximum(i - 1, 0), j)),
                  pl.BlockSpec((1, tn), lambda i, j: (0, j))],
        out_specs=pl.BlockSpec((tm, tn), lambda i, j: (jnp.maximum(i - 1, 0), j)),
        out_shape=jax.ShapeDtypeStruct((t, D), F32),
        scratch_shapes=[pltpu.VMEM((nj, tm, tn), F32), pltpu.VMEM((2, tm, 1), F32)],
        compiler_params=_params(("arbitrary", "arbitrary"), need + (8 << 20)),
        name="out_proj",
    )(y, w_o, x2, g.reshape(1, D))


def _cast_kernel(x_ref, o_ref):
    o_ref[...] = x_ref[...].astype(o_ref.dtype)


def _to_bf16(w):
    nl, r, c = w.shape
    tr, tc = 512, 2048
    spec = pl.BlockSpec((None, tr, tc), lambda l, i, j: (l, i, j))
    return pl.pallas_call(
        _cast_kernel,
        grid=(nl, r // tr, c // tc),
        in_specs=[spec],
        out_specs=spec,
        out_shape=jax.ShapeDtypeStruct(w.shape, BF16),
        compiler_params=_params(("arbitrary",) * 3, 6 * tr * tc * 4),
        name="cast_bf16",
    )(w)


def _transposed_w_in(w_in):
    w_t = jnp.swapaxes(w_in, 1, 2).astype(BF16)

    def rows(name):
        return w_t[:, IN_START[name]:IN_START[name] + IN_WIDTH[name], :]

    pad = jnp.zeros((w_in.shape[0], LANES - F_LANE - HC, D), BF16)
    misc = jnp.concatenate([rows("a_kr"), rows("c_f"), pad], axis=1)
    return w_t, misc


def _layer_weights(w_uq, w_uk, w_uv, b_f):
    col_scale = jnp.ones((NP,), F32)
    col_scale = col_scale.at[CH_BQ * CHUNK:CH_CQ * CHUNK].set(DB ** -0.5 * LOG2E)
    col_scale = col_scale.at[CH_CQ * CHUNK:CH_CK * CHUNK].set(DC ** -0.5 * LOG2E)
    col_scale = col_scale.reshape(1, NP)
    bf_row = jnp.concatenate([jnp.zeros((F_LANE,), F32), b_f.astype(F32),
                              jnp.zeros((LANES - F_LANE - HC,), F32)]).reshape(1, LANES)
    wq = jnp.pad(w_uq, ((0, 0), (0, 0), (0, A_QPAD - A_NOPE - A_ROPE)))
    wq = wq.reshape(A_QR, HA * A_QPAD).astype(BF16)
    wkv = jnp.concatenate([w_uk.reshape(A_KVR, HA * A_NOPE), w_uv.reshape(A_KVR, HA * A_NOPE)],
                          axis=1).astype(BF16)
    return col_scale, bf_row, wq, wkv


def _sink_columns(sinks):
    s = (sinks.astype(F32) * LOG2E).reshape(KVB, GB // 2, 1, 2)
    s = jnp.broadcast_to(s, (KVB, GB // 2, SEQ_BLOCK, 2)).reshape(KVB, (GB // 2) * SEQ_BLOCK, 2)
    return jnp.pad(s, ((0, 0), (0, 0), (0, LANES - 2)))


def kernel(x, positions, rel_table, pre_norm, w_in, q_a_norm, kv_a_norm, w_uq, w_uk, w_uv,
           sinks, b_f, w_proj_a, w_proj_b, w_proj_c, w_merge, w_o, post_norm):
    batch, seq, _ = x.shape
    depth = w_in.shape[0]
    x2 = x.reshape(batch * seq, D)
    tables = _rope_tables(positions)
    bias = _rel_bias(rel_table)
    wm16, wo16 = _to_bf16(w_merge), w_o.astype(BF16)
    wa16, wb16, wc16 = (w.astype(BF16) for w in (w_proj_a, w_proj_b, w_proj_c))
    w_inr, w_misc = _transposed_w_in(w_in)
    for l in range(depth):
        col_scale, bf_row, wq, wkv = _layer_weights(w_uq[l], w_uk[l], w_uv[l], b_f[l])
        h = _prenorm(x2, pre_norm[l])
        proj = _fused_proj(h, w_inr, wm16, l, col_scale)
        k_rope, ccol, crow = _misc_proj(h, w_misc, l, bf_row, tables[0], tables[1], batch, seq)
        q_a = _latent_proj(proj, 0, A_QR, q_a_norm[l], wq, tables,
                           (A_NOPE + A_ROPE) ** -0.5 * LOG2E)
        kv_a = _latent_proj(proj, CH_ACKV, A_KVR, kv_a_norm[l], wkv, None, 1.0)
        o_a = _attention_a(q_a, kv_a, k_rope, proj, batch, seq)
        o_b = _attention_b(proj, bias, _sink_columns(sinks[l]), batch, seq)
        o_c = _attention_c(proj, ccol, crow, batch, seq)
        y = _merge_proj(o_a, o_b, o_c, wa16, wb16, wc16, l, proj)
        x2 = _out_proj(y, wo16, l, x2, post_norm[l])
    return x2.reshape(batch, seq, D)
```

```python
import functools
import math

import jax
import jax.numpy as jnp
from jax import lax
from jax.experimental import pallas as pl
from jax.experimental.pallas import tpu as pltpu

F32 = jnp.float32
BF16 = jnp.bfloat16

D = 4096
SEQ_BLOCK = 128
NORM_EPS = 1e-6
MASKED = -1e30
LOG2E = math.log2(math.e)
HA = 16
A_QR = 1536
A_KVR = 512
A_NOPE = 128
A_ROPE = 64
A_HALF = A_ROPE // 2
A_QPAD = 256
THETA = 10000.0
HB = 32
KVB = 4
DB = 64
GB = HB // KVB
HC = 16
DC = 128
N_BUCKETS = 32
MAX_DIST = 128

LANES = 128
CHUNK = 512

CH_ACQ, CH_ACKV, CH_BQ, CH_CQ, CH_CK, CH_CV = 0, 3, 4, 8, 12, 16
CH_BZ, CH_AZ, CH_CZ = 20, 24, 28
CH_GATE = 32
CH_BKV = 56
N_CHUNKS = 57
NP = N_CHUNKS * CHUNK
BF16_SUBLANES = 16

IN_SIZES = (A_QR, A_KVR, A_ROPE, HA * A_NOPE, HB * DB, KVB * DB, KVB * DB, HB * DB,
            HC * DC, HC * DC, HC * DC, HC, HC * DC)
IN_NAMES = ("a_cq", "a_ckv", "a_kr", "a_z", "b_q", "b_k", "b_v", "b_z", "c_q", "c_k", "c_v", "c_f", "c_z")
IN_START = {name: sum(IN_SIZES[:i]) for i, name in enumerate(IN_NAMES)}
IN_WIDTH = dict(zip(IN_NAMES, IN_SIZES))
F_LANE = 64

VMEM_CAP = 60 * 1024 * 1024


def _vmem(nbytes):
    return int(min(VMEM_CAP, nbytes))


def _params(sem, nbytes):
    return pltpu.CompilerParams(dimension_semantics=sem, vmem_limit_bytes=_vmem(nbytes))


def _rope_table_kernel(pos_ref, freq_ref, cos_ref, sin_ref):
    ang = pos_ref[...].astype(F32) * freq_ref[...]
    lane = lax.broadcasted_iota(jnp.int32, ang.shape, 1)
    live = lane < A_ROPE
    s = jnp.sin(ang)
    cos_ref[...] = jnp.where(live, jnp.cos(ang), 0.0)
    sin_ref[...] = jnp.where(live, jnp.where(lane < A_HALF, -s, s), 0.0)


def _rope_tables(positions):
    t = positions.size
    tm = 2048
    half = jnp.arange(A_HALF, dtype=F32)
    inv_freq = THETA ** (-half / A_HALF)
    freq = jnp.concatenate([inv_freq, inv_freq, jnp.zeros((LANES - A_ROPE,), F32)]).reshape(1, LANES)
    pos = positions.reshape(t, 1)
    return pl.pallas_call(
        _rope_table_kernel,
        grid=(t // tm,),
        in_specs=[pl.BlockSpec((tm, 1), lambda i: (i, 0)),
                  pl.BlockSpec((1, LANES), lambda i: (0, 0))],
        out_specs=[pl.BlockSpec((tm, LANES), lambda i: (i, 0)),
                   pl.BlockSpec((tm, LANES), lambda i: (i, 0))],
        out_shape=[jax.ShapeDtypeStruct((t, LANES), F32)] * 2,
        compiler_params=_params(("arbitrary",), 32 << 20),
        name="rope_tables",
    )(pos, freq)


def _rope(t, cos, sin):
    lane = lax.broadcasted_iota(jnp.int32, t.shape, 1)
    partner = jnp.where(lane < A_HALF,
                        pltpu.roll(t, LANES - A_HALF, 1),
                        pltpu.roll(t, A_HALF, 1))
    return t * cos + partner * sin


def _relbias_kernel(table_ref, o_ref):
    h = pl.program_id(0)
    qi = lax.broadcasted_iota(jnp.int32, (SEQ_BLOCK, 2 * SEQ_BLOCK), 0)
    si = lax.broadcasted_iota(jnp.int32, (SEQ_BLOCK, 2 * SEQ_BLOCK), 1)
    dist = qi + SEQ_BLOCK - si
    max_exact = N_BUCKETS // 2
    d = jnp.maximum(dist, 0)
    large = max_exact + (jnp.log(jnp.maximum(d, 1).astype(F32) / max_exact)
                         / math.log(MAX_DIST / max_exact)
                         * (N_BUCKETS - max_exact)).astype(jnp.int32)
    large = jnp.minimum(large, N_BUCKETS - 1)
    bucket = jnp.where(d < max_exact, d, large)
    acc = jnp.zeros(bucket.shape, F32)
    for b in range(N_BUCKETS):
        acc = jnp.where(bucket == b, table_ref[b, h], acc)
    o_ref[0] = acc * LOG2E


def _rel_bias(rel_table):
    def out_map(h):
        return (h // GB, (h % GB) // 2, h % 2)
    return pl.pallas_call(
        _relbias_kernel,
        grid=(HB,),
        in_specs=[pl.BlockSpec(memory_space=pltpu.SMEM)],
        out_specs=pl.BlockSpec((1, SEQ_BLOCK, 2 * SEQ_BLOCK), out_map),
        out_shape=jax.ShapeDtypeStruct((KVB, (GB // 2) * SEQ_BLOCK, 4 * SEQ_BLOCK), F32),
        compiler_params=_params(("arbitrary",), 16 << 20),
        name="rel_bias",
    )(rel_table)


def _prenorm_kernel(x_ref, g_ref, o_ref):
    x = x_ref[...]
    r = lax.rsqrt(jnp.mean(x * x, axis=-1, keepdims=True) + NORM_EPS)
    o_ref[...] = (x * r * g_ref[...]).astype(o_ref.dtype)


def _prenorm(x2, g):
    t = x2.shape[0]
    tm = 256
    return pl.pallas_call(
        _prenorm_kernel,
        grid=(t // tm,),
        in_specs=[pl.BlockSpec((tm, D), lambda i: (i, 0)),
                  pl.BlockSpec((1, D), lambda i: (0, 0))],
        out_specs=pl.BlockSpec((tm, D), lambda i: (i, 0)),
        out_shape=jax.ShapeDtypeStruct((t, D), BF16),
        compiler_params=_params(("arbitrary",), 32 << 20),
        name="prenorm",
    )(x2, g.reshape(1, D))


ROW_CHUNK = 256


def _chunked_dot(x_ref, w_ref, o_ref, epilogue, trans_w=False):
    dims = (((1,), (1,) if trans_w else (0,)), ((), ()))
    for r in range(x_ref.shape[0] // ROW_CHUNK):
        rows = slice(r * ROW_CHUNK, (r + 1) * ROW_CHUNK)
        acc = lax.dot_general(x_ref[rows, :], w_ref[...], dims, preferred_element_type=F32)
        o_ref[rows, :] = epilogue(acc, rows).astype(o_ref.dtype)


def _proj_kernel(rows_ref, h_ref, wi_ref, wm_ref, cs_ref, o_ref, *,
                 silu_start, gate_start, gate_end):
    del rows_ref
    j = pl.program_id(1)
    wi = wi_ref.at[0]

    @pl.when((j < silu_start) | (j >= gate_end))
    def _():
        _chunked_dot(h_ref, wi, o_ref, lambda a, rows: a * cs_ref[...], trans_w=True)

    @pl.when((j >= silu_start) & (j < gate_start))
    def _():
        _chunked_dot(h_ref, wi, o_ref, lambda a, rows: a * jax.nn.sigmoid(a), trans_w=True)

    @pl.when((j >= gate_start) & (j < gate_end))
    def _():
        _chunked_dot(h_ref, wm_ref, o_ref, lambda a, rows: jax.nn.sigmoid(a))


def _source_rows():
    starts = {CH_ACQ: "a_cq", CH_ACKV: "a_ckv", CH_BQ: "b_q", CH_CQ: "c_q", CH_CK: "c_k",
              CH_CV: "c_v", CH_BZ: "b_z", CH_AZ: "a_z", CH_CZ: "c_z", CH_BKV: "b_k"}
    rows, row = [], 0
    for j in range(N_CHUNKS):
        if CH_GATE <= j < CH_BKV:
            rows.append(rows[-1])
            continue
        if j in starts:
            row = IN_START[starts[j]]
        rows.append(row)
        row += CHUNK
    return jnp.asarray(rows, jnp.int32)


def _fused_proj(h, w_t, w_merge, layer, col_scale):
    t = h.shape[0]
    tm, tn = 2048, CHUNK
    n_gate = CH_BKV - CH_GATE
    kern = functools.partial(_proj_kernel, silu_start=CH_BZ, gate_start=CH_GATE, gate_end=CH_BKV)

    def wi_map(i, j, rows):
        return (layer, pl.multiple_of(rows[j], BF16_SUBLANES), 0)

    def wm_map(i, j, rows):
        return (layer, 0, jnp.clip(j - CH_GATE, 0, n_gate - 1))

    need = 2 * (tm * D * 2 + 2 * D * tn * 2 + tm * tn * 2) + 4 * tm * tn * 4
    grid_spec = pltpu.PrefetchScalarGridSpec(
        num_scalar_prefetch=1,
        grid=(t // tm, NP // tn),
        in_specs=[pl.BlockSpec((tm, D), lambda i, j, rows: (i, 0)),
                  pl.BlockSpec((pl.Element(1), pl.Element(tn), pl.Element(D)), wi_map),
                  pl.BlockSpec((None, D, tn), wm_map),
                  pl.BlockSpec((1, tn), lambda i, j, rows: (0, j))],
        out_specs=pl.BlockSpec((tm, tn), lambda i, j, rows: (i, j)),
    )
    return pl.pallas_call(
        kern,
        grid_spec=grid_spec,
        out_shape=jax.ShapeDtypeStruct((t, NP), BF16),
        compiler_params=_params(("parallel", "arbitrary"), need + (8 << 20)),
        name="fused_proj",
    )(_source_rows(), h, w_t, w_merge, col_scale)


def _misc_kernel(h_ref, w_ref, bf_ref, cos_ref, sin_ref, kr_ref, ccol_ref, crow_ref, carry_ref):
    t = pl.program_id(1)
    acc = lax.dot_general(h_ref[...], w_ref[...], (((1,), (1,)), ((), ())),
                          preferred_element_type=F32)
    kr_ref[...] = _rope(acc, cos_ref[...], sin_ref[...]).astype(kr_ref.dtype)

    xx = acc + bf_ref[...]
    log_f = jnp.minimum(xx, 0.0) - jnp.log1p(jnp.exp(-jnp.abs(xx)))
    hi = log_f.astype(BF16)
    r1 = log_f - hi.astype(F32)
    mid = r1.astype(BF16)
    lo = (r1 - mid.astype(F32)).astype(BF16)
    tm = acc.shape[0]
    ri = lax.broadcasted_iota(jnp.int32, (tm, tm), 0)
    ci = lax.broadcasted_iota(jnp.int32, (tm, tm), 1)
    tri = jnp.where(ri >= ci, 1.0, 0.0).astype(BF16)
    cs = (jnp.dot(tri, hi, preferred_element_type=F32)
          + jnp.dot(tri, mid, preferred_element_type=F32)
          + jnp.dot(tri, lo, preferred_element_type=F32))

    @pl.when(t == 0)
    def _():
        carry_ref[...] = jnp.zeros_like(carry_ref)

    cs = cs + carry_ref[...]
    carry_ref[...] = cs[tm - 1:tm, :]
    cs2 = cs * LOG2E
    ccol_ref[...] = cs2
    crow_ref[0] = cs2.T


def _misc_proj(h, w_misc, layer, bf_row, cos_t, sin_t, batch, seq):
    t = h.shape[0]
    tm = 512
    nt = seq // tm
    need = 2 * (tm * D * 2 + D * LANES * 2 + 5 * tm * LANES * 4) + 4 * tm * tm * 4
    return pl.pallas_call(
        _misc_kernel,
        grid=(batch, nt),
        in_specs=[pl.BlockSpec((tm, D), lambda b, i: (b * nt + i, 0)),
                  pl.BlockSpec((None, LANES, D), lambda b, i: (layer, 0, 0)),
                  pl.BlockSpec((1, LANES), lambda b, i: (0, 0)),
                  pl.BlockSpec((tm, LANES), lambda b, i: (b * nt + i, 0)),
                  pl.BlockSpec((tm, LANES), lambda b, i: (b * nt + i, 0))],
        out_specs=[pl.BlockSpec((tm, LANES), lambda b, i: (b * nt + i, 0)),
                   pl.BlockSpec((tm, LANES), lambda b, i: (b * nt + i, 0)),
                   pl.BlockSpec((1, LANES, tm), lambda b, i: (b, 0, i))],
        out_shape=[jax.ShapeDtypeStruct((t, LANES), BF16),
                   jax.ShapeDtypeStruct((t, LANES), F32),
                   jax.ShapeDtypeStruct((batch, LANES, seq), F32)],
        scratch_shapes=[pltpu.VMEM((1, LANES), F32)],
        compiler_params=_params(("arbitrary", "arbitrary"), need + (8 << 20)),
        name="misc_proj",
    )(h, w_misc, bf_row, cos_t, sin_t)


def _latent_kernel(*refs, rope, out_scale):
    if rope:
        x_ref, g_ref, w_ref, cos_ref, sin_ref, o_ref, xn_ref = refs
    else:
        x_ref, g_ref, w_ref, o_ref, xn_ref = refs

    @pl.when(pl.program_id(1) == 0)
    def _():
        x = x_ref[...].astype(F32)
        r = lax.rsqrt(jnp.mean(x * x, axis=-1, keepdims=True) + NORM_EPS)
        xn_ref[...] = (x * r * g_ref[...]).astype(xn_ref.dtype)

    def rotary(acc, rows):
        cos = cos_ref[rows, :]
        sin = sin_ref[rows, :]
        pieces = []
        for hh in range(acc.shape[1] // A_QPAD):
            base = hh * A_QPAD
            pieces.append(acc[:, base:base + A_NOPE])
            pieces.append(_rope(acc[:, base + A_NOPE:base + A_QPAD], cos, sin))
        return jnp.concatenate(pieces, axis=1) * out_scale

    _chunked_dot(xn_ref, w_ref, o_ref, rotary if rope else (lambda acc, rows: acc))


def _latent_proj(proj, col_block, k, g, w, tables, out_scale):
    t = proj.shape[0]
    n = w.shape[1]
    tm, tn = 1024, 2048
    rope = tables is not None
    kern = functools.partial(_latent_kernel, rope=rope, out_scale=out_scale)
    in_specs = [pl.BlockSpec((tm, k), lambda i, j: (i, col_block)),
                pl.BlockSpec((1, k), lambda i, j: (0, 0)),
                pl.BlockSpec((k, tn), lambda i, j: (0, j))]
    args = [proj, g.reshape(1, k), w]
    if rope:
        in_specs += [pl.BlockSpec((tm, LANES), lambda i, j: (i, 0))] * 2
        args += list(tables)
    need = 2 * (tm * k * 2 + k * tn * 2 + tm * tn * 2 + 2 * tm * LANES * 4) + tm * k * 2 + 4 * tm * tn * 4
    return pl.pallas_call(
        kern,
        grid=(t // tm, n // tn),
        in_specs=in_specs,
        out_specs=pl.BlockSpec((tm, tn), lambda i, j: (i, j)),
        out_shape=jax.ShapeDtypeStruct((t, n), BF16),
        scratch_shapes=[pltpu.VMEM((tm, k), BF16)],
        compiler_params=_params(("parallel", "arbitrary"), need + (8 << 20)),
        name="latent_q" if rope else "latent_kv",
    )(*args)


def _causal_attn_kernel(*refs, tq, fox, heads):
    if fox:
        q_ref, k_ref, v_ref, z_ref, ccol_ref, crow_ref, o_ref, vaug_ref = refs
    else:
        q_ref, kn_ref, kr_ref, v_ref, z_ref, o_ref, kcat_ref, vaug_ref = refs
    seq = q_ref.shape[0]
    dk = q_ref.shape[1] // heads
    lane = lax.broadcasted_iota(jnp.int32, (seq, LANES), 1)
    ones_col = jnp.where(lane == 0, 1.0, 0.0).astype(BF16)
    keys, cq_all, ck_all = [], [], []
    for hh in range(heads):
        cols = slice(hh * LANES, (hh + 1) * LANES)
        vaug_ref[hh, :, 0:LANES] = v_ref[:, cols]
        vaug_ref[hh, :, LANES:2 * LANES] = ones_col
        if fox:
            keys.append(k_ref.at[:, cols])
            h = pl.program_id(1) * heads + hh
            cq_all.append(jnp.sum(jnp.where(lane == F_LANE + h, ccol_ref[...], 0.0),
                                  axis=1, keepdims=True))
            ck_all.append(crow_ref[0, pl.ds(F_LANE + h, 1), :])
        else:
            kcat_ref[hh, :, 0:LANES] = kn_ref[:, cols]
            kcat_ref[hh, :, LANES:2 * LANES] = kr_ref[...]
            keys.append(kcat_ref.at[hh])
    row = lax.broadcasted_iota(jnp.int32, (tq, tq), 0)
    col = lax.broadcasted_iota(jnp.int32, (tq, tq), 1)
    causal = row >= col
    dims = (((1,), (1,)), ((), ()))

    def logits(item):
        hh, c = item
        lo, hi = c * tq, (c + 1) * tq
        q = q_ref[lo:hi, hh * dk:(hh + 1) * dk]
        t_d = lax.dot_general(q, keys[hh][lo:hi, :], dims, preferred_element_type=F32)
        if fox:
            t_d = t_d - ck_all[hh][:, lo:hi]
        t_d = jnp.where(causal, t_d, MASKED)
        t_o = None
        if c:
            t_o = lax.dot_general(q, keys[hh][0:lo, :], dims, preferred_element_type=F32)
            if fox:
                t_o = t_o - ck_all[hh][:, 0:lo]
        return t_d, t_o

    nq = seq // tq
    order = [(hh, c) for hh in range(heads) for c in range(nq - 1, -1, -1)]
    queue = [logits(item) for item in order[:LOOKAHEAD]]
    for i, (hh, c) in enumerate(order):
        lo, hi = c * tq, (c + 1) * tq
        cols = slice(hh * LANES, (hh + 1) * LANES)
        t_d, t_o = queue.pop(0)
        if i + LOOKAHEAD < len(order):
            queue.append(logits(order[i + LOOKAHEAD]))
        mt = jnp.max(t_d, axis=1, keepdims=True)
        if c:
            mt = jnp.maximum(mt, jnp.max(t_o, axis=1, keepdims=True))
        if fox:
            cq = cq_all[hh][lo:hi, :]
            shift = (mt + cq) - cq
        else:
            shift = mt
        acc = jnp.dot(jnp.exp2(t_d - shift).astype(BF16), vaug_ref[hh, lo:hi, :],
                      preferred_element_type=F32)
        if c:
            acc = acc + jnp.dot(jnp.exp2(t_o - shift).astype(BF16), vaug_ref[hh, 0:lo, :],
                                preferred_element_type=F32)
        inv_l = 1.0 / acc[:, LANES:LANES + 1]
        o_ref[lo:hi, cols] = (acc[:, 0:LANES] * inv_l
                              * z_ref[lo:hi, cols].astype(F32)).astype(o_ref.dtype)


ATTN_TQ = 256
LOOKAHEAD = 2
ATTN_HEADS = 4


def _attention_a(q_a, kv_a, k_rope, proj, batch, seq):
    t = q_a.shape[0]
    hs = ATTN_HEADS
    kern = functools.partial(_causal_attn_kernel, tq=ATTN_TQ, fox=False, heads=hs)
    z_col = CH_AZ * CHUNK // LANES
    need = 2 * (hs * seq * A_QPAD * 2 + (4 * hs + 1) * seq * LANES * 2) + 2 * hs * seq * A_QPAD * 2
    wide = pl.BlockSpec((seq, hs * LANES), lambda b, h: (b, h))
    return pl.pallas_call(
        kern,
        grid=(batch, HA // hs),
        in_specs=[pl.BlockSpec((seq, hs * A_QPAD), lambda b, h: (b, h)),
                  wide,
                  pl.BlockSpec((seq, LANES), lambda b, h: (b, 0)),
                  pl.BlockSpec((seq, hs * LANES), lambda b, h: (b, HA // hs + h)),
                  pl.BlockSpec((seq, hs * LANES), lambda b, h: (b, z_col // hs + h))],
        out_specs=wide,
        out_shape=jax.ShapeDtypeStruct((t, HA * LANES), BF16),
        scratch_shapes=[pltpu.VMEM((hs, seq, A_QPAD), BF16), pltpu.VMEM((hs, seq, 2 * LANES), BF16)],
        compiler_params=_params(("parallel", "arbitrary"), need + (24 << 20)),
        name="attn_mla",
    )(q_a, kv_a, k_rope, kv_a, proj)


def _attention_c(proj, ccol, crow, batch, seq):
    t = proj.shape[0]
    hs = ATTN_HEADS
    kern = functools.partial(_causal_attn_kernel, tq=ATTN_TQ, fox=True, heads=hs)
    q_col, k_col, v_col, z_col = (c * CHUNK // (hs * LANES) for c in (CH_CQ, CH_CK, CH_CV, CH_CZ))
    need = 2 * (5 * hs * seq * LANES * 2 + 2 * seq * LANES * 4) + hs * seq * 2 * LANES * 2

    def head_cols(first):
        return pl.BlockSpec((seq, hs * LANES), lambda b, h: (b, first + h))

    return pl.pallas_call(
        kern,
        grid=(batch, HC // hs),
        in_specs=[head_cols(q_col), head_cols(k_col), head_cols(v_col), head_cols(z_col),
                  pl.BlockSpec((seq, LANES), lambda b, h: (b, 0)),
                  pl.BlockSpec((1, LANES, seq), lambda b, h: (b, 0, 0))],
        out_specs=head_cols(0),
        out_shape=jax.ShapeDtypeStruct((t, HC * DC), BF16),
        scratch_shapes=[pltpu.VMEM((hs, seq, 2 * LANES), BF16)],
        compiler_params=_params(("parallel", "arbitrary"), need + (24 << 20)),
        name="attn_fox",
    )(proj, proj, proj, proj, ccol, crow)


SWA_BLOCKS = 2


def _swa_kernel(q_ref, kvp_ref, kvc_ref, z_ref, bias_ref, sink_ref, o_ref, vaug_ref):
    n0 = pl.program_id(1) * SWA_BLOCKS
    blk = SEQ_BLOCK
    band = 2 * blk
    rows = (GB // 2) * blk
    kv = jnp.concatenate([kvp_ref[...], kvc_ref[...]], axis=0).astype(F32)
    lane = lax.broadcasted_iota(jnp.int32, (kv.shape[0], LANES), 1)
    low = lane < DB

    def split(two_heads):
        lo = jnp.where(low, two_heads, 0.0)
        hi = jnp.where(low, 0.0, two_heads)
        return (lo, pltpu.roll(lo, DB, 1)), (pltpu.roll(hi, DB, 1), hi)

    k_heads = split(kv[:, 0:LANES]) + split(kv[:, LANES:2 * LANES])
    v_heads = split(kv[:, 2 * LANES:3 * LANES]) + split(kv[:, 3 * LANES:4 * LANES])

    r = lax.broadcasted_iota(jnp.int32, (rows, band), 0) % blk
    c = lax.broadcasted_iota(jnp.int32, (rows, band), 1)
    dist = r + blk - c
    in_window = (dist >= 0) & (dist < blk)
    valid = [in_window & (c >= jnp.where(n0 + u > 0, 0, blk)) for u in range(SWA_BLOCKS)]

    r2 = lax.broadcasted_iota(jnp.int32, (2 * band, LANES), 0)
    l2 = lax.broadcasted_iota(jnp.int32, (2 * band, LANES), 1)
    ones_cols = jnp.where(((l2 == 0) & (r2 < band)) | ((l2 == 1) & (r2 >= band)), 1.0, 0.0)
    units = [(u, kh) for u in range(SWA_BLOCKS) for kh in range(KVB)]
    for i, (u, kh) in enumerate(units):
        keys = slice(u * blk, u * blk + band)
        vaug_ref[i, :, LANES:2 * LANES] = ones_cols.astype(BF16)
        vaug_ref[i, :, 0:LANES] = jnp.concatenate([half[keys] for half in v_heads[kh]],
                                                  axis=0).astype(BF16)
    even_lanes = lax.broadcasted_iota(jnp.int32, (rows, LANES), 1) < DB

    def logits(u, kh):
        keys = slice(u * blk, u * blk + band)
        k_cat = jnp.concatenate([half[keys] for half in k_heads[kh]], axis=0).astype(BF16)
        q4 = jnp.concatenate(
            [q_ref[u * blk:(u + 1) * blk, (kh * (GB // 2) + jj) * LANES:(kh * (GB // 2) + jj + 1) * LANES]
             for jj in range(GB // 2)], axis=0)
        s = lax.dot_general(q4, k_cat, (((1,), (1,)), ((), ())), preferred_element_type=F32)
        return s + bias_ref[kh]

    ids = range(len(units))
    s_all = [logits(u, kh) for u, kh in units]
    se = [[jnp.where(valid[units[i][0]], s_all[i][:, e * band:(e + 1) * band], MASKED)
           for e in range(2)] for i in ids]
    sink = [[sink_ref[kh, :, e:e + 1] for e in range(2)] for _, kh in units]
    m = [[jnp.maximum(jnp.max(se[i][e], axis=1, keepdims=True), sink[i][e]) for e in range(2)]
         for i in ids]
    p = [jnp.concatenate([jnp.exp2(se[i][e] - m[i][e]).astype(BF16) for e in range(2)], axis=1)
         for i in ids]
    pv = [jnp.dot(p[i], vaug_ref[i], preferred_element_type=F32) for i in ids]
    for i, (u, kh) in enumerate(units):
        inv_even = 1.0 / (pv[i][:, LANES:LANES + 1] + jnp.exp2(sink[i][0] - m[i][0]))
        inv_odd = 1.0 / (pv[i][:, LANES + 1:LANES + 2] + jnp.exp2(sink[i][1] - m[i][1]))
        o = pv[i][:, 0:LANES] * jnp.where(even_lanes, inv_even, inv_odd)
        qrows = slice(u * blk, (u + 1) * blk)
        for jj in range(GB // 2):
            c0 = (kh * (GB // 2) + jj) * LANES
            o_ref[qrows, c0:c0 + LANES] = (o[jj * blk:(jj + 1) * blk, :]
                                           * z_ref[qrows, c0:c0 + LANES].astype(F32)).astype(o_ref.dtype)


def _attention_b(proj, bias, sink_cols, batch, seq):
    t = proj.shape[0]
    blk = SEQ_BLOCK
    nb = seq // blk
    step = SWA_BLOCKS * blk
    ns = seq // step
    width = HB * DB
    q_col = CH_BQ * CHUNK // width
    kv_col = CH_BKV
    z_col = CH_BZ * CHUNK // width
    assert CH_BQ * CHUNK % width == 0 and CH_BZ * CHUNK % width == 0
    need = 2 * (3 * step * width * 2 + 3 * step * CHUNK * 2 + bias.size * 4 + sink_cols.size * 4)
    return pl.pallas_call(
        _swa_kernel,
        grid=(batch, ns),
        in_specs=[pl.BlockSpec((step, width), lambda b, n: (b * ns + n, q_col)),
                  pl.BlockSpec((blk, CHUNK),
                               lambda b, n: (b * nb + jnp.maximum(n * SWA_BLOCKS - 1, 0), kv_col)),
                  pl.BlockSpec((step, CHUNK), lambda b, n: (b * ns + n, kv_col)),
                  pl.BlockSpec((step, width), lambda b, n: (b * ns + n, z_col)),
                  pl.BlockSpec(bias.shape, lambda b, n: (0, 0, 0)),
                  pl.BlockSpec(sink_cols.shape, lambda b, n: (0, 0, 0))],
        out_specs=pl.BlockSpec((step, width), lambda b, n: (b * ns + n, 0)),
        out_shape=jax.ShapeDtypeStruct((t, width), BF16),
        scratch_shapes=[pltpu.VMEM((SWA_BLOCKS * KVB, 4 * blk, 2 * LANES), BF16)],
        compiler_params=_params(("parallel", "arbitrary"), need + (32 << 20)),
        name="attn_swa",
    )(proj, proj, proj, proj, bias, sink_cols)


def _merge_kernel(oa_ref, ob_ref, oc_ref, wa_ref, wb_ref, wc_ref, ga_ref, gb_ref, gc_ref, y_ref):
    branches = ((oa_ref, wa_ref, ga_ref), (ob_ref, wb_ref, gb_ref), (oc_ref, wc_ref, gc_ref))
    for r in range(y_ref.shape[0] // ROW_CHUNK):
        rows = slice(r * ROW_CHUNK, (r + 1) * ROW_CHUNK)
        y = None
        for o_ref, w_ref, g_ref in branches:
            term = g_ref[rows, :].astype(F32) * jnp.dot(o_ref[rows, :], w_ref[...],
                                                        preferred_element_type=F32)
            y = term if y is None else y + term
        y_ref[rows, :] = y.astype(y_ref.dtype)


def _merge_proj(oa, ob, oc, wa, wb, wc, layer, proj):
    t, k = oa.shape
    tm, tn = 1024, CHUNK
    gcols = D // tn
    o_spec = pl.BlockSpec((tm, k), lambda i, j: (i, 0))
    w_spec = pl.BlockSpec((None, k, tn), lambda i, j: (layer, 0, j))

    def gate_spec(branch):
        return pl.BlockSpec((tm, tn), lambda i, j: (i, CH_GATE + branch * gcols + j))

    need = 2 * (3 * tm * k * 2 + 3 * k * tn * 2 + 4 * tm * tn * 2) + 4 * tm * tn * 4
    return pl.pallas_call(
        _merge_kernel,
        grid=(t // tm, D // tn),
        in_specs=[o_spec, o_spec, o_spec, w_spec, w_spec, w_spec,
                  gate_spec(0), gate_spec(1), gate_spec(2)],
        out_specs=pl.BlockSpec((tm, tn), lambda i, j: (i, j)),
        out_shape=jax.ShapeDtypeStruct((t, D), BF16),
        compiler_params=_params(("parallel", "arbitrary"), need + (8 << 20)),
        name="merge_proj",
    )(oa, ob, oc, wa, wb, wc, proj, proj, proj)


def _out_kernel(y_ref, w_ref, x_ref, g_ref, o_ref, acc_ref, ssq_ref):
    i = pl.program_id(0)
    j = pl.program_id(1)
    slot = i % 2
    prev = 1 - slot

    @pl.when((i == 0) & (j == 0))
    def _():
        acc_ref[...] = jnp.zeros_like(acc_ref)
        ssq_ref[...] = jnp.zeros_like(ssq_ref)

    def finish():
        rs = lax.rsqrt(ssq_ref[prev] * (1.0 / D) + NORM_EPS)
        o_ref[...] = x_ref[...] + acc_ref[j] * rs * g_ref[...]

    @pl.when(i < pl.num_programs(0) - 1)
    def _():
        finish()
        ss = jnp.where(j == 0, 0.0, ssq_ref[slot])
        for r in range(y_ref.shape[0] // ROW_CHUNK):
            rows = slice(r * ROW_CHUNK, (r + 1) * ROW_CHUNK)
            acc = jnp.dot(y_ref[rows, :], w_ref[...], preferred_element_type=F32)
            acc_ref[j, rows, :] = acc
            ssq_ref[slot, rows, :] = ss[rows, :] + jnp.sum(acc * acc, axis=1, keepdims=True)

    @pl.when(i == pl.num_programs(0) - 1)
    def _():
        finish()


def _out_proj(y, w_o, layer, x2, g):
    t = y.shape[0]
    tm, tn = 1024, CHUNK
    ni, nj = t // tm, D // tn
    need = 2 * (tm * D * 2 + D * tn * 2 + 2 * tm * tn * 4) + tm * D * 4 + 4 * tm * tn * 4
    return pl.pallas_call(
        _out_kernel,
        grid=(ni + 1, nj),
        in_specs=[pl.BlockSpec((tm, D), lambda i, j: (jnp.minimum(i, ni - 1), 0)),
                  pl.BlockSpec((None, D, tn), lambda i, j: (layer, 0, j)),
                  pl.BlockSpec((tm, tn), lambda i, j: (jnp.maximum(i - 1, 0), j)),
                  pl.BlockSpec((1, tn), lambda i, j: (0, j))],
        out_specs=pl.BlockSpec((tm, tn), lambda i, j: (jnp.maximum(i - 1, 0), j)),
        out_shape=jax.ShapeDtypeStruct((t, D), F32),
        scratch_shapes=[pltpu.VMEM((nj, tm, tn), F32), pltpu.VMEM((2, tm, 1), F32)],
        compiler_params=_params(("arbitrary", "arbitrary"), need + (8 << 20)),
        name="out_proj",
    )(y, w_o, x2, g.reshape(1, D))


def _cast_kernel(x_ref, o_ref):
    o_ref[...] = x_ref[...].astype(o_ref.dtype)


def _to_bf16(w):
    nl, r, c = w.shape
    tr, tc = 512, 2048
    spec = pl.BlockSpec((None, tr, tc), lambda l, i, j: (l, i, j))
    return pl.pallas_call(
        _cast_kernel,
        grid=(nl, r // tr, c // tc),
        in_specs=[spec],
        out_specs=spec,
        out_shape=jax.ShapeDtypeStruct(w.shape, BF16),
        compiler_params=_params(("arbitrary",) * 3, 6 * tr * tc * 4),
        name="cast_bf16",
    )(w)


def _transposed_w_in(w_in):
    w_t = jnp.swapaxes(w_in, 1, 2).astype(BF16)

    def rows(name):
        return w_t[:, IN_START[name]:IN_START[name] + IN_WIDTH[name], :]

    pad = jnp.zeros((w_in.shape[0], LANES - F_LANE - HC, D), BF16)
    misc = jnp.concatenate([rows("a_kr"), rows("c_f"), pad], axis=1)
    return w_t, misc


def _layer_weights(w_uq, w_uk, w_uv, b_f):
    col_scale = jnp.ones((NP,), F32)
    col_scale = col_scale.at[CH_BQ * CHUNK:CH_CQ * CHUNK].set(DB ** -0.5 * LOG2E)
    col_scale = col_scale.at[CH_CQ * CHUNK:CH_CK * CHUNK].set(DC ** -0.5 * LOG2E)
    col_scale = col_scale.reshape(1, NP)
    bf_row = jnp.concatenate([jnp.zeros((F_LANE,), F32), b_f.astype(F32),
                              jnp.zeros((LANES - F_LANE - HC,), F32)]).reshape(1, LANES)
    wq = jnp.pad(w_uq, ((0, 0), (0, 0), (0, A_QPAD - A_NOPE - A_ROPE)))
    wq = wq.reshape(A_QR, HA * A_QPAD).astype(BF16)
    wkv = jnp.concatenate([w_uk.reshape(A_KVR, HA * A_NOPE), w_uv.reshape(A_KVR, HA * A_NOPE)],
                          axis=1).astype(BF16)
    return col_scale, bf_row, wq, wkv


def _sink_columns(sinks):
    s = (sinks.astype(F32) * LOG2E).reshape(KVB, GB // 2, 1, 2)
    s = jnp.broadcast_to(s, (KVB, GB // 2, SEQ_BLOCK, 2)).reshape(KVB, (GB // 2) * SEQ_BLOCK, 2)
    return jnp.pad(s, ((0, 0), (0, 0), (0, LANES - 2)))


def kernel(x, positions, rel_table, pre_norm, w_in, q_a_norm, kv_a_norm, w_uq, w_uk, w_uv,
           sinks, b_f, w_proj_a, w_proj_b, w_proj_c, w_merge, w_o, post_norm):
    batch, seq, _ = x.shape
    depth = w_in.shape[0]
    x2 = x.reshape(batch * seq, D)
    tables = _rope_tables(positions)
    bias = _rel_bias(rel_table)
    wm16, wo16 = _to_bf16(w_merge), w_o.astype(BF16)
    wa16, wb16, wc16 = (w.astype(BF16) for w in (w_proj_a, w_proj_b, w_proj_c))
    w_inr, w_misc = _transposed_w_in(w_in)
    for l in range(depth):
        col_scale, bf_row, wq, wkv = _layer_weights(w_uq[l], w_uk[l], w_uv[l], b_f[l])
        h = _prenorm(x2, pre_norm[l])
        proj = _fused_proj(h, w_inr, wm16, l, col_scale)
        k_rope, ccol, crow = _misc_proj(h, w_misc, l, bf_row, tables[0], tables[1], batch, seq)
        q_a = _latent_proj(proj, 0, A_QR, q_a_norm[l], wq, tables,
                           (A_NOPE + A_ROPE) ** -0.5 * LOG2E)
        kv_a = _latent_proj(proj, CH_ACKV, A_KVR, kv_a_norm[l], wkv, None, 1.0)
        o_a = _attention_a(q_a, kv_a, k_rope, proj, batch, seq)
        o_b = _attention_b(proj, bias, _sink_columns(sinks[l]), batch, seq)
        o_c = _attention_c(proj, ccol, crow, batch, seq)
        y = _merge_proj(o_a, o_b, o_c, wa16, wb16, wc16, l, proj)
        x2 = _out_proj(y, wo16, l, x2, post_norm[l])
    return x2.reshape(batch, seq, D)
```
